```python
import jax, jax.numpy as jnp
from jax import lax
import numpy as np

D_MODEL = 1024
BATCH = 8
SEQ = 4096
DEPTH = 1

HEAD_DIM = 64
GRID_W = 64
NA_HEADS = 8
NA_WIN_ROWS = 8
NA_WIN_COLS = 16
NA_QBLOCK_COLS = 16
NA_KBLOCK_COLS = 32
DIL_GROUPS = ((128, 1), (512, 4), (2048, 16))
DIL_HEADS_PER_GROUP = 4
DIL_HEADS = DIL_HEADS_PER_GROUP * len(DIL_GROUPS)
NA_WIDTH = NA_HEADS * HEAD_DIM
DIL_WIDTH = DIL_HEADS * HEAD_DIM
DIL_OUT_WIDTH = DIL_HEADS_PER_GROUP * HEAD_DIM
IN_WIDTHS = (NA_WIDTH, NA_WIDTH, NA_WIDTH, DIL_WIDTH, DIL_WIDTH, DIL_WIDTH, D_MODEL, D_MODEL)
IN_WIDTH = sum(IN_WIDTHS)
D_FF = 4 * D_MODEL
PLE_DIM = 256
ROPE_THETA = 10000.0
RMS_EPS = 1e-6
NEG_INF = -1e30

kernel_name = "hybrid_na_dilated_gated_encoder"


def rms_norm(x, g):
    xf = x.astype(jnp.float32)
    y = xf * lax.rsqrt(jnp.mean(xf * xf, axis=-1, keepdims=True) + RMS_EPS)
    return (y * g.astype(jnp.float32)).astype(x.dtype)


def rotary(x, positions):
    half = HEAD_DIM // 2
    inv_freq = ROPE_THETA ** (-jnp.arange(half, dtype=jnp.float32) / half)
    ang = positions.astype(jnp.float32)[:, None, :, None] * inv_freq
    cos, sin = jnp.cos(ang), jnp.sin(ang)
    xf = x.astype(jnp.float32)
    x1, x2 = xf[..., :half], xf[..., half:]
    return jnp.concatenate([x1 * cos - x2 * sin, x2 * cos + x1 * sin], axis=-1).astype(x.dtype)


def neighborhood_attention(q, k, v, rpb):
    b, h, s, dh = q.shape
    rows = s // GRID_W
    wr = min(NA_WIN_ROWS, rows)
    n_cb = GRID_W // NA_QBLOCK_COLS
    r = np.arange(rows)
    rs = np.clip(r - wr // 2, 0, rows - wr)
    row_idx = rs[:, None] + np.arange(wr)[None, :]
    row_off = row_idx - r[:, None] + (NA_WIN_ROWS - 1)
    c = np.arange(GRID_W)
    cs = np.clip(c - NA_WIN_COLS // 2, 0, GRID_W - NA_WIN_COLS)
    cb = np.arange(n_cb)
    kc0 = np.clip(cb * NA_QBLOCK_COLS - NA_WIN_COLS // 2, 0, GRID_W - NA_KBLOCK_COLS)
    col_idx = kc0[:, None] + np.arange(NA_KBLOCK_COLS)[None, :]
    qcol = cb[:, None] * NA_QBLOCK_COLS + np.arange(NA_QBLOCK_COLS)[None, :]
    kcol = col_idx[:, None, :]
    qs = cs[qcol][:, :, None]
    col_valid = (kcol >= qs) & (kcol < qs + NA_WIN_COLS)
    col_off = np.clip(kcol - qcol[:, :, None], -(NA_WIN_COLS - 1), NA_WIN_COLS - 1) + (NA_WIN_COLS - 1)
    bias = jnp.take(rpb[:, row_off].astype(jnp.float32), col_off, axis=-1)
    bias = jnp.where(col_valid, bias, NEG_INF).transpose(0, 1, 3, 4, 2, 5)

    qg = (q * (dh ** -0.5)).reshape(b, h, rows, n_cb, NA_QBLOCK_COLS, dh)
    kc = jnp.take(k.reshape(b, h, rows, GRID_W, dh), col_idx, axis=3)
    vc = jnp.take(v.reshape(b, h, rows, GRID_W, dh), col_idx, axis=3)
    scores = jnp.stack(
        [jnp.einsum('bhrcqd,bhrckd->bhrcqk', qg, jnp.take(kc, row_idx[:, i], axis=2)).astype(jnp.float32)
         for i in range(wr)], axis=-2)
    scores = scores + bias[None]
    probs = jax.nn.softmax(scores.reshape(*scores.shape[:-2], wr * NA_KBLOCK_COLS), axis=-1)
    probs = probs.reshape(scores.shape).astype(v.dtype)
    out = jnp.einsum('bhrcqk,bhrckd->bhrcqd', probs[..., 0, :], jnp.take(vc, row_idx[:, 0], axis=2))
    for i in range(1, wr):
        out = out + jnp.einsum('bhrcqk,bhrckd->bhrcqd', probs[..., i, :], jnp.take(vc, row_idx[:, i], axis=2))
    return out.reshape(b, h, s, dh)


def banded_attention(q, k, v, radius):
    *lead, L, dh = q.shape
    blk = radius
    nb = -(-L // blk)
    lp = nb * blk
    nlead = len(lead)
    qb = jnp.pad(q, [(0, 0)] * nlead + [(0, lp - L), (0, 0)]).reshape(*lead, nb, blk, dh)

    def band(t):
        tb = jnp.pad(t, [(0, 0)] * nlead + [(blk, lp - L + blk), (0, 0)]).reshape(*lead, nb + 2, blk, dh)
        return jnp.concatenate([tb[..., :-2, :, :], tb[..., 1:-1, :, :], tb[..., 2:, :, :]], axis=-2)

    kb, vb = band(k), band(v)
    s = jnp.einsum('...nqd,...nkd->...nqk', qb, kb).astype(jnp.float32) * (dh ** -0.5)
    qi = np.arange(lp).reshape(nb, blk)[:, :, None]
    kj = np.arange(nb)[:, None, None] * blk - blk + np.arange(3 * blk)[None, None, :]
    valid = (kj >= 0) & (kj < L) & (np.abs(qi - kj) <= radius)
    s = jnp.where(valid, s, NEG_INF)
    lse = jax.nn.logsumexp(s, axis=-1)
    pr = jnp.exp(s - lse[..., None]).astype(v.dtype)
    o = jnp.einsum('...nqk,...nkd->...nqd', pr, vb)
    return o.reshape(*lead, lp, dh)[..., :L, :], lse.reshape(*lead, lp)[..., :L]


def dilated_attention(q, k, v):
    b, _, s, dh = q.shape
    hg = DIL_HEADS_PER_GROUP
    outs, lses = [], []
    for g, (window, dil) in enumerate(DIL_GROUPS):
        radius = window // (2 * dil)

        def split(t):
            return t[:, g * hg:(g + 1) * hg].reshape(b, hg, s // dil, dil, dh).swapaxes(2, 3)

        o, lse = banded_attention(split(q), split(k), split(v), radius)
        outs.append(o.swapaxes(2, 3).reshape(b, hg, s, dh))
        lses.append(lse.swapaxes(2, 3).reshape(b, hg, s))
    w = jax.nn.softmax(jnp.stack(lses), axis=0).astype(q.dtype)
    return jnp.einsum('gbhs,gbhsd->bhsd', w, jnp.stack(outs))


def setup_inputs(seed: int = 0) -> dict:
    key = jax.random.key(seed)
    ks = jax.random.split(key, 20)
    f32 = jnp.float32

    def nrm(k, shape, fan_in):
        return jax.random.normal(k, shape, f32) * (fan_in ** -0.5)

    def gain(k, shape):
        return 1.0 + 0.01 * jax.random.normal(k, shape, f32)

    return {
        "x": jax.random.normal(ks[0], (BATCH, SEQ, D_MODEL), f32),
        "p": jax.random.normal(ks[1], (DEPTH, BATCH, SEQ, PLE_DIM), f32),
        "positions": (jnp.arange(SEQ, dtype=jnp.int32)[None, :]
                      + jax.random.randint(ks[2], (BATCH, 1), 0, 1024, dtype=jnp.int32)),
        "g_mix": gain(ks[3], (DEPTH, D_MODEL)),
        "w_in": nrm(ks[4], (DEPTH, D_MODEL, IN_WIDTH), D_MODEL),
        "rpb": 0.02 * jax.random.normal(ks[5], (DEPTH, NA_HEADS, 2 * NA_WIN_ROWS - 1, 2 * NA_WIN_COLS - 1), f32),
        "w_branch_na": nrm(ks[6], (DEPTH, NA_WIDTH, D_MODEL), NA_WIDTH),
        "w_branch_dil": nrm(ks[7], (DEPTH, DIL_OUT_WIDTH, D_MODEL), DIL_OUT_WIDTH),
        "w_out": nrm(ks[8], (DEPTH, D_MODEL, D_MODEL), D_MODEL),
        "g_mlp": gain(ks[9], (DEPTH, D_MODEL)),
        "w_up": nrm(ks[10], (DEPTH, D_MODEL, D_FF), D_MODEL),
        "w_down": nrm(ks[11], (DEPTH, D_FF, D_MODEL), D_FF),
        "g_ple": gain(ks[12], (DEPTH, D_MODEL)),
        "w_ple_gate": nrm(ks[13], (DEPTH, D_MODEL, D_MODEL), D_MODEL),
        "w_ple_proj": nrm(ks[14], (DEPTH, PLE_DIM, D_MODEL), PLE_DIM),
        "g_final": gain(ks[15], (D_MODEL,)),
    }


def reference(x, p, positions, g_mix, w_in, rpb, w_branch_na, w_branch_dil, w_out, g_mlp, w_up, w_down,
              g_ple, w_ple_gate, w_ple_proj, g_final):
    b, s, _ = x.shape
    split_points = [int(v) for v in np.cumsum(IN_WIDTHS)[:-1]]

    def heads(t, n):
        return t.reshape(b, s, n, HEAD_DIM).transpose(0, 2, 1, 3)

    def merge(t):
        return t.transpose(0, 2, 1, 3).reshape(b, s, -1)

    h = x
    for i in range(DEPTH):
        a = rms_norm(h, g_mix[i])
        qa, ka, va, qd, kd, vd, gate_na, gate_dil = jnp.split(a @ w_in[i], split_points, axis=-1)
        y_na = merge(neighborhood_attention(heads(qa, NA_HEADS), heads(ka, NA_HEADS), heads(va, NA_HEADS), rpb[i]))
        y_dil = merge(dilated_attention(rotary(heads(qd, DIL_HEADS), positions),
                                        rotary(heads(kd, DIL_HEADS), positions),
                                        heads(vd, DIL_HEADS)))
        mixed = (jax.nn.sigmoid(gate_na) * (y_na @ w_branch_na[i])
                 + jax.nn.sigmoid(gate_dil) * (y_dil @ w_branch_dil[i]))
        h = h + mixed @ w_out[i]
        c = rms_norm(h, g_mlp[i])
        h = h + jnp.square(jax.nn.relu(c @ w_up[i])) @ w_down[i]
        e = rms_norm(h, g_ple[i])
        h = h + jax.nn.sigmoid(e @ w_ple_gate[i]) * (p[i] @ w_ple_proj[i])
    return rms_norm(h, g_final)
```

```python
import functools

import jax
import jax.numpy as jnp
import numpy as np
from jax import lax
from jax.experimental import pallas as pl
from jax.experimental.pallas import tpu as pltpu

HEAD_DIM = 64
GRID_W = 64
NA_HEADS = 8
NA_WIN_ROWS = 8
NA_WIN_COLS = 16
DIL_GROUPS = ((128, 1), (512, 4), (2048, 16))
DIL_HEADS_PER_GROUP = 4
ROPE_THETA = 10000.0
RMS_EPS = 1e-6
NEG_INF = -1e30

LANES = 256
HEADS_PER_CALL = LANES // HEAD_DIM
NA_WIDTH = NA_HEADS * HEAD_DIM
DIL_WIDTH = DIL_HEADS_PER_GROUP * len(DIL_GROUPS) * HEAD_DIM
QKV_WIDTH = 3 * NA_WIDTH + 3 * DIL_WIDTH
N_QKV_CHUNKS = QKV_WIDTH // LANES
NA_Q0, NA_K0, NA_V0 = 0, 2, 4
DIL_Q0, DIL_K0, DIL_V0 = 6, 9, 12

NA_Q_ROWS = 4
NA_K_ROWS = 12
DIL_Q = 128
DIL_K = 256
DIL_RADIUS = 64

VMEM_LIMIT = 56 * 1024 * 1024


def _rms(x, g):
    ms = jnp.mean(x * x, axis=-1, keepdims=True)
    return x * lax.rsqrt(ms + RMS_EPS) * g


def _sigmoid(x):
    return 1.0 / (1.0 + jnp.exp(-x))


def _head_of_lane(shape):
    return lax.broadcasted_iota(jnp.int32, shape, len(shape) - 1) // HEAD_DIM


def _stack_heads(q):
    head = _head_of_lane(q.shape)
    zero = jnp.zeros_like(q)
    return jnp.concatenate([jnp.where(head == h, q, zero) for h in range(HEADS_PER_CALL)], axis=0)


def _unstack_heads(x, m):
    head = _head_of_lane((m, LANES))
    out = x[0:m]
    for h in range(1, HEADS_PER_CALL):
        out = jnp.where(head == h, x[h * m:(h + 1) * m], out)
    return out


def _rope_kernel(pos_ref, freq_ref, cos_ref, sin_ref):
    ang = pos_ref[...] * freq_ref[...]
    cos_ref[...] = jnp.cos(ang)
    sin_ref[...] = jnp.sin(ang)


def _rope_tables(positions):
    half = HEAD_DIM // 2
    n = positions.size
    inv_freq = ROPE_THETA ** (-jnp.arange(half, dtype=jnp.float32) / half)
    per_row = 128 // half
    pos_rep = jnp.repeat(positions.reshape(n).astype(jnp.float32), half).reshape(n // per_row, 128)
    freq = jnp.tile(inv_freq, per_row).reshape(1, 128)
    rows = n // per_row
    blk = 1024
    cos, sin = pl.pallas_call(
        _rope_kernel,
        out_shape=(jax.ShapeDtypeStruct((rows, 128), jnp.float32),) * 2,
        grid=(rows // blk,),
        in_specs=[pl.BlockSpec((blk, 128), lambda i: (i, 0)), pl.BlockSpec((1, 128), lambda i: (0, 0))],
        out_specs=(pl.BlockSpec((blk, 128), lambda i: (i, 0)),) * 2,
        name="rope_tables",
    )(pos_rep, freq)
    cos = cos.reshape(n, half)
    sin = sin.reshape(n, half)
    cos128 = jnp.tile(cos, (1, 4))
    sin128 = jnp.tile(jnp.concatenate([-sin, sin], axis=1), (1, 2))
    return cos128, sin128


def _qkv_kernel(x_ref, g_ref, w_ref, cos_ref, sin_ref, o_ref):
    a = _rms(x_ref[...], g_ref[...]).astype(jnp.bfloat16)
    cos = jnp.concatenate([cos_ref[...]] * 2, axis=1)
    sin = jnp.concatenate([sin_ref[...]] * 2, axis=1)
    first_half = (lax.broadcasted_iota(jnp.int32, cos.shape, 1) % HEAD_DIM) < (HEAD_DIM // 2)
    scale = HEAD_DIM ** -0.5
    for c in range(N_QKV_CHUNKS):
        acc = jnp.dot(a, w_ref[:, c * LANES:(c + 1) * LANES], preferred_element_type=jnp.float32)
        if DIL_Q0 <= c < DIL_V0:
            swapped = jnp.where(first_half,
                                pltpu.roll(acc, LANES - HEAD_DIM // 2, axis=1),
                                pltpu.roll(acc, HEAD_DIM // 2, axis=1))
            acc = acc * cos + swapped * sin
        if NA_Q0 <= c < NA_K0 or DIL_Q0 <= c < DIL_K0:
            acc = acc * scale
        o_ref[:, c * LANES:(c + 1) * LANES] = acc.astype(o_ref.dtype)


def _qkv_project(x2, g, w_qkv, cos128, sin128, bm=512):
    n, d = x2.shape
    return pl.pallas_call(
        _qkv_kernel,
        out_shape=jax.ShapeDtypeStruct((n, QKV_WIDTH), jnp.bfloat16),
        grid=(n // bm,),
        in_specs=[
            pl.BlockSpec((bm, d), lambda i: (i, 0)),
            pl.BlockSpec((1, d), lambda i: (0, 0)),
            pl.BlockSpec((d, QKV_WIDTH), lambda i: (0, 0), pipeline_mode=pl.Buffered(1)),
            pl.BlockSpec((bm, 128), lambda i: (i, 0)),
            pl.BlockSpec((bm, 128), lambda i: (i, 0)),
        ],
        out_specs=pl.BlockSpec((bm, QKV_WIDTH), lambda i: (i, 0)),
        compiler_params=pltpu.CompilerParams(dimension_semantics=("arbitrary",), vmem_limit_bytes=VMEM_LIMIT),
        name="qkv_project",
    )(x2, g, w_qkv, cos128, sin128)


def _na_window_start(row0, rows):
    return jnp.clip(row0 - NA_WIN_ROWS // 2, 0, rows - NA_K_ROWS)


def _na_bias_tables(rpb, rows):
    nq, nk = NA_Q_ROWS, NA_K_ROWS
    variants = []
    for r0 in (0, nq, rows - nq):
        w0 = int(np.clip(r0 - NA_WIN_ROWS // 2, 0, rows - nk))
        qr = r0 + np.arange(nq)[:, None, None, None]
        qc = np.arange(GRID_W)[None, :, None, None]
        kr = w0 + np.arange(nk)[None, None, :, None]
        kc = np.arange(GRID_W)[None, None, None, :]
        rs = np.clip(qr - NA_WIN_ROWS // 2, 0, rows - NA_WIN_ROWS)
        cs = np.clip(qc - NA_WIN_COLS // 2, 0, GRID_W - NA_WIN_COLS)
        valid = (kr >= rs) & (kr < rs + NA_WIN_ROWS) & (kc >= cs) & (kc < cs + NA_WIN_COLS)
        row_off = np.clip(kr - qr + NA_WIN_ROWS - 1, 0, 2 * NA_WIN_ROWS - 2)
        col_off = np.clip(kc - qc + NA_WIN_COLS - 1, 0, 2 * NA_WIN_COLS - 2)
        idx = np.broadcast_to(row_off * (2 * NA_WIN_COLS - 1) + col_off, valid.shape)
        variants.append((idx.reshape(nq * GRID_W, nk * GRID_W), valid.reshape(nq * GRID_W, nk * GRID_W)))
    idx = np.stack([v[0] for v in variants]).astype(np.int32)
    valid = np.stack([v[1] for v in variants])
    h = rpb.shape[0]
    table = jnp.take(rpb.reshape(h, -1).astype(jnp.float32), jnp.asarray(idx), axis=1)
    table = jnp.where(jnp.asarray(valid)[None], table, NEG_INF)
    q, k = idx.shape[1:]
    table = table.reshape(h // HEADS_PER_CALL, HEADS_PER_CALL, 3, q, k).transpose(2, 0, 1, 3, 4)
    return table.reshape(3, h // HEADS_PER_CALL, HEADS_PER_CALL * q, k)


def _na_kernel(q_ref, k_ref, v_ref, bias_ref, o_ref, *, rows):
    r = pl.program_id(2)
    w0 = pl.multiple_of(_na_window_start(r * NA_Q_ROWS, rows) * GRID_W, GRID_W)
    nk = NA_K_ROWS * GRID_W
    kw = k_ref[pl.ds(w0, nk), :]
    vw = v_ref[pl.ds(w0, nk), :]
    m_q = NA_Q_ROWS * GRID_W
    qs = _stack_heads(q_ref[...])
    s = lax.dot_general(qs, kw, (((1,), (1,)), ((), ())), preferred_element_type=jnp.float32)
    s = s + bias_ref[...]
    mx = jnp.max(s, axis=-1, keepdims=True)
    p = jnp.exp(s - mx)
    l = jnp.sum(p, axis=-1, keepdims=True)
    pv = jnp.dot(p.astype(jnp.bfloat16), vw, preferred_element_type=jnp.float32)
    pv = pv * (1.0 / l)
    o_ref[...] = _unstack_heads(pv, m_q).astype(o_ref.dtype)


def _na_attention(qkv3, bias, seq):
    b = qkv3.shape[0]
    rows = seq // GRID_W
    m_q = NA_Q_ROWS * GRID_W
    n_groups = NA_HEADS // HEADS_PER_CALL
    n_blocks = rows // NA_Q_ROWS

    def bias_map(bi, g, r):
        return (jnp.where(r == 0, 0, jnp.where(r == n_blocks - 1, 2, 1)), g, 0, 0)

    return pl.pallas_call(
        functools.partial(_na_kernel, rows=rows),
        out_shape=jax.ShapeDtypeStruct((b, seq, NA_WIDTH), jnp.bfloat16),
        grid=(b, n_groups, n_blocks),
        in_specs=[
            pl.BlockSpec((None, m_q, LANES), lambda bi, g, r: (bi, r, NA_Q0 + g)),
            pl.BlockSpec((None, seq, LANES), lambda bi, g, r: (bi, 0, NA_K0 + g)),
            pl.BlockSpec((None, seq, LANES), lambda bi, g, r: (bi, 0, NA_V0 + g)),
            pl.BlockSpec((None, None) + bias.shape[2:], bias_map),
        ],
        out_specs=pl.BlockSpec((None, m_q, LANES), lambda bi, g, r: (bi, r, g)),
        compiler_params=pltpu.CompilerParams(
            dimension_semantics=("arbitrary",) * 3, vmem_limit_bytes=VMEM_LIMIT),
        name="na_attention",
    )(qkv3, qkv3, qkv3, bias)


def _dil_mask_tables():
    i = np.arange(DIL_Q)[:, None]
    j = np.arange(DIL_K)[None, :]
    tabs = []
    for off in (0, DIL_RADIUS, DIL_K - DIL_Q):
        tabs.append(np.where(np.abs(off + i - j) <= DIL_RADIUS, 0.0, NEG_INF))
    return jnp.asarray(np.stack(tabs), dtype=jnp.float32)


def _dil_kernel(q_ref, k_ref, v_ref, mask_ref, o_ref, lse_ref, *, sub_len):
    n = pl.program_id(2)
    n_last = sub_len // DIL_Q - 1
    w0 = pl.multiple_of(jnp.clip(n * DIL_Q - DIL_RADIUS, 0, sub_len - DIL_K), DIL_RADIUS)
    kw = k_ref[pl.ds(w0, DIL_K), :]
    vw = v_ref[pl.ds(w0, DIL_K), :]
    variant = jnp.where(n == 0, 0, jnp.where(n == n_last, 2, 1))
    mask = mask_ref[variant]
    qs = _stack_heads(q_ref[...])
    s = lax.dot_general(qs, kw, (((1,), (1,)), ((), ())), preferred_element_type=jnp.float32)
    s = s + jnp.concatenate([mask] * HEADS_PER_CALL, axis=0)
    mx = jnp.max(s, axis=-1, keepdims=True)
    p = jnp.exp(s - mx)
    l = jnp.sum(p, axis=-1, keepdims=True)
    pv = jnp.dot(p.astype(jnp.bfloat16), vw, preferred_element_type=jnp.float32)
    pv = pv * (1.0 / l)
    o_ref[...] = _unstack_heads(pv, DIL_Q).astype(o_ref.dtype)
    lse = jnp.broadcast_to(mx + jnp.log(l), (HEADS_PER_CALL * DIL_Q, LANES))
    lse_ref[...] = _unstack_heads(lse, DIL_Q)


def _dil_attention(qkv3, mask, seq, group, dil):
    b = qkv3.shape[0]
    sub_len = seq // dil
    view = qkv3.reshape(b, sub_len, dil * QKV_WIDTH)
    o, lse = pl.pallas_call(
        functools.partial(_dil_kernel, sub_len=sub_len),
        out_shape=(jax.ShapeDtypeStruct((b, sub_len, dil * LANES), jnp.bfloat16),
                   jax.ShapeDtypeStruct((b, sub_len, dil * LANES), jnp.float32)),
        grid=(b, dil, sub_len // DIL_Q),
        in_specs=[
            pl.BlockSpec((None, DIL_Q, LANES), lambda bi, rho, n: (bi, n, rho * N_QKV_CHUNKS + DIL_Q0 + group)),
            pl.BlockSpec((None, sub_len, LANES), lambda bi, rho, n: (bi, 0, rho * N_QKV_CHUNKS + DIL_K0 + group)),
            pl.BlockSpec((None, sub_len, LANES), lambda bi, rho, n: (bi, 0, rho * N_QKV_CHUNKS + DIL_V0 + group)),
            pl.BlockSpec(mask.shape, lambda bi, rho, n: (0, 0, 0)),
        ],
        out_specs=(pl.BlockSpec((None, DIL_Q, LANES), lambda bi, rho, n: (bi, n, rho)),
                   pl.BlockSpec((None, DIL_Q, LANES), lambda bi, rho, n: (bi, n, rho))),
        compiler_params=pltpu.CompilerParams(
            dimension_semantics=("arbitrary",) * 3, vmem_limit_bytes=VMEM_LIMIT),
        name=f"dil_attention_d{dil}",
    )(view, view, view, mask)
    return o.reshape(b * seq, LANES), lse.reshape(b * seq, LANES)


def _post_kernel(x_ref, yna_ref, o1_ref, o2_ref, o3_ref, l1_ref, l2_ref, l3_ref, p_ref,
                 gmix_ref, wgate_ref, wbna_ref, wbdil_ref, wout_ref,
                 gmlp_ref, wup_ref, wdown_ref, gple_ref, wpg_ref, wpp_ref, gfin_ref,
                 out_ref, *, final_norm, ff_chunk):
    f32, bf16 = jnp.float32, jnp.bfloat16
    x = x_ref[...]
    d = x.shape[1]
    a = _rms(x, gmix_ref[...]).astype(bf16)
    gate_na = _sigmoid(jnp.dot(a, wgate_ref[:, :d], preferred_element_type=f32))
    gate_dil = _sigmoid(jnp.dot(a, wgate_ref[:, d:], preferred_element_type=f32))

    l1, l2, l3 = l1_ref[...], l2_ref[...], l3_ref[...]
    mx = jnp.maximum(jnp.maximum(l1, l2), l3)
    e1, e2, e3 = jnp.exp(l1 - mx), jnp.exp(l2 - mx), jnp.exp(l3 - mx)
    inv = 1.0 / (e1 + e2 + e3)
    ydil = (e1 * o1_ref[...].astype(f32) + e2 * o2_ref[...].astype(f32) + e3 * o3_ref[...].astype(f32)) * inv

    mixed = (gate_na * jnp.dot(yna_ref[...], wbna_ref[...], preferred_element_type=f32)
             + gate_dil * jnp.dot(ydil.astype(bf16), wbdil_ref[...], preferred_element_type=f32))
    h = x + jnp.dot(mixed.astype(bf16), wout_ref[...], preferred_element_type=f32)

    c = _rms(h, gmlp_ref[...]).astype(bf16)
    d_ff = wup_ref.shape[1]
    acc = jnp.zeros_like(h)
    for f in range(d_ff // ff_chunk):
        u = jnp.dot(c, wup_ref[:, f * ff_chunk:(f + 1) * ff_chunk], preferred_element_type=f32)
        u = jnp.square(jnp.maximum(u, 0.0)).astype(bf16)
        acc = acc + jnp.dot(u, wdown_ref[f * ff_chunk:(f + 1) * ff_chunk, :], preferred_element_type=f32)
    h = h + acc

    e = _rms(h, gple_ref[...]).astype(bf16)
    pg = _sigmoid(jnp.dot(e, wpg_ref[...], preferred_element_type=f32))
    pp = jnp.dot(p_ref[...].astype(bf16), wpp_ref[...], preferred_element_type=f32)
    h = h + pg * pp
    if final_norm:
        h = _rms(h, gfin_ref[...])
    out_ref[...] = h


def _post_block(x2, yna, outs, lses, p2, weights, final_norm, bm=256, ff_chunk=1024):
    n, d = x2.shape

    def rows(width):
        return pl.BlockSpec((bm, width), lambda i: (i, 0))

    def whole(arr):
        return pl.BlockSpec(arr.shape, lambda i: (0, 0), pipeline_mode=pl.Buffered(1))

    acts = [x2, yna, *outs, *lses, p2]
    return pl.pallas_call(
        functools.partial(_post_kernel, final_norm=final_norm, ff_chunk=ff_chunk),
        out_shape=jax.ShapeDtypeStruct((n, d), jnp.float32),
        grid=(n // bm,),
        in_specs=[rows(a.shape[1]) for a in acts] + [whole(w) for w in weights],
        out_specs=rows(d),
        compiler_params=pltpu.CompilerParams(dimension_semantics=("arbitrary",), vmem_limit_bytes=VMEM_LIMIT),
        name="post_block",
    )(*acts, *weights)


def kernel(x, p, positions, g_mix, w_in, rpb, w_branch_na, w_branch_dil, w_out, g_mlp, w_up, w_down,
           g_ple, w_ple_gate, w_ple_proj, g_final):
    b, s, d = x.shape
    depth = w_in.shape[0]
    n = b * s
    bf16 = jnp.bfloat16
    cos128, sin128 = _rope_tables(positions)
    dil_mask = _dil_mask_tables()
    h = x.reshape(n, d)
    for i in range(depth):
        w_qkv = w_in[i][:, :QKV_WIDTH].astype(bf16)
        w_gate = w_in[i][:, QKV_WIDTH:].astype(bf16)
        qkv = _qkv_project(h, g_mix[i].reshape(1, d), w_qkv, cos128, sin128)
        qkv3 = qkv.reshape(b, s, QKV_WIDTH)
        yna = _na_attention(qkv3, _na_bias_tables(rpb[i], s // GRID_W), s).reshape(n, NA_WIDTH)
        outs, lses = [], []
        for g, (window, dil) in enumerate(DIL_GROUPS):
            assert window // (2 * dil) == DIL_RADIUS
            o, lse = _dil_attention(qkv3, dil_mask, s, g, dil)
            outs.append(o)
            lses.append(lse)
        weights = [
            g_mix[i].reshape(1, d), w_gate, w_branch_na[i].astype(bf16), w_branch_dil[i].astype(bf16),
            w_out[i].astype(bf16), g_mlp[i].reshape(1, d), w_up[i].astype(bf16), w_down[i].astype(bf16),
            g_ple[i].reshape(1, d), w_ple_gate[i].astype(bf16), w_ple_proj[i].astype(bf16),
            g_final.reshape(1, d),
        ]
        h = _post_block(h, yna, outs, lses, p[i].reshape(n, -1), weights, final_norm=(i == depth - 1))
    return h.reshape(b, s, d)
```

```python
import functools

import jax
import jax.numpy as jnp
import numpy as np
from jax import lax
from jax.experimental import pallas as pl
from jax.experimental.pallas import tpu as pltpu

HEAD_DIM = 64
GRID_W = 64
NA_HEADS = 8
NA_WIN_ROWS = 8
NA_WIN_COLS = 16
DIL_GROUPS = ((128, 1), (512, 4), (2048, 16))
DIL_HEADS_PER_GROUP = 4
ROPE_THETA = 10000.0
RMS_EPS = 1e-6
NEG_INF = -1e30

LANES = 256
HEADS_PER_CALL = LANES // HEAD_DIM
NA_WIDTH = NA_HEADS * HEAD_DIM
DIL_WIDTH = DIL_HEADS_PER_GROUP * len(DIL_GROUPS) * HEAD_DIM
QKV_WIDTH = 3 * NA_WIDTH + 3 * DIL_WIDTH
N_QKV_CHUNKS = QKV_WIDTH // LANES
NA_Q0, NA_K0, NA_V0 = 0, 2, 4
DIL_Q0, DIL_K0, DIL_V0 = 6, 9, 12

NA_Q_ROWS = 4
NA_K_ROWS = 12
DIL_Q = 128
DIL_K = 256
DIL_RADIUS = 64

VMEM_LIMIT = 56 * 1024 * 1024


def _rms(x, g):
    ms = jnp.mean(x * x, axis=-1, keepdims=True)
    return x * lax.rsqrt(ms + RMS_EPS) * g


def _sigmoid(x):
    return 1.0 / (1.0 + jnp.exp(-x))


def _head_of_lane(shape):
    return lax.broadcasted_iota(jnp.int32, shape, len(shape) - 1) // HEAD_DIM


def _stack_heads(q):
    head = _head_of_lane(q.shape)
    zero = jnp.zeros_like(q)
    return jnp.concatenate([jnp.where(head == h, q, zero) for h in range(HEADS_PER_CALL)], axis=0)


def _unstack_heads(x, m):
    head = _head_of_lane((m, LANES))
    out = x[0:m]
    for h in range(1, HEADS_PER_CALL):
        out = jnp.where(head == h, x[h * m:(h + 1) * m], out)
    return out


def _rope_kernel(pos_ref, freq_ref, cos_ref, sin_ref):
    ang = pos_ref[...] * freq_ref[...]
    cos_ref[...] = jnp.cos(ang)
    sin_ref[...] = jnp.sin(ang)


def _rope_tables(positions):
    half = HEAD_DIM // 2
    n = positions.size
    inv_freq = ROPE_THETA ** (-jnp.arange(half, dtype=jnp.float32) / half)
    per_row = 128 // half
    pos_rep = jnp.repeat(positions.reshape(n).astype(jnp.float32), half).reshape(n // per_row, 128)
    freq = jnp.tile(inv_freq, per_row).reshape(1, 128)
    rows = n // per_row
    blk = 1024
    cos, sin = pl.pallas_call(
        _rope_kernel,
        out_shape=(jax.ShapeDtypeStruct((rows, 128), jnp.float32),) * 2,
        grid=(rows // blk,),
        in_specs=[pl.BlockSpec((blk, 128), lambda i: (i, 0)), pl.BlockSpec((1, 128), lambda i: (0, 0))],
        out_specs=(pl.BlockSpec((blk, 128), lambda i: (i, 0)),) * 2,
        name="rope_tables",
    )(pos_rep, freq)
    cos = cos.reshape(n, half)
    sin = sin.reshape(n, half)
    cos128 = jnp.tile(cos, (1, 4))
    sin128 = jnp.tile(jnp.concatenate([-sin, sin], axis=1), (1, 2))
    return cos128, sin128


def _qkv_kernel(x_ref, g_ref, w_ref, cos_ref, sin_ref, nat_ref, *rest):
    dil_refs, stage_ref = rest[:-1], rest[-1]
    bm = x_ref.shape[0]
    a = _rms(x_ref[...], g_ref[...]).astype(jnp.bfloat16)
    cos = jnp.concatenate([cos_ref[...]] * 2, axis=1)
    sin = jnp.concatenate([sin_ref[...]] * 2, axis=1)
    first_half = (lax.broadcasted_iota(jnp.int32, cos.shape, 1) % HEAD_DIM) < (HEAD_DIM // 2)
    scale = HEAD_DIM ** -0.5
    n_groups = len(DIL_GROUPS)
    for c in range(N_QKV_CHUNKS):
        acc = jnp.dot(a, w_ref[:, c * LANES:(c + 1) * LANES], preferred_element_type=jnp.float32)
        if DIL_Q0 <= c < DIL_V0:
            swapped = jnp.where(first_half,
                                pltpu.roll(acc, LANES - HEAD_DIM // 2, axis=1),
                                pltpu.roll(acc, HEAD_DIM // 2, axis=1))
            acc = acc * cos + swapped * sin
        if NA_Q0 <= c < NA_K0 or DIL_Q0 <= c < DIL_K0:
            acc = acc * scale
        group, kind = (c - DIL_Q0) % n_groups, (c - DIL_Q0) // n_groups
        if c < DIL_Q0:
            nat_ref[:, c * LANES:(c + 1) * LANES] = acc.astype(nat_ref.dtype)
        elif DIL_GROUPS[group][1] == 1:
            nat_ref[:, (DIL_Q0 + kind) * LANES:(DIL_Q0 + kind + 1) * LANES] = acc.astype(nat_ref.dtype)
        else:
            dil = DIL_GROUPS[group][1]
            o_ref = dil_refs[group - 1]
            for half in range(LANES // 128):
                stage_ref[half] = acc[:, half * 128:(half + 1) * 128]
            for rho in range(dil):
                for half in range(LANES // 128):
                    lane0 = kind * LANES + half * 128
                    o_ref[rho, :, lane0:lane0 + 128] = (
                        stage_ref[half, pl.ds(rho, bm // dil, stride=dil), :].astype(o_ref.dtype))


def _qkv_project(x3, g, w_qkv, cos3, sin3, bm=512):
    b, s, d = x3.shape
    assert DIL_GROUPS[0][1] == 1
    nat_width = (DIL_Q0 + 3) * LANES
    out_shape = [jax.ShapeDtypeStruct((b, s, nat_width), jnp.bfloat16)]
    out_specs = [pl.BlockSpec((None, bm, nat_width), lambda bi, t: (bi, t, 0))]
    for _, dil in DIL_GROUPS[1:]:
        out_shape.append(jax.ShapeDtypeStruct((b, dil, s // dil, 3 * LANES), jnp.bfloat16))
        out_specs.append(pl.BlockSpec((None, dil, bm // dil, 3 * LANES), lambda bi, t: (bi, 0, t, 0)))
    return pl.pallas_call(
        _qkv_kernel,
        out_shape=out_shape,
        grid=(b, s // bm),
        in_specs=[
            pl.BlockSpec((None, bm, d), lambda bi, t: (bi, t, 0)),
            pl.BlockSpec((1, d), lambda bi, t: (0, 0)),
            pl.BlockSpec((d, QKV_WIDTH), lambda bi, t: (0, 0), pipeline_mode=pl.Buffered(1)),
            pl.BlockSpec((None, bm, 128), lambda bi, t: (bi, t, 0)),
            pl.BlockSpec((None, bm, 128), lambda bi, t: (bi, t, 0)),
        ],
        out_specs=out_specs,
        scratch_shapes=[pltpu.VMEM((LANES // 128, bm, 128), jnp.float32)],
        compiler_params=pltpu.CompilerParams(
            dimension_semantics=("arbitrary",) * 2, vmem_limit_bytes=VMEM_LIMIT),
        name="qkv_project",
    )(x3, g, w_qkv, cos3, sin3)


def _na_window_start(row0, rows):
    return jnp.clip(row0 - NA_WIN_ROWS // 2, 0, rows - NA_K_ROWS)


def _na_bias_kernel(rpb_ref, out_ref, *, rows):
    n_off_c = 2 * NA_WIN_COLS - 1
    qc = lax.broadcasted_iota(jnp.int32, (GRID_W, 128), 0)
    kc = lax.broadcasted_iota(jnp.int32, (GRID_W, 128), 1)
    cs = jnp.clip(qc - NA_WIN_COLS // 2, 0, GRID_W - NA_WIN_COLS)
    col_valid = (kc >= cs) & (kc < cs + NA_WIN_COLS)
    neg = jnp.full((GRID_W, GRID_W), NEG_INF, jnp.float32)
    for h in range(HEADS_PER_CALL):
        toeplitz = []
        for ro in range(2 * NA_WIN_ROWS - 1):
            row = jnp.broadcast_to(rpb_ref[h, ro:ro + 1, :], (GRID_W, 128))
            t = pltpu.roll(row, 128 - (n_off_c // 2), axis=1, stride=1, stride_axis=0)
            toeplitz.append(jnp.where(col_valid, t, NEG_INF)[:, :GRID_W])
        for variant, r0 in enumerate((0, NA_Q_ROWS, rows - NA_Q_ROWS)):
            w0 = int(np.clip(r0 - NA_WIN_ROWS // 2, 0, rows - NA_K_ROWS))
            for qr in range(NA_Q_ROWS):
                r = r0 + qr
                rs = int(np.clip(r - NA_WIN_ROWS // 2, 0, rows - NA_WIN_ROWS))
                blocks = [toeplitz[w0 + kr - r + NA_WIN_ROWS - 1] if rs <= w0 + kr < rs + NA_WIN_ROWS else neg
                          for kr in range(NA_K_ROWS)]
                row0 = (h * NA_Q_ROWS + qr) * GRID_W
                out_ref[variant, row0:row0 + GRID_W, :] = jnp.concatenate(blocks, axis=1)


def _na_bias_tables(rpb, rows):
    h, n_ro, n_co = rpb.shape
    rpb_pad = jnp.pad(rpb.astype(jnp.float32), ((0, 0), (0, 16 - n_ro), (0, 128 - n_co)))
    n_groups = h // HEADS_PER_CALL
    q, k = HEADS_PER_CALL * NA_Q_ROWS * GRID_W, NA_K_ROWS * GRID_W
    return pl.pallas_call(
        functools.partial(_na_bias_kernel, rows=rows),
        out_shape=jax.ShapeDtypeStruct((3, n_groups, q, k), jnp.float32),
        grid=(n_groups,),
        in_specs=[pl.BlockSpec((HEADS_PER_CALL, 16, 128), lambda g: (g, 0, 0))],
        out_specs=pl.BlockSpec((3, None, q, k), lambda g: (0, g, 0, 0)),
        compiler_params=pltpu.CompilerParams(dimension_semantics=("arbitrary",), vmem_limit_bytes=VMEM_LIMIT),
        name="na_bias_tables",
    )(rpb_pad)


def _na_kernel(q_ref, k_ref, v_ref, bias_ref, o_ref, *, rows):
    r = pl.program_id(2)
    w0 = pl.multiple_of(_na_window_start(r * NA_Q_ROWS, rows) * GRID_W, GRID_W)
    nk = NA_K_ROWS * GRID_W
    kw = k_ref[pl.ds(w0, nk), :]
    vw = v_ref[pl.ds(w0, nk), :]
    m_q = NA_Q_ROWS * GRID_W
    qs = _stack_heads(q_ref[...])
    s = lax.dot_general(qs, kw, (((1,), (1,)), ((), ())), preferred_element_type=jnp.float32)
    s = s + bias_ref[...]
    mx = jnp.max(s, axis=-1, keepdims=True)
    p = jnp.exp(s - mx)
    l = jnp.sum(p, axis=-1, keepdims=True)
    pv = jnp.dot(p.astype(jnp.bfloat16), vw, preferred_element_type=jnp.float32)
    pv = pv * (1.0 / l)
    o_ref[...] = _unstack_heads(pv, m_q).astype(o_ref.dtype)


def _na_attention(qkv3, bias, seq):
    b = qkv3.shape[0]
    rows = seq // GRID_W
    m_q = NA_Q_ROWS * GRID_W
    n_groups = NA_HEADS // HEADS_PER_CALL
    n_blocks = rows // NA_Q_ROWS

    def bias_map(bi, g, r):
        return (jnp.where(r == 0, 0, jnp.where(r == n_blocks - 1, 2, 1)), g, 0, 0)

    return pl.pallas_call(
        functools.partial(_na_kernel, rows=rows),
        out_shape=jax.ShapeDtypeStruct((b, seq, NA_WIDTH), jnp.bfloat16),
        grid=(b, n_groups, n_blocks),
        in_specs=[
            pl.BlockSpec((None, m_q, LANES), lambda bi, g, r: (bi, r, NA_Q0 + g)),
            pl.BlockSpec((None, seq, LANES), lambda bi, g, r: (bi, 0, NA_K0 + g)),
            pl.BlockSpec((None, seq, LANES), lambda bi, g, r: (bi, 0, NA_V0 + g)),
            pl.BlockSpec((None, None) + bias.shape[2:], bias_map),
        ],
        out_specs=pl.BlockSpec((None, m_q, LANES), lambda bi, g, r: (bi, r, g)),
        compiler_params=pltpu.CompilerParams(
            dimension_semantics=("arbitrary",) * 3, vmem_limit_bytes=VMEM_LIMIT),
        name="na_attention",
    )(qkv3, qkv3, qkv3, bias)


def _dil_mask_tables():
    i = np.arange(DIL_Q)[:, None]
    j = np.arange(DIL_K)[None, :]
    tabs = []
    for off in (0, DIL_RADIUS, DIL_K - DIL_Q):
        tabs.append(np.where(np.abs(off + i - j) <= DIL_RADIUS, 0.0, NEG_INF))
    return jnp.asarray(np.stack(tabs), dtype=jnp.float32)


def _dil_kernel(q_ref, k_ref, v_ref, mask_ref, o_ref, lse_ref, *, sub_len):
    n = pl.program_id(2)
    n_last = sub_len // DIL_Q - 1
    w0 = pl.multiple_of(jnp.clip(n * DIL_Q - DIL_RADIUS, 0, sub_len - DIL_K), DIL_RADIUS)
    kw = k_ref[pl.ds(w0, DIL_K), :]
    vw = v_ref[pl.ds(w0, DIL_K), :]
    variant = jnp.where(n == 0, 0, jnp.where(n == n_last, 2, 1))
    mask = mask_ref[variant]
    qs = _stack_heads(q_ref[...])
    s = lax.dot_general(qs, kw, (((1,), (1,)), ((), ())), preferred_element_type=jnp.float32)
    s = s + jnp.concatenate([mask] * HEADS_PER_CALL, axis=0)
    mx = jnp.max(s, axis=-1, keepdims=True)
    p = jnp.exp(s - mx)
    l = jnp.sum(p, axis=-1, keepdims=True)
    pv = jnp.dot(p.astype(jnp.bfloat16), vw, preferred_element_type=jnp.float32)
    pv = pv * (1.0 / l)
    o_ref[...] = _unstack_heads(pv, DIL_Q).astype(o_ref.dtype)
    lse = jnp.broadcast_to(mx + jnp.log(l), (HEADS_PER_CALL * DIL_Q, LANES))
    lse_ref[...] = _unstack_heads(lse, DIL_Q)


def _dil_attention(arr, mask, chunk0):
    b, dil, sub_len, _ = arr.shape
    o_spec = pl.BlockSpec((None, None, DIL_Q, LANES), lambda bi, rho, n: (bi, rho, n, 0))
    return pl.pallas_call(
        functools.partial(_dil_kernel, sub_len=sub_len),
        out_shape=(jax.ShapeDtypeStruct((b, dil, sub_len, LANES), jnp.bfloat16),
                   jax.ShapeDtypeStruct((b, dil, sub_len, LANES), jnp.float32)),
        grid=(b, dil, sub_len // DIL_Q),
        in_specs=[
            pl.BlockSpec((None, None, DIL_Q, LANES), lambda bi, rho, n: (bi, rho, n, chunk0)),
            pl.BlockSpec((None, None, sub_len, LANES), lambda bi, rho, n: (bi, rho, 0, chunk0 + 1)),
            pl.BlockSpec((None, None, sub_len, LANES), lambda bi, rho, n: (bi, rho, 0, chunk0 + 2)),
            pl.BlockSpec(mask.shape, lambda bi, rho, n: (0, 0, 0)),
        ],
        out_specs=(o_spec, o_spec),
        compiler_params=pltpu.CompilerParams(
            dimension_semantics=("arbitrary",) * 3, vmem_limit_bytes=VMEM_LIMIT),
        name=f"dil_attention_d{dil}",
    )(arr, arr, arr, mask)


def _post_kernel(x_ref, yna_ref, o1_ref, o2_ref, o3_ref, l1_ref, l2_ref, l3_ref, p_ref,
                 gmix_ref, wgate_ref, wbna_ref, wbdil_ref, wout_ref,
                 gmlp_ref, wup_ref, wdown_ref, gple_ref, wpg_ref, wpp_ref, gfin_ref,
                 out_ref, *stage_refs, final_norm, ff_chunk):
    f32, bf16 = jnp.float32, jnp.bfloat16
    x = x_ref[...]
    bm, d = x.shape
    a = _rms(x, gmix_ref[...]).astype(bf16)
    gate_na = _sigmoid(jnp.dot(a, wgate_ref[:, :d], preferred_element_type=f32))
    gate_dil = _sigmoid(jnp.dot(a, wgate_ref[:, d:], preferred_element_type=f32))

    def natural_order(ref, stage_ref):
        dil = ref.shape[0]
        if dil == 1:
            return ref[0].astype(f32)
        for rho in range(dil):
            blk = ref[rho].astype(f32)
            for half in range(LANES // 128):
                stage_ref[half, pl.ds(rho, bm // dil, stride=dil), :] = blk[:, half * 128:(half + 1) * 128]
        return jnp.concatenate([stage_ref[half] for half in range(LANES // 128)], axis=1)

    stages = iter(stage_refs)
    o1, o2, o3, l1, l2, l3 = [natural_order(r, None if r.shape[0] == 1 else next(stages))
                              for r in (o1_ref, o2_ref, o3_ref, l1_ref, l2_ref, l3_ref)]
    mx = jnp.maximum(jnp.maximum(l1, l2), l3)
    e1, e2, e3 = jnp.exp(l1 - mx), jnp.exp(l2 - mx), jnp.exp(l3 - mx)
    inv = 1.0 / (e1 + e2 + e3)
    ydil = (e1 * o1 + e2 * o2 + e3 * o3) * inv

    mixed = (gate_na * jnp.dot(yna_ref[...], wbna_ref[...], preferred_element_type=f32)
             + gate_dil * jnp.dot(ydil.astype(bf16), wbdil_ref[...], preferred_element_type=f32))
    h = x + jnp.dot(mixed.astype(bf16), wout_ref[...], preferred_element_type=f32)

    c = _rms(h, gmlp_ref[...]).astype(bf16)
    d_ff = wup_ref.shape[1]
    acc = jnp.zeros_like(h)
    for f in range(d_ff // ff_chunk):
        u = jnp.dot(c, wup_ref[:, f * ff_chunk:(f + 1) * ff_chunk], preferred_element_type=f32)
        u = jnp.square(jnp.maximum(u, 0.0)).astype(bf16)
        acc = acc + jnp.dot(u, wdown_ref[f * ff_chunk:(f + 1) * ff_chunk, :], preferred_element_type=f32)
    h = h + acc

    e = _rms(h, gple_ref[...]).astype(bf16)
    pg = _sigmoid(jnp.dot(e, wpg_ref[...], preferred_element_type=f32))
    pp = jnp.dot(p_ref[...].astype(bf16), wpp_ref[...], preferred_element_type=f32)
    h = h + pg * pp
    if final_norm:
        h = _rms(h, gfin_ref[...])
    out_ref[...] = h


def _post_block(x3, yna, outs, lses, p3, weights, final_norm, bm=256, ff_chunk=1024):
    b, s, d = x3.shape

    def rows(arr):
        return pl.BlockSpec((None, bm, arr.shape[-1]), lambda bi, t: (bi, t, 0))

    def residues(arr):
        dil = arr.shape[1]
        return pl.BlockSpec((None, dil, bm // dil, arr.shape[-1]), lambda bi, t: (bi, 0, t, 0))

    def whole(arr):
        return pl.BlockSpec(arr.shape, lambda bi, t: (0, 0), pipeline_mode=pl.Buffered(1))

    n_stages = sum(a.shape[1] > 1 for a in (*outs, *lses))
    return pl.pallas_call(
        functools.partial(_post_kernel, final_norm=final_norm, ff_chunk=ff_chunk),
        out_shape=jax.ShapeDtypeStruct((b, s, d), jnp.float32),
        grid=(b, s // bm),
        in_specs=([rows(x3), rows(yna)] + [residues(a) for a in (*outs, *lses)] + [rows(p3)]
                  + [whole(w) for w in weights]),
        out_specs=rows(x3),
        scratch_shapes=[pltpu.VMEM((LANES // 128, bm, 128), jnp.float32)] * n_stages,
        compiler_params=pltpu.CompilerParams(
            dimension_semantics=("arbitrary",) * 2, vmem_limit_bytes=VMEM_LIMIT),
        name="post_block",
    )(x3, yna, *outs, *lses, p3, *weights)


def kernel(x, p, positions, g_mix, w_in, rpb, w_branch_na, w_branch_dil, w_out, g_mlp, w_up, w_down,
           g_ple, w_ple_gate, w_ple_proj, g_final):
    b, s, d = x.shape
    depth = w_in.shape[0]
    bf16 = jnp.bfloat16
    cos128, sin128 = _rope_tables(positions)
    cos3, sin3 = cos128.reshape(b, s, 128), sin128.reshape(b, s, 128)
    dil_mask = _dil_mask_tables()
    h = x
    for i in range(depth):
        w_qkv = w_in[i][:, :QKV_WIDTH].astype(bf16)
        w_gate = w_in[i][:, QKV_WIDTH:].astype(bf16)
        nat, *dil_arrays = _qkv_project(h, g_mix[i].reshape(1, d), w_qkv, cos3, sin3)
        yna = _na_attention(nat, _na_bias_tables(rpb[i], s // GRID_W), s)
        outs, lses = [], []
        for g, (window, dil) in enumerate(DIL_GROUPS):
            assert window // (2 * dil) == DIL_RADIUS
            if dil == 1:
                o, lse = _dil_attention(nat.reshape(b, 1, s, nat.shape[-1]), dil_mask, DIL_Q0)
            else:
                o, lse = _dil_attention(dil_arrays[g - 1], dil_mask, 0)
            outs.append(o)
            lses.append(lse)
        weights = [
            g_mix[i].reshape(1, d), w_gate, w_branch_na[i].astype(bf16), w_branch_dil[i].astype(bf16),
            w_out[i].astype(bf16), g_mlp[i].reshape(1, d), w_up[i].astype(bf16), w_down[i].astype(bf16),
            g_ple[i].reshape(1, d), w_ple_gate[i].astype(bf16), w_ple_proj[i].astype(bf16),
            g_final.reshape(1, d),
        ]
        h = _post_block(h, yna, outs, lses, p[i], weights, final_norm=(i == depth - 1))
    return h
```

```python
import functools

import jax
import jax.numpy as jnp
import numpy as np
from jax import lax
from jax.experimental import pallas as pl
from jax.experimental.pallas import tpu as pltpu

HEAD_DIM = 64
GRID_W = 64
NA_HEADS = 8
NA_WIN_ROWS = 8
NA_WIN_COLS = 16
DIL_GROUPS = ((128, 1), (512, 4), (2048, 16))
DIL_HEADS_PER_GROUP = 4
ROPE_THETA = 10000.0
RMS_EPS = 1e-6
NEG_INF = -1e30
LOG2E = 1.4426950408889634
LN2 = 0.6931471805599453

LANES = 256
HEADS_PER_CALL = LANES // HEAD_DIM
NA_WIDTH = NA_HEADS * HEAD_DIM
DIL_WIDTH = DIL_HEADS_PER_GROUP * len(DIL_GROUPS) * HEAD_DIM
QKV_WIDTH = 3 * NA_WIDTH + 3 * DIL_WIDTH
N_QKV_CHUNKS = QKV_WIDTH // LANES
NA_Q0, NA_K0, NA_V0 = 0, 2, 4
DIL_Q0, DIL_K0, DIL_V0 = 6, 9, 12

NA_Q_ROWS = 4
NA_K_ROWS = 12
DIL_Q = 128
DIL_K = 256
DIL_RADIUS = 64

VMEM_LIMIT = 56 * 1024 * 1024


def _rms(x, g):
    ms = jnp.mean(x * x, axis=-1, keepdims=True)
    return x * lax.rsqrt(ms + RMS_EPS) * g


def _sigmoid(x):
    return 1.0 / (1.0 + jnp.exp(-x))


def _head_of_lane(shape):
    return lax.broadcasted_iota(jnp.int32, shape, len(shape) - 1) // HEAD_DIM


def _stack_heads(q):
    head = _head_of_lane(q.shape)
    zero = jnp.zeros_like(q)
    return jnp.concatenate([jnp.where(head == h, q, zero) for h in range(HEADS_PER_CALL)], axis=0)


def _unstack_heads(x, m):
    head = _head_of_lane((m, LANES))
    out = x[0:m]
    for h in range(1, HEADS_PER_CALL):
        out = jnp.where(head == h, x[h * m:(h + 1) * m], out)
    return out


def _rope_kernel(pos_ref, freq_ref, cos_ref, sin_ref):
    ang = pos_ref[...] * freq_ref[...]
    cos_ref[...] = jnp.cos(ang)
    sin_ref[...] = jnp.sin(ang)


def _rope_tables(positions):
    half = HEAD_DIM // 2
    n = positions.size
    inv_freq = ROPE_THETA ** (-jnp.arange(half, dtype=jnp.float32) / half)
    per_row = 128 // half
    pos_rep = jnp.repeat(positions.reshape(n).astype(jnp.float32), half).reshape(n // per_row, 128)
    freq = jnp.tile(inv_freq, per_row).reshape(1, 128)
    rows = n // per_row
    blk = 1024
    cos, sin = pl.pallas_call(
        _rope_kernel,
        out_shape=(jax.ShapeDtypeStruct((rows, 128), jnp.float32),) * 2,
        grid=(rows // blk,),
        in_specs=[pl.BlockSpec((blk, 128), lambda i: (i, 0)), pl.BlockSpec((1, 128), lambda i: (0, 0))],
        out_specs=(pl.BlockSpec((blk, 128), lambda i: (i, 0)),) * 2,
        name="rope_tables",
    )(pos_rep, freq)
    cos = cos.reshape(n, half)
    sin = sin.reshape(n, half)
    cos128 = jnp.tile(cos, (1, 4))
    sin128 = jnp.tile(jnp.concatenate([-sin, sin], axis=1), (1, 2))
    return cos128, sin128


def _qkv_kernel(x_ref, g_ref, w_ref, cos_ref, sin_ref, nat_ref, *rest):
    dil_refs, stage_ref = rest[:-1], rest[-1]
    bm = x_ref.shape[0]
    a = _rms(x_ref[...], g_ref[...]).astype(jnp.bfloat16)
    cos = jnp.concatenate([cos_ref[...]] * 2, axis=1)
    sin = jnp.concatenate([sin_ref[...]] * 2, axis=1)
    first_half = (lax.broadcasted_iota(jnp.int32, cos.shape, 1) % HEAD_DIM) < (HEAD_DIM // 2)
    scale = HEAD_DIM ** -0.5 * LOG2E
    n_groups = len(DIL_GROUPS)
    for c in range(N_QKV_CHUNKS):
        acc = jnp.dot(a, w_ref[:, c * LANES:(c + 1) * LANES], preferred_element_type=jnp.float32)
        if DIL_Q0 <= c < DIL_V0:
            swapped = jnp.where(first_half,
                                pltpu.roll(acc, LANES - HEAD_DIM // 2, axis=1),
                                pltpu.roll(acc, HEAD_DIM // 2, axis=1))
            acc = acc * cos + swapped * sin
        if NA_Q0 <= c < NA_K0 or DIL_Q0 <= c < DIL_K0:
            acc = acc * scale
        group, kind = (c - DIL_Q0) % n_groups, (c - DIL_Q0) // n_groups
        if c < DIL_Q0:
            nat_ref[:, c * LANES:(c + 1) * LANES] = acc.astype(nat_ref.dtype)
        elif DIL_GROUPS[group][1] == 1:
            nat_ref[:, (DIL_Q0 + kind) * LANES:(DIL_Q0 + kind + 1) * LANES] = acc.astype(nat_ref.dtype)
        else:
            dil = DIL_GROUPS[group][1]
            o_ref = dil_refs[group - 1]
            for half in range(LANES // 128):
                stage_ref[half] = acc[:, half * 128:(half + 1) * 128]
            for rho in range(dil):
                for half in range(LANES // 128):
                    lane0 = kind * LANES + half * 128
                    o_ref[rho, :, lane0:lane0 + 128] = (
                        stage_ref[half, pl.ds(rho, bm // dil, stride=dil), :].astype(o_ref.dtype))


def _qkv_project(x3, g, w_qkv, cos3, sin3, bm=512):
    b, s, d = x3.shape
    assert DIL_GROUPS[0][1] == 1
    nat_width = (DIL_Q0 + 3) * LANES
    out_shape = [jax.ShapeDtypeStruct((b, s, nat_width), jnp.bfloat16)]
    out_specs = [pl.BlockSpec((None, bm, nat_width), lambda bi, t: (bi, t, 0))]
    for _, dil in DIL_GROUPS[1:]:
        out_shape.append(jax.ShapeDtypeStruct((b, dil, s // dil, 3 * LANES), jnp.bfloat16))
        out_specs.append(pl.BlockSpec((None, dil, bm // dil, 3 * LANES), lambda bi, t: (bi, 0, t, 0)))
    return pl.pallas_call(
        _qkv_kernel,
        out_shape=out_shape,
        grid=(b, s // bm),
        in_specs=[
            pl.BlockSpec((None, bm, d), lambda bi, t: (bi, t, 0)),
            pl.BlockSpec((1, d), lambda bi, t: (0, 0)),
            pl.BlockSpec((d, QKV_WIDTH), lambda bi, t: (0, 0), pipeline_mode=pl.Buffered(1)),
            pl.BlockSpec((None, bm, 128), lambda bi, t: (bi, t, 0)),
            pl.BlockSpec((None, bm, 128), lambda bi, t: (bi, t, 0)),
        ],
        out_specs=out_specs,
        scratch_shapes=[pltpu.VMEM((LANES // 128, bm, 128), jnp.float32)],
        compiler_params=pltpu.CompilerParams(
            dimension_semantics=("arbitrary",) * 2, vmem_limit_bytes=VMEM_LIMIT),
        name="qkv_project",
    )(x3, g, w_qkv, cos3, sin3)


def _na_window_start(row0, rows):
    return jnp.clip(row0 - NA_WIN_ROWS // 2, 0, rows - NA_K_ROWS)


def _na_bias_kernel(rpb_ref, out_ref, *, rows):
    n_off_c = 2 * NA_WIN_COLS - 1
    qc = lax.broadcasted_iota(jnp.int32, (GRID_W, 128), 0)
    kc = lax.broadcasted_iota(jnp.int32, (GRID_W, 128), 1)
    cs = jnp.clip(qc - NA_WIN_COLS // 2, 0, GRID_W - NA_WIN_COLS)
    col_valid = (kc >= cs) & (kc < cs + NA_WIN_COLS)
    neg = jnp.full((GRID_W, GRID_W), NEG_INF, jnp.float32)
    for h in range(HEADS_PER_CALL):
        toeplitz = []
        for ro in range(2 * NA_WIN_ROWS - 1):
            row = jnp.broadcast_to(rpb_ref[h, ro:ro + 1, :], (GRID_W, 128))
            t = pltpu.roll(row, 128 - (n_off_c // 2), axis=1, stride=1, stride_axis=0)
            toeplitz.append(jnp.where(col_valid, t * LOG2E, NEG_INF)[:, :GRID_W])
        for variant, r0 in enumerate((0, NA_Q_ROWS, rows - NA_Q_ROWS)):
            w0 = int(np.clip(r0 - NA_WIN_ROWS // 2, 0, rows - NA_K_ROWS))
            for qr in range(NA_Q_ROWS):
                r = r0 + qr
                rs = int(np.clip(r - NA_WIN_ROWS // 2, 0, rows - NA_WIN_ROWS))
                blocks = [toeplitz[w0 + kr - r + NA_WIN_ROWS - 1] if rs <= w0 + kr < rs + NA_WIN_ROWS else neg
                          for kr in range(NA_K_ROWS)]
                row0 = (h * NA_Q_ROWS + qr) * GRID_W
                out_ref[variant, row0:row0 + GRID_W, :] = jnp.concatenate(blocks, axis=1)


def _na_bias_tables(rpb, rows):
    h, n_ro, n_co = rpb.shape
    rpb_pad = jnp.pad(rpb.astype(jnp.float32), ((0, 0), (0, 16 - n_ro), (0, 128 - n_co)))
    n_groups = h // HEADS_PER_CALL
    q, k = HEADS_PER_CALL * NA_Q_ROWS * GRID_W, NA_K_ROWS * GRID_W
    return pl.pallas_call(
        functools.partial(_na_bias_kernel, rows=rows),
        out_shape=jax.ShapeDtypeStruct((3, n_groups, q, k), jnp.float32),
        grid=(n_groups,),
        in_specs=[pl.BlockSpec((HEADS_PER_CALL, 16, 128), lambda g: (g, 0, 0))],
        out_specs=pl.BlockSpec((3, None, q, k), lambda g: (0, g, 0, 0)),
        compiler_params=pltpu.CompilerParams(dimension_semantics=("arbitrary",), vmem_limit_bytes=VMEM_LIMIT),
        name="na_bias_tables",
    )(rpb_pad)


def _attn_kernel(q_ref, k_ref, v_ref, bias_ref, o_ref, *rest, mq, kw, blocks_per_seq, window_start, row_chunk,
                 with_lse):
    f32 = jnp.float32
    if with_lse:
        lse_ref, *scratch = rest
    else:
        scratch = rest
    s_refs, p_refs, st_refs = scratch[0:2], scratch[2:4], scratch[4:6]
    ST_MAX, ST_INV, ST_LSE = 0, 1, 2
    n_blocks = q_ref.shape[0] * blocks_per_seq
    rows_s = HEADS_PER_CALL * mq
    rows_b = bias_ref.shape[1]
    log2_bps = blocks_per_seq.bit_length() - 1
    assert blocks_per_seq == 1 << log2_bps and n_blocks % 2 == 0 and n_blocks >= 2
    assert rows_s % rows_b == 0 and kw % 128 == 0

    def locate(j):
        j = jnp.int32(j)
        seq = lax.shift_right_logical(j, log2_bps)
        n = j & (blocks_per_seq - 1)
        q0 = pl.multiple_of(n * mq, mq)
        w0 = pl.multiple_of(window_start(n), 64)
        variant = jnp.where(n == 0, 0, jnp.where(n == blocks_per_seq - 1, 2, 1))
        return seq, q0, w0, variant

    def lanes_of(x, width):
        return jnp.concatenate([x] * (width // 128), axis=1)

    def stage_a(j, par):
        seq, q0, w0, variant = locate(j)
        qs = _stack_heads(q_ref[seq, pl.ds(q0, mq), :])
        kwin = k_ref[seq, pl.ds(w0, kw), :]
        s = lax.dot_general(qs, kwin, (((1,), (1,)), ((), ())), preferred_element_type=f32)
        s_refs[par][...] = s + jnp.concatenate([bias_ref[variant]] * (rows_s // rows_b), axis=0)
        for r0 in range(0, rows_s, row_chunk):
            mx = jnp.max(s_refs[par][r0:r0 + row_chunk, :], axis=-1, keepdims=True)
            st_refs[par][ST_MAX, r0:r0 + row_chunk, :] = jnp.broadcast_to(mx, (row_chunk, 128))

    def stage_b(par):
        for r0 in range(0, rows_s, row_chunk):
            mx = st_refs[par][ST_MAX, r0:r0 + row_chunk, :]
            p = jnp.exp2(s_refs[par][r0:r0 + row_chunk, :] - lanes_of(mx, kw))
            l = jnp.broadcast_to(jnp.sum(p, axis=-1, keepdims=True), (row_chunk, 128))
            p_refs[par][r0:r0 + row_chunk, :] = p.astype(p_refs[par].dtype)
            st_refs[par][ST_INV, r0:r0 + row_chunk, :] = 1.0 / l
            if with_lse:
                st_refs[par][ST_LSE, r0:r0 + row_chunk, :] = (mx + jnp.log2(l)) * LN2

    def stage_c(j, par):
        seq, q0, w0, _ = locate(j)
        vwin = v_ref[seq, pl.ds(w0, kw), :]
        pv = jnp.dot(p_refs[par][...], vwin, preferred_element_type=f32)
        pv = pv * lanes_of(st_refs[par][ST_INV], LANES)
        o_ref[seq, pl.ds(q0, mq), :] = _unstack_heads(pv, mq).astype(o_ref.dtype)
        if with_lse:
            lse_ref[seq, pl.ds(q0, mq), :] = _unstack_heads(lanes_of(st_refs[par][ST_LSE], LANES), mq)

    stage_a(0, 0)
    stage_a(1, 1)
    stage_b(0)

    def body(i, carry):
        for par in (0, 1):
            j = 2 * i + par
            stage_a(j, par)
            stage_b(1 - par)
            stage_c(j - 2, par)
        return carry

    lax.fori_loop(1, n_blocks // 2, body, 0)
    stage_b(1)
    stage_c(n_blocks - 2, 0)
    stage_c(n_blocks - 1, 1)


def _windowed_attention(arrays, chunks, bias, bias_spec, grid, batch_of, lane_chunk_of, out_width, *, name,
                        mq, kw, window_start, row_chunk, with_lse):
    b, r, sub_len, _ = arrays[0].shape
    blocks_per_seq = sub_len // mq

    def spec(chunk):
        return pl.BlockSpec((None, r, sub_len, LANES),
                            lambda *g: (batch_of(*g), 0, 0, chunk + lane_chunk_of(*g)))

    out_shape = [jax.ShapeDtypeStruct((b, r, sub_len, out_width), jnp.bfloat16)]
    out_specs = [spec(0)]
    if with_lse:
        out_shape.append(jax.ShapeDtypeStruct((b, r, sub_len, LANES), jnp.float32))
        out_specs.append(spec(0))
    rows_s = HEADS_PER_CALL * mq
    scratch = ([pltpu.VMEM((rows_s, kw), jnp.float32)] * 2 + [pltpu.VMEM((rows_s, kw), jnp.bfloat16)] * 2
               + [pltpu.VMEM((3 if with_lse else 2, rows_s, 128), jnp.float32)] * 2)
    return pl.pallas_call(
        functools.partial(_attn_kernel, mq=mq, kw=kw, blocks_per_seq=blocks_per_seq, window_start=window_start,
                          row_chunk=row_chunk, with_lse=with_lse),
        out_shape=out_shape,
        grid=grid,
        in_specs=[spec(c) for c in chunks] + [bias_spec],
        out_specs=out_specs,
        scratch_shapes=scratch,
        compiler_params=pltpu.CompilerParams(
            dimension_semantics=("arbitrary",) * len(grid), vmem_limit_bytes=VMEM_LIMIT),
        name=name,
    )(*arrays, bias)


def _na_attention(nat, bias):
    b, seq, c = nat.shape
    rows = seq // GRID_W
    n_groups = NA_HEADS // HEADS_PER_CALL
    arr = nat.reshape(b, 1, seq, c)
    (out,) = _windowed_attention(
        (arr, arr, arr), (NA_Q0, NA_K0, NA_V0), bias,
        pl.BlockSpec((3, None) + bias.shape[2:], lambda g, bi: (0, g, 0, 0)),
        grid=(n_groups, b), batch_of=lambda g, bi: bi, lane_chunk_of=lambda g, bi: g, out_width=NA_WIDTH,
        name="na_attention", mq=NA_Q_ROWS * GRID_W, kw=NA_K_ROWS * GRID_W,
        window_start=lambda n: _na_window_start(n * NA_Q_ROWS, rows) * GRID_W, row_chunk=32, with_lse=False)
    return out.reshape(b, seq, NA_WIDTH)


def _dil_mask_tables():
    i = np.arange(DIL_Q)[:, None]
    j = np.arange(DIL_K)[None, :]
    tabs = []
    for off in (0, DIL_RADIUS, DIL_K - DIL_Q):
        tabs.append(np.where(np.abs(off + i - j) <= DIL_RADIUS, 0.0, NEG_INF))
    return jnp.asarray(np.stack(tabs), dtype=jnp.float32)


def _dil_attention(arr, mask, chunk0):
    b, dil, sub_len, _ = arr.shape
    return _windowed_attention(
        (arr, arr, arr), (chunk0, chunk0 + 1, chunk0 + 2), mask,
        pl.BlockSpec(mask.shape, lambda bi: (0, 0, 0)),
        grid=(b,), batch_of=lambda bi: bi, lane_chunk_of=lambda bi: 0, out_width=LANES,
        name=f"dil_attention_d{dil}", mq=DIL_Q, kw=DIL_K,
        window_start=lambda n: jnp.clip(n * DIL_Q - DIL_RADIUS, 0, sub_len - DIL_K), row_chunk=DIL_Q, with_lse=True)


def _post_kernel(x_ref, yna_ref, o1_ref, o2_ref, o3_ref, l1_ref, l2_ref, l3_ref, p_ref,
                 gmix_ref, wgate_ref, wbna_ref, wbdil_ref, wout_ref,
                 gmlp_ref, wup_ref, wdown_ref, gple_ref, wpg_ref, wpp_ref, gfin_ref,
                 out_ref, *stage_refs, final_norm, ff_chunk):
    f32, bf16 = jnp.float32, jnp.bfloat16
    x = x_ref[...]
    bm, d = x.shape
    a = _rms(x, gmix_ref[...]).astype(bf16)
    gate_na = _sigmoid(jnp.dot(a, wgate_ref[:, :d], preferred_element_type=f32))
    gate_dil = _sigmoid(jnp.dot(a, wgate_ref[:, d:], preferred_element_type=f32))

    def natural_order(ref, stage_ref):
        dil = ref.shape[0]
        if dil == 1:
            return ref[0].astype(f32)
        for rho in range(dil):
            blk = ref[rho].astype(f32)
            for half in range(LANES // 128):
                stage_ref[half, pl.ds(rho, bm // dil, stride=dil), :] = blk[:, half * 128:(half + 1) * 128]
        return jnp.concatenate([stage_ref[half] for half in range(LANES // 128)], axis=1)

    stages = iter(stage_refs)
    o1, o2, o3, l1, l2, l3 = [natural_order(r, None if r.shape[0] == 1 else next(stages))
                              for r in (o1_ref, o2_ref, o3_ref, l1_ref, l2_ref, l3_ref)]
    mx = jnp.maximum(jnp.maximum(l1, l2), l3)
    e1, e2, e3 = jnp.exp(l1 - mx), jnp.exp(l2 - mx), jnp.exp(l3 - mx)
    inv = 1.0 / (e1 + e2 + e3)
    ydil = (e1 * o1 + e2 * o2 + e3 * o3) * inv

    mixed = (gate_na * jnp.dot(yna_ref[...], wbna_ref[...], preferred_element_type=f32)
             + gate_dil * jnp.dot(ydil.astype(bf16), wbdil_ref[...], preferred_element_type=f32))
    h = x + jnp.dot(mixed.astype(bf16), wout_ref[...], preferred_element_type=f32)

    c = _rms(h, gmlp_ref[...]).astype(bf16)
    d_ff = wup_ref.shape[1]
    acc = jnp.zeros_like(h)
    for f in range(d_ff // ff_chunk):
        u = jnp.dot(c, wup_ref[:, f * ff_chunk:(f + 1) * ff_chunk], preferred_element_type=f32)
        u = jnp.square(jnp.maximum(u, 0.0)).astype(bf16)
        acc = acc + jnp.dot(u, wdown_ref[f * ff_chunk:(f + 1) * ff_chunk, :], preferred_element_type=f32)
    h = h + acc

    e = _rms(h, gple_ref[...]).astype(bf16)
    pg = _sigmoid(jnp.dot(e, wpg_ref[...], preferred_element_type=f32))
    pp = jnp.dot(p_ref[...].astype(bf16), wpp_ref[...], preferred_element_type=f32)
    h = h + pg * pp
    if final_norm:
        h = _rms(h, gfin_ref[...])
    out_ref[...] = h


def _post_block(x3, yna, outs, lses, p3, weights, final_norm, bm=256, ff_chunk=1024):
    b, s, d = x3.shape

    def rows(arr):
        return pl.BlockSpec((None, bm, arr.shape[-1]), lambda bi, t: (bi, t, 0))

    def residues(arr):
        dil = arr.shape[1]
        return pl.BlockSpec((None, dil, bm // dil, arr.shape[-1]), lambda bi, t: (bi, 0, t, 0))

    def whole(arr):
        return pl.BlockSpec(arr.shape, lambda bi, t: (0, 0), pipeline_mode=pl.Buffered(1))

    n_stages = sum(a.shape[1] > 1 for a in (*outs, *lses))
    return pl.pallas_call(
        functools.partial(_post_kernel, final_norm=final_norm, ff_chunk=ff_chunk),
        out_shape=jax.ShapeDtypeStruct((b, s, d), jnp.float32),
        grid=(b, s // bm),
        in_specs=([rows(x3), rows(yna)] + [residues(a) for a in (*outs, *lses)] + [rows(p3)]
                  + [whole(w) for w in weights]),
        out_specs=rows(x3),
        scratch_shapes=[pltpu.VMEM((LANES // 128, bm, 128), jnp.float32)] * n_stages,
        compiler_params=pltpu.CompilerParams(
            dimension_semantics=("arbitrary",) * 2, vmem_limit_bytes=VMEM_LIMIT),
        name="post_block",
    )(x3, yna, *outs, *lses, p3, *weights)


def kernel(x, p, positions, g_mix, w_in, rpb, w_branch_na, w_branch_dil, w_out, g_mlp, w_up, w_down,
           g_ple, w_ple_gate, w_ple_proj, g_final):
    b, s, d = x.shape
    depth = w_in.shape[0]
    bf16 = jnp.bfloat16
    cos128, sin128 = _rope_tables(positions)
    cos3, sin3 = cos128.reshape(b, s, 128), sin128.reshape(b, s, 128)
    dil_mask = _dil_mask_tables()
    h = x
    for i in range(depth):
        w_qkv = w_in[i][:, :QKV_WIDTH].astype(bf16)
        w_gate = w_in[i][:, QKV_WIDTH:].astype(bf16)
        nat, *dil_arrays = _qkv_project(h, g_mix[i].reshape(1, d), w_qkv, cos3, sin3)
        yna = _na_attention(nat, _na_bias_tables(rpb[i], s // GRID_W))
        outs, lses = [], []
        for g, (window, dil) in enumerate(DIL_GROUPS):
            assert window // (2 * dil) == DIL_RADIUS
            if dil == 1:
                o, lse = _dil_attention(nat.reshape(b, 1, s, nat.shape[-1]), dil_mask, DIL_Q0)
            else:
                o, lse = _dil_attention(dil_arrays[g - 1], dil_mask, 0)
            outs.append(o)
            lses.append(lse)
        weights = [
            g_mix[i].reshape(1, d), w_gate, w_branch_na[i].astype(bf16), w_branch_dil[i].astype(bf16),
            w_out[i].astype(bf16), g_mlp[i].reshape(1, d), w_up[i].astype(bf16), w_down[i].astype(bf16),
            g_ple[i].reshape(1, d), w_ple_gate[i].astype(bf16), w_ple_proj[i].astype(bf16),
            g_final.reshape(1, d),
        ]
        h = _post_block(h, yna, outs, lses, p[i], weights, final_norm=(i == depth - 1))
    return h
```

```python
import functools

import jax
import jax.numpy as jnp
import numpy as np
from jax import lax
from jax.experimental import pallas as pl
from jax.experimental.pallas import tpu as pltpu

HEAD_DIM = 64
GRID_W = 64
NA_HEADS = 8
NA_WIN_ROWS = 8
NA_WIN_COLS = 16
DIL_GROUPS = ((128, 1), (512, 4), (2048, 16))
DIL_HEADS_PER_GROUP = 4
ROPE_THETA = 10000.0
RMS_EPS = 1e-6
NEG_INF = -1e30
LOG2E = 1.4426950408889634
LN2 = 0.6931471805599453

LANES = 256
HEADS_PER_CALL = LANES // HEAD_DIM
NA_WIDTH = NA_HEADS * HEAD_DIM
DIL_WIDTH = DIL_HEADS_PER_GROUP * len(DIL_GROUPS) * HEAD_DIM
QKV_WIDTH = 3 * NA_WIDTH + 3 * DIL_WIDTH
N_QKV_CHUNKS = QKV_WIDTH // LANES
NA_Q0, NA_K0, NA_V0 = 0, 2, 4
DIL_Q0, DIL_K0, DIL_V0 = 6, 9, 12

NA_Q_ROWS = 4
NA_K_ROWS = 12
DIL_Q = 128
DIL_K = 256
DIL_RADIUS = 64

VMEM_LIMIT = 56 * 1024 * 1024


def _rms(x, g):
    ms = jnp.mean(x * x, axis=-1, keepdims=True)
    return x * lax.rsqrt(ms + RMS_EPS) * g


def _sigmoid(x):
    return 1.0 / (1.0 + jnp.exp(-x))


def _head_of_lane(shape):
    return lax.broadcasted_iota(jnp.int32, shape, len(shape) - 1) // HEAD_DIM


def _stack_heads(q):
    head = _head_of_lane(q.shape)
    zero = jnp.zeros_like(q)
    return jnp.concatenate([jnp.where(head == h, q, zero) for h in range(HEADS_PER_CALL)], axis=0)


def _unstack_heads(x, m):
    head = _head_of_lane((m, LANES))
    out = x[0:m]
    for h in range(1, HEADS_PER_CALL):
        out = jnp.where(head == h, x[h * m:(h + 1) * m], out)
    return out


def _rope_kernel(pos_ref, freq_ref, cos_ref, sin_ref):
    ang = pos_ref[...] * freq_ref[...]
    cos_ref[...] = jnp.cos(ang)
    sin_ref[...] = jnp.sin(ang)


def _rope_tables(positions):
    half = HEAD_DIM // 2
    n = positions.size
    inv_freq = ROPE_THETA ** (-jnp.arange(half, dtype=jnp.float32) / half)
    blk = 4096
    return pl.pallas_call(
        _rope_kernel,
        out_shape=(jax.ShapeDtypeStruct((half, n), jnp.float32),) * 2,
        grid=(n // blk,),
        in_specs=[pl.BlockSpec((1, blk), lambda i: (0, i)), pl.BlockSpec((half, 1), lambda i: (0, 0))],
        out_specs=(pl.BlockSpec((half, blk), lambda i: (0, i)),) * 2,
        name="rope_tables",
    )(positions.reshape(1, n).astype(jnp.float32), inv_freq.reshape(half, 1))


def _qkv_kernel(x_ref, g_ref, w_ref, cos_ref, sin_ref, nat_ref, *rest):
    dil_refs, stage_ref = rest[:-1], rest[-1]
    bm = x_ref.shape[0]
    a = _rms(x_ref[...], g_ref[...]).astype(jnp.bfloat16)
    cos_t, sin_t = cos_ref[...], sin_ref[...]
    cos = jnp.concatenate([cos_t] * 4, axis=0).T
    sin = jnp.concatenate([-sin_t, sin_t] * 2, axis=0).T
    cos = jnp.concatenate([cos] * (LANES // 128), axis=1)
    sin = jnp.concatenate([sin] * (LANES // 128), axis=1)
    first_half = (lax.broadcasted_iota(jnp.int32, cos.shape, 1) % HEAD_DIM) < (HEAD_DIM // 2)
    scale = HEAD_DIM ** -0.5 * LOG2E
    n_groups = len(DIL_GROUPS)
    for c in range(N_QKV_CHUNKS):
        acc = jnp.dot(a, w_ref[:, c * LANES:(c + 1) * LANES], preferred_element_type=jnp.float32)
        if DIL_Q0 <= c < DIL_V0:
            swapped = jnp.where(first_half,
                                pltpu.roll(acc, LANES - HEAD_DIM // 2, axis=1),
                                pltpu.roll(acc, HEAD_DIM // 2, axis=1))
            acc = acc * cos + swapped * sin
        if NA_Q0 <= c < NA_K0 or DIL_Q0 <= c < DIL_K0:
            acc = acc * scale
        group, kind = (c - DIL_Q0) % n_groups, (c - DIL_Q0) // n_groups
        if c < DIL_Q0:
            nat_ref[:, c * LANES:(c + 1) * LANES] = acc.astype(nat_ref.dtype)
        elif DIL_GROUPS[group][1] == 1:
            nat_ref[:, (DIL_Q0 + kind) * LANES:(DIL_Q0 + kind + 1) * LANES] = acc.astype(nat_ref.dtype)
        else:
            dil = DIL_GROUPS[group][1]
            o_ref = dil_refs[group - 1]
            for half in range(LANES // 128):
                stage_ref[half] = acc[:, half * 128:(half + 1) * 128]
            for rho in range(dil):
                for half in range(LANES // 128):
                    lane0 = kind * LANES + half * 128
                    o_ref[rho, :, lane0:lane0 + 128] = (
                        stage_ref[half, pl.ds(rho, bm // dil, stride=dil), :].astype(o_ref.dtype))


def _qkv_project(x3, g, w_qkv, cos_t, sin_t, bm=1024):
    b, s, d = x3.shape
    assert DIL_GROUPS[0][1] == 1
    half = HEAD_DIM // 2
    tiles = s // bm
    nat_width = (DIL_Q0 + 3) * LANES
    out_shape = [jax.ShapeDtypeStruct((b, s, nat_width), jnp.bfloat16)]
    out_specs = [pl.BlockSpec((None, bm, nat_width), lambda bi, t: (bi, t, 0))]
    for _, dil in DIL_GROUPS[1:]:
        out_shape.append(jax.ShapeDtypeStruct((b, dil, s // dil, 3 * LANES), jnp.bfloat16))
        out_specs.append(pl.BlockSpec((None, dil, bm // dil, 3 * LANES), lambda bi, t: (bi, 0, t, 0)))
    return pl.pallas_call(
        _qkv_kernel,
        out_shape=out_shape,
        grid=(b, s // bm),
        in_specs=[
            pl.BlockSpec((None, bm, d), lambda bi, t: (bi, t, 0)),
            pl.BlockSpec((1, d), lambda bi, t: (0, 0)),
            pl.BlockSpec((d, QKV_WIDTH), lambda bi, t: (0, 0), pipeline_mode=pl.Buffered(1)),
            pl.BlockSpec((half, bm), lambda bi, t: (0, bi * tiles + t)),
            pl.BlockSpec((half, bm), lambda bi, t: (0, bi * tiles + t)),
        ],
        out_specs=out_specs,
        scratch_shapes=[pltpu.VMEM((LANES // 128, bm, 128), jnp.float32)],
        compiler_params=pltpu.CompilerParams(
            dimension_semantics=("arbitrary",) * 2, vmem_limit_bytes=VMEM_LIMIT),
        name="qkv_project",
    )(x3, g, w_qkv, cos_t, sin_t)


def _na_window_start(row0, rows):
    return jnp.clip(row0 - NA_WIN_ROWS // 2, 0, rows - NA_K_ROWS)


def _na_bias_kernel(rpb_ref, out_ref, *, rows):
    n_off_c = 2 * NA_WIN_COLS - 1
    qc = lax.broadcasted_iota(jnp.int32, (GRID_W, 128), 0)
    kc = lax.broadcasted_iota(jnp.int32, (GRID_W, 128), 1)
    cs = jnp.clip(qc - NA_WIN_COLS // 2, 0, GRID_W - NA_WIN_COLS)
    col_valid = (kc >= cs) & (kc < cs + NA_WIN_COLS)
    neg = jnp.full((GRID_W, GRID_W), NEG_INF, jnp.float32)
    for h in range(HEADS_PER_CALL):
        toeplitz = []
        for ro in range(2 * NA_WIN_ROWS - 1):
            row = jnp.broadcast_to(rpb_ref[h, ro:ro + 1, :], (GRID_W, 128))
            t = pltpu.roll(row, 128 - (n_off_c // 2), axis=1, stride=1, stride_axis=0)
            toeplitz.append(jnp.where(col_valid, t * LOG2E, NEG_INF)[:, :GRID_W])
        for variant, r0 in enumerate((0, NA_Q_ROWS, rows - NA_Q_ROWS)):
            w0 = int(np.clip(r0 - NA_WIN_ROWS // 2, 0, rows - NA_K_ROWS))
            for qr in range(NA_Q_ROWS):
                r = r0 + qr
                rs = int(np.clip(r - NA_WIN_ROWS // 2, 0, rows - NA_WIN_ROWS))
                blocks = [toeplitz[w0 + kr - r + NA_WIN_ROWS - 1] if rs <= w0 + kr < rs + NA_WIN_ROWS else neg
                          for kr in range(NA_K_ROWS)]
                row0 = (h * NA_Q_ROWS + qr) * GRID_W
                out_ref[variant, row0:row0 + GRID_W, :] = jnp.concatenate(blocks, axis=1)


def _na_bias_tables(rpb, rows):
    h, n_ro, n_co = rpb.shape
    rpb_pad = jnp.pad(rpb.astype(jnp.float32), ((0, 0), (0, 16 - n_ro), (0, 128 - n_co)))
    n_groups = h // HEADS_PER_CALL
    q, k = HEADS_PER_CALL * NA_Q_ROWS * GRID_W, NA_K_ROWS * GRID_W
    return pl.pallas_call(
        functools.partial(_na_bias_kernel, rows=rows),
        out_shape=jax.ShapeDtypeStruct((3, n_groups, q, k), jnp.float32),
        grid=(n_groups,),
        in_specs=[pl.BlockSpec((HEADS_PER_CALL, 16, 128), lambda g: (g, 0, 0))],
        out_specs=pl.BlockSpec((3, None, q, k), lambda g: (0, g, 0, 0)),
        compiler_params=pltpu.CompilerParams(dimension_semantics=("arbitrary",), vmem_limit_bytes=VMEM_LIMIT),
        name="na_bias_tables",
    )(rpb_pad)


def _attn_kernel(q_ref, k_ref, v_ref, bias_ref, o_ref, *rest, mq, kw, blocks_per_seq, window_start, row_chunk,
                 with_lse):
    f32 = jnp.float32
    if with_lse:
        lse_ref, *scratch = rest
    else:
        scratch = rest
    s_refs, p_refs, st_refs = scratch[0:2], scratch[2:4], scratch[4:6]
    ST_MAX, ST_INV, ST_LSE = 0, 1, 2
    n_blocks = q_ref.shape[0] * blocks_per_seq
    rows_s = HEADS_PER_CALL * mq
    rows_b = bias_ref.shape[1]
    log2_bps = blocks_per_seq.bit_length() - 1
    assert blocks_per_seq == 1 << log2_bps and n_blocks % 2 == 0 and n_blocks >= 2
    assert rows_s % rows_b == 0 and kw % 128 == 0

    def locate(j):
        j = jnp.int32(j)
        seq = lax.shift_right_logical(j, log2_bps)
        n = j & (blocks_per_seq - 1)
        q0 = pl.multiple_of(n * mq, mq)
        w0 = pl.multiple_of(window_start(n), 64)
        variant = jnp.where(n == 0, 0, jnp.where(n == blocks_per_seq - 1, 2, 1))
        return seq, q0, w0, variant

    def lanes_of(x, width):
        return jnp.concatenate([x] * (width // 128), axis=1)

    def stage_a(j, par):
        seq, q0, w0, variant = locate(j)
        qs = _stack_heads(q_ref[seq, pl.ds(q0, mq), :])
        kwin = k_ref[seq, pl.ds(w0, kw), :]
        s = lax.dot_general(qs, kwin, (((1,), (1,)), ((), ())), preferred_element_type=f32)
        s_refs[par][...] = s + jnp.concatenate([bias_ref[variant]] * (rows_s // rows_b), axis=0)
        for r0 in range(0, rows_s, row_chunk):
            mx = jnp.max(s_refs[par][r0:r0 + row_chunk, :], axis=-1, keepdims=True)
            st_refs[par][ST_MAX, r0:r0 + row_chunk, :] = jnp.broadcast_to(mx, (row_chunk, 128))

    def stage_b(par):
        for r0 in range(0, rows_s, row_chunk):
            mx = st_refs[par][ST_MAX, r0:r0 + row_chunk, :]
            p = jnp.exp2(s_refs[par][r0:r0 + row_chunk, :] - lanes_of(mx, kw))
            l = jnp.broadcast_to(jnp.sum(p, axis=-1, keepdims=True), (row_chunk, 128))
            p_refs[par][r0:r0 + row_chunk, :] = p.astype(p_refs[par].dtype)
            st_refs[par][ST_INV, r0:r0 + row_chunk, :] = 1.0 / l
            if with_lse:
                st_refs[par][ST_LSE, r0:r0 + row_chunk, :] = (mx + jnp.log2(l)) * LN2

    def stage_c(j, par):
        seq, q0, w0, _ = locate(j)
        vwin = v_ref[seq, pl.ds(w0, kw), :]
        pv = jnp.dot(p_refs[par][...], vwin, preferred_element_type=f32)
        pv = pv * lanes_of(st_refs[par][ST_INV], LANES)
        o_ref[seq, pl.ds(q0, mq), :] = _unstack_heads(pv, mq).astype(o_ref.dtype)
        if with_lse:
            lse_ref[seq, pl.ds(q0, mq), :] = _unstack_heads(lanes_of(st_refs[par][ST_LSE], LANES), mq)

    stage_a(0, 0)
    stage_a(1, 1)
    stage_b(0)

    def body(i, carry):
        for par in (0, 1):
            j = 2 * i + par
            stage_a(j, par)
            stage_b(1 - par)
            stage_c(j - 2, par)
        return carry

    lax.fori_loop(1, n_blocks // 2, body, 0)
    stage_b(1)
    stage_c(n_blocks - 2, 0)
    stage_c(n_blocks - 1, 1)


def _windowed_attention(arrays, chunks, bias, bias_spec, grid, batch_of, lane_chunk_of, out_width, *, name,
                        mq, kw, window_start, row_chunk, with_lse):
    b, r, sub_len, _ = arrays[0].shape
    blocks_per_seq = sub_len // mq

    def spec(chunk):
        return pl.BlockSpec((None, r, sub_len, LANES),
                            lambda *g: (batch_of(*g), 0, 0, chunk + lane_chunk_of(*g)))

    out_shape = [jax.ShapeDtypeStruct((b, r, sub_len, out_width), jnp.bfloat16)]
    out_specs = [spec(0)]
    if with_lse:
        out_shape.append(jax.ShapeDtypeStruct((b, r, sub_len, LANES), jnp.float32))
        out_specs.append(spec(0))
    rows_s = HEADS_PER_CALL * mq
    scratch = ([pltpu.VMEM((rows_s, kw), jnp.float32)] * 2 + [pltpu.VMEM((rows_s, kw), jnp.bfloat16)] * 2
               + [pltpu.VMEM((3 if with_lse else 2, rows_s, 128), jnp.float32)] * 2)
    return pl.pallas_call(
        functools.partial(_attn_kernel, mq=mq, kw=kw, blocks_per_seq=blocks_per_seq, window_start=window_start,
                          row_chunk=row_chunk, with_lse=with_lse),
        out_shape=out_shape,
        grid=grid,
        in_specs=[spec(c) for c in chunks] + [bias_spec],
        out_specs=out_specs,
        scratch_shapes=scratch,
        compiler_params=pltpu.CompilerParams(
            dimension_semantics=("arbitrary",) * len(grid), vmem_limit_bytes=VMEM_LIMIT),
        name=name,
    )(*arrays, bias)


def _na_attention(nat, bias):
    b, seq, c = nat.shape
    rows = seq // GRID_W
    n_groups = NA_HEADS // HEADS_PER_CALL
    arr = nat.reshape(b, 1, seq, c)
    (out,) = _windowed_attention(
        (arr, arr, arr), (NA_Q0, NA_K0, NA_V0), bias,
        pl.BlockSpec((3, None) + bias.shape[2:], lambda g, bi: (0, g, 0, 0)),
        grid=(n_groups, b), batch_of=lambda g, bi: bi, lane_chunk_of=lambda g, bi: g, out_width=NA_WIDTH,
        name="na_attention", mq=NA_Q_ROWS * GRID_W, kw=NA_K_ROWS * GRID_W,
        window_start=lambda n: _na_window_start(n * NA_Q_ROWS, rows) * GRID_W, row_chunk=32, with_lse=False)
    return out.reshape(b, seq, NA_WIDTH)


def _dil_mask_tables():
    i = np.arange(DIL_Q)[:, None]
    j = np.arange(DIL_K)[None, :]
    tabs = []
    for off in (0, DIL_RADIUS, DIL_K - DIL_Q):
        tabs.append(np.where(np.abs(off + i - j) <= DIL_RADIUS, 0.0, NEG_INF))
    return jnp.asarray(np.stack(tabs), dtype=jnp.float32)


def _dil_attention(arr, mask, chunk0):
    b, dil, sub_len, _ = arr.shape
    return _windowed_attention(
        (arr, arr, arr), (chunk0, chunk0 + 1, chunk0 + 2), mask,
        pl.BlockSpec(mask.shape, lambda bi: (0, 0, 0)),
        grid=(b,), batch_of=lambda bi: bi, lane_chunk_of=lambda bi: 0, out_width=LANES,
        name=f"dil_attention_d{dil}", mq=DIL_Q, kw=DIL_K,
        window_start=lambda n: jnp.clip(n * DIL_Q - DIL_RADIUS, 0, sub_len - DIL_K), row_chunk=DIL_Q, with_lse=True)


def _post_kernel(x_ref, yna_ref, o1_ref, o2_ref, o3_ref, l1_ref, l2_ref, l3_ref, p_ref,
                 gmix_ref, wgate_ref, wbna_ref, wbdil_ref, wout_ref,
                 gmlp_ref, wup_ref, wdown_ref, gple_ref, wpg_ref, wpp_ref, gfin_ref,
                 out_ref, *stage_refs, final_norm, ff_chunk, sub_rows):
    f32, bf16 = jnp.float32, jnp.bfloat16
    bm, d = x_ref.shape
    d_ff = wup_ref.shape[1]

    stages = iter(stage_refs)
    token_order = []
    for ref in (o1_ref, o2_ref, o3_ref, l1_ref, l2_ref, l3_ref):
        dil = ref.shape[0]
        if dil == 1:
            token_order.append(lambda r0, ref=ref: ref[0, r0:r0 + sub_rows, :].astype(f32))
            continue
        stage_ref = next(stages)
        for rho in range(dil):
            blk = ref[rho].astype(f32)
            for half in range(LANES // 128):
                stage_ref[half, pl.ds(rho, bm // dil, stride=dil), :] = blk[:, half * 128:(half + 1) * 128]
        token_order.append(lambda r0, stage_ref=stage_ref: jnp.concatenate(
            [stage_ref[half, r0:r0 + sub_rows, :] for half in range(LANES // 128)], axis=1))

    for r0 in range(0, bm, sub_rows):
        x = x_ref[r0:r0 + sub_rows, :]
        a = _rms(x, gmix_ref[...]).astype(bf16)
        gate_na = _sigmoid(jnp.dot(a, wgate_ref[:, :d], preferred_element_type=f32))
        gate_dil = _sigmoid(jnp.dot(a, wgate_ref[:, d:], preferred_element_type=f32))

        o1, o2, o3, l1, l2, l3 = [get(r0) for get in token_order]
        mx = jnp.maximum(jnp.maximum(l1, l2), l3)
        e1, e2, e3 = jnp.exp(l1 - mx), jnp.exp(l2 - mx), jnp.exp(l3 - mx)
        inv = 1.0 / (e1 + e2 + e3)
        ydil = (e1 * o1 + e2 * o2 + e3 * o3) * inv

        mixed = (gate_na * jnp.dot(yna_ref[r0:r0 + sub_rows, :], wbna_ref[...], preferred_element_type=f32)
                 + gate_dil * jnp.dot(ydil.astype(bf16), wbdil_ref[...], preferred_element_type=f32))
        h = x + jnp.dot(mixed.astype(bf16), wout_ref[...], preferred_element_type=f32)

        c = _rms(h, gmlp_ref[...]).astype(bf16)
        acc = jnp.zeros_like(h)
        for f in range(d_ff // ff_chunk):
            u = jnp.dot(c, wup_ref[:, f * ff_chunk:(f + 1) * ff_chunk], preferred_element_type=f32)
            u = jnp.square(jnp.maximum(u, 0.0)).astype(bf16)
            acc = acc + jnp.dot(u, wdown_ref[f * ff_chunk:(f + 1) * ff_chunk, :], preferred_element_type=f32)
        h = h + acc

        e = _rms(h, gple_ref[...]).astype(bf16)
        pg = _sigmoid(jnp.dot(e, wpg_ref[...], preferred_element_type=f32))
        pp = jnp.dot(p_ref[r0:r0 + sub_rows, :].astype(bf16), wpp_ref[...], preferred_element_type=f32)
        h = h + pg * pp
        if final_norm:
            h = _rms(h, gfin_ref[...])
        out_ref[r0:r0 + sub_rows, :] = h


def _post_block(x3, yna, outs, lses, p3, weights, final_norm, bm=512, sub_rows=512, ff_chunk=1024):
    b, s, d = x3.shape

    def rows(arr):
        return pl.BlockSpec((None, bm, arr.shape[-1]), lambda bi, t: (bi, t, 0))

    def residues(arr):
        dil = arr.shape[1]
        return pl.BlockSpec((None, dil, bm // dil, arr.shape[-1]), lambda bi, t: (bi, 0, t, 0))

    def whole(arr):
        return pl.BlockSpec(arr.shape, lambda bi, t: (0, 0), pipeline_mode=pl.Buffered(1))

    n_stages = sum(a.shape[1] > 1 for a in (*outs, *lses))
    return pl.pallas_call(
        functools.partial(_post_kernel, final_norm=final_norm, ff_chunk=ff_chunk, sub_rows=sub_rows),
        out_shape=jax.ShapeDtypeStruct((b, s, d), jnp.float32),
        grid=(b, s // bm),
        in_specs=([rows(x3), rows(yna)] + [residues(a) for a in (*outs, *lses)] + [rows(p3)]
                  + [whole(w) for w in weights]),
        out_specs=rows(x3),
        scratch_shapes=[pltpu.VMEM((LANES // 128, bm, 128), jnp.float32)] * n_stages,
        compiler_params=pltpu.CompilerParams(
            dimension_semantics=("arbitrary",) * 2, vmem_limit_bytes=VMEM_LIMIT),
        name="post_block",
    )(x3, yna, *outs, *lses, p3, *weights)


def kernel(x, p, positions, g_mix, w_in, rpb, w_branch_na, w_branch_dil, w_out, g_mlp, w_up, w_down,
           g_ple, w_ple_gate, w_ple_proj, g_final):
    b, s, d = x.shape
    depth = w_in.shape[0]
    bf16 = jnp.bfloat16
    cos_t, sin_t = _rope_tables(positions)
    dil_mask = _dil_mask_tables()
    h = x
    for i in range(depth):
        w_qkv = w_in[i][:, :QKV_WIDTH].astype(bf16)
        w_gate = w_in[i][:, QKV_WIDTH:].astype(bf16)
        nat, *dil_arrays = _qkv_project(h, g_mix[i].reshape(1, d), w_qkv, cos_t, sin_t)
        yna = _na_attention(nat, _na_bias_tables(rpb[i], s // GRID_W))
        outs, lses = [], []
        for g, (window, dil) in enumerate(DIL_GROUPS):
            assert window // (2 * dil) == DIL_RADIUS
            if dil == 1:
                o, lse = _dil_attention(nat.reshape(b, 1, s, nat.shape[-1]), dil_mask, DIL_Q0)
            else:
                o, lse = _dil_attention(dil_arrays[g - 1], dil_mask, 0)
            outs.append(o)
            lses.append(lse)
        weights = [
            g_mix[i].reshape(1, d), w_gate, w_branch_na[i].astype(bf16), w_branch_dil[i].astype(bf16),
            w_out[i].astype(bf16), g_mlp[i].reshape(1, d), w_up[i].astype(bf16), w_down[i].astype(bf16),
            g_ple[i].reshape(1, d), w_ple_gate[i].astype(bf16), w_ple_proj[i].astype(bf16),
            g_final.reshape(1, d),
        ]
        h = _post_block(h, yna, outs, lses, p[i], weights, final_norm=(i == depth - 1))
    return h
```

```python
import functools

import jax
import jax.numpy as jnp
import numpy as np
from jax import lax
from jax.experimental import pallas as pl
from jax.experimental.pallas import tpu as pltpu

HEAD_DIM = 64
GRID_W = 64
NA_HEADS = 8
NA_WIN_ROWS = 8
NA_WIN_COLS = 16
DIL_GROUPS = ((128, 1), (512, 4), (2048, 16))
DIL_HEADS_PER_GROUP = 4
ROPE_THETA = 10000.0
RMS_EPS = 1e-6
NEG_INF = -1e30
LOG2E = 1.4426950408889634

LANES = 256
HEADS_PER_CALL = LANES // HEAD_DIM
NA_WIDTH = NA_HEADS * HEAD_DIM
DIL_WIDTH = DIL_HEADS_PER_GROUP * len(DIL_GROUPS) * HEAD_DIM
QKV_WIDTH = 3 * NA_WIDTH + 3 * DIL_WIDTH
N_QKV_CHUNKS = QKV_WIDTH // LANES
NA_Q0, NA_K0, NA_V0 = 0, 2, 4
DIL_Q0, DIL_K0, DIL_V0 = 6, 9, 12

NA_Q_ROWS = 4
NA_K_ROWS = 12
DIL_Q = 128
DIL_K = 256
DIL_RADIUS = 64

VMEM_LIMIT = 56 * 1024 * 1024


def _rms(x, g):
    ms = jnp.mean(x * x, axis=-1, keepdims=True)
    return x * lax.rsqrt(ms + RMS_EPS) * g


def _sigmoid(x):
    return 1.0 / (1.0 + jnp.exp(-x))


def _first_head_of_tile(rows):
    assert HEAD_DIM * 2 == 128
    return lax.broadcasted_iota(jnp.int32, (rows, 128), 1) < HEAD_DIM


def _stack_heads(q):
    m = q.shape[0]
    first = _first_head_of_tile(m)
    zero = jnp.zeros((m, 128), q.dtype)
    slabs = []
    for h in range(HEADS_PER_CALL):
        t = h // 2
        tile = q[:, t * 128:(t + 1) * 128]
        kept = jnp.where(first, tile, zero) if h % 2 == 0 else jnp.where(first, zero, tile)
        slabs.append(jnp.concatenate([kept if u == t else zero for u in range(LANES // 128)], axis=1))
    return jnp.concatenate(slabs, axis=0)


def _unstack_heads(x, m):
    first = _first_head_of_tile(m)
    tiles = []
    for t in range(LANES // 128):
        lanes = slice(t * 128, (t + 1) * 128) if x.shape[1] == LANES else slice(0, 128)
        tiles.append(jnp.where(first, x[2 * t * m:(2 * t + 1) * m, lanes], x[(2 * t + 1) * m:(2 * t + 2) * m, lanes]))
    return jnp.concatenate(tiles, axis=1)


def _rope_kernel(pos_ref, freq_ref, cos_ref, sin_ref):
    ang = pos_ref[...] * freq_ref[...]
    cos_ref[...] = jnp.cos(ang)
    sin_ref[...] = jnp.sin(ang)


def _rope_tables(positions):
    half = HEAD_DIM // 2
    n = positions.size
    inv_freq = ROPE_THETA ** (-jnp.arange(half, dtype=jnp.float32) / half)
    blk = 4096
    return pl.pallas_call(
        _rope_kernel,
        out_shape=(jax.ShapeDtypeStruct((half, n), jnp.float32),) * 2,
        grid=(n // blk,),
        in_specs=[pl.BlockSpec((1, blk), lambda i: (0, i)), pl.BlockSpec((half, 1), lambda i: (0, 0))],
        out_specs=(pl.BlockSpec((half, blk), lambda i: (0, i)),) * 2,
        name="rope_tables",
    )(positions.reshape(1, n).astype(jnp.float32), inv_freq.reshape(half, 1))


def _qkv_kernel(x_ref, g_ref, w_ref, cos_ref, sin_ref, nat_ref, *rest):
    dil_refs, stage_ref, a_ref = rest[:-2], rest[-2], rest[-1]
    bm = x_ref.shape[0]
    a_ref[...] = _rms(x_ref[...], g_ref[...]).astype(a_ref.dtype)
    cos_t, sin_t = cos_ref[...], sin_ref[...]
    cos = jnp.concatenate([cos_t] * 4, axis=0).T
    sin = jnp.concatenate([-sin_t, sin_t] * 2, axis=0).T
    cos = jnp.concatenate([cos] * (LANES // 128), axis=1)
    sin = jnp.concatenate([sin] * (LANES // 128), axis=1)
    first_half = (lax.broadcasted_iota(jnp.int32, cos.shape, 1) % HEAD_DIM) < (HEAD_DIM // 2)
    scale = HEAD_DIM ** -0.5 * LOG2E
    n_groups = len(DIL_GROUPS)
    for c in reversed(range(N_QKV_CHUNKS)):
        acc = jnp.dot(a_ref[...], w_ref[:, c * LANES:(c + 1) * LANES], preferred_element_type=jnp.float32)
        if DIL_Q0 <= c < DIL_V0:
            swapped = jnp.where(first_half,
                                pltpu.roll(acc, LANES - HEAD_DIM // 2, axis=1),
                                pltpu.roll(acc, HEAD_DIM // 2, axis=1))
            acc = acc * cos + swapped * sin
        if NA_Q0 <= c < NA_K0 or DIL_Q0 <= c < DIL_K0:
            acc = acc * scale
        group, kind = (c - DIL_Q0) % n_groups, (c - DIL_Q0) // n_groups
        if c < DIL_Q0:
            nat_ref[:, c * LANES:(c + 1) * LANES] = acc.astype(nat_ref.dtype)
        elif DIL_GROUPS[group][1] == 1:
            nat_ref[:, (DIL_Q0 + kind) * LANES:(DIL_Q0 + kind + 1) * LANES] = acc.astype(nat_ref.dtype)
        else:
            dil = DIL_GROUPS[group][1]
            o_ref = dil_refs[group - 1]
            for half in range(LANES // 128):
                stage_ref[half] = acc[:, half * 128:(half + 1) * 128]
            for rho in range(dil):
                for half in range(LANES // 128):
                    lane0 = kind * LANES + half * 128
                    o_ref[rho, :, lane0:lane0 + 128] = (
                        stage_ref[half, pl.ds(rho, bm // dil, stride=dil), :].astype(o_ref.dtype))


def _qkv_project(x3, g, w_qkv, cos_t, sin_t, bm=1024):
    b, s, d = x3.shape
    assert DIL_GROUPS[0][1] == 1
    half = HEAD_DIM // 2
    tiles = s // bm
    nat_width = (DIL_Q0 + 3) * LANES
    out_shape = [jax.ShapeDtypeStruct((b, s, nat_width), jnp.bfloat16)]
    out_specs = [pl.BlockSpec((None, bm, nat_width), lambda bi, t: (bi, t, 0))]
    for _, dil in DIL_GROUPS[1:]:
        out_shape.append(jax.ShapeDtypeStruct((b, dil, s // dil, 3 * LANES), jnp.bfloat16))
        out_specs.append(pl.BlockSpec((None, dil, bm // dil, 3 * LANES), lambda bi, t: (bi, 0, t, 0)))
    return pl.pallas_call(
        _qkv_kernel,
        out_shape=out_shape,
        grid=(b, s // bm),
        in_specs=[
            pl.BlockSpec((None, bm, d), lambda bi, t: (bi, t, 0)),
            pl.BlockSpec((1, d), lambda bi, t: (0, 0)),
            pl.BlockSpec((d, QKV_WIDTH), lambda bi, t: (0, 0), pipeline_mode=pl.Buffered(1)),
            pl.BlockSpec((half, bm), lambda bi, t: (0, bi * tiles + t)),
            pl.BlockSpec((half, bm), lambda bi, t: (0, bi * tiles + t)),
        ],
        out_specs=out_specs,
        scratch_shapes=[pltpu.VMEM((LANES // 128, bm, 128), jnp.float32), pltpu.VMEM((bm, d), jnp.bfloat16)],
        compiler_params=pltpu.CompilerParams(
            dimension_semantics=("arbitrary",) * 2, vmem_limit_bytes=VMEM_LIMIT),
        name="qkv_project",
    )(x3, g, w_qkv, cos_t, sin_t)


def _na_window_start(row0, rows):
    return jnp.clip(row0 - NA_WIN_ROWS // 2, 0, rows - NA_K_ROWS)


def _na_bias_kernel(rpb_ref, out_ref, *, rows):
    n_off_c = 2 * NA_WIN_COLS - 1
    qc = lax.broadcasted_iota(jnp.int32, (GRID_W, 128), 0)
    kc = lax.broadcasted_iota(jnp.int32, (GRID_W, 128), 1)
    cs = jnp.clip(qc - NA_WIN_COLS // 2, 0, GRID_W - NA_WIN_COLS)
    col_valid = (kc >= cs) & (kc < cs + NA_WIN_COLS)
    neg = jnp.full((GRID_W, GRID_W), NEG_INF, jnp.float32)
    for h in range(HEADS_PER_CALL):
        toeplitz = []
        for ro in range(2 * NA_WIN_ROWS - 1):
            row = jnp.broadcast_to(rpb_ref[h, ro:ro + 1, :], (GRID_W, 128))
            t = pltpu.roll(row, 128 - (n_off_c // 2), axis=1, stride=1, stride_axis=0)
            toeplitz.append(jnp.where(col_valid, t * LOG2E, NEG_INF)[:, :GRID_W])
        for variant, r0 in enumerate((0, NA_Q_ROWS, rows - NA_Q_ROWS)):
            w0 = int(np.clip(r0 - NA_WIN_ROWS // 2, 0, rows - NA_K_ROWS))
            for qr in range(NA_Q_ROWS):
                r = r0 + qr
                rs = int(np.clip(r - NA_WIN_ROWS // 2, 0, rows - NA_WIN_ROWS))
                blocks = [toeplitz[w0 + kr - r + NA_WIN_ROWS - 1] if rs <= w0 + kr < rs + NA_WIN_ROWS else neg
                          for kr in range(NA_K_ROWS)]
                row0 = (h * NA_Q_ROWS + qr) * GRID_W
                out_ref[variant, row0:row0 + GRID_W, :] = jnp.concatenate(blocks, axis=1)


def _na_bias_tables(rpb, rows):
    h, n_ro, n_co = rpb.shape
    rpb_pad = jnp.pad(rpb.astype(jnp.float32), ((0, 0), (0, 16 - n_ro), (0, 128 - n_co)))
    n_groups = h // HEADS_PER_CALL
    q, k = HEADS_PER_CALL * NA_Q_ROWS * GRID_W, NA_K_ROWS * GRID_W
    return pl.pallas_call(
        functools.partial(_na_bias_kernel, rows=rows),
        out_shape=jax.ShapeDtypeStruct((3, n_groups, q, k), jnp.float32),
        grid=(n_groups,),
        in_specs=[pl.BlockSpec((HEADS_PER_CALL, 16, 128), lambda g: (g, 0, 0))],
        out_specs=pl.BlockSpec((3, None, q, k), lambda g: (0, g, 0, 0)),
        compiler_params=pltpu.CompilerParams(dimension_semantics=("arbitrary",), vmem_limit_bytes=VMEM_LIMIT),
        name="na_bias_tables",
    )(rpb_pad)


def _attn_kernel(q_ref, k_ref, v_ref, bias_ref, o_ref, *rest, mq, kw, blocks_per_seq, window_start, row_chunk,
                 with_lse):
    f32 = jnp.float32
    if with_lse:
        lse_ref, *scratch = rest
    else:
        scratch = rest
    s_refs, p_refs, st_refs = scratch[0:2], scratch[2:4], scratch[4:6]
    ST_MAX, ST_INV, ST_LSE = 0, 1, 2
    n_blocks = q_ref.shape[0] * blocks_per_seq
    rows_s = HEADS_PER_CALL * mq
    rows_b = bias_ref.shape[1]
    log2_bps = blocks_per_seq.bit_length() - 1
    assert blocks_per_seq == 1 << log2_bps and n_blocks % 2 == 0 and n_blocks >= 2
    assert rows_s % rows_b == 0 and kw % 128 == 0

    def locate(j):
        j = jnp.int32(j)
        seq = lax.shift_right_logical(j, log2_bps)
        n = j & (blocks_per_seq - 1)
        q0 = pl.multiple_of(n * mq, mq)
        w0 = pl.multiple_of(window_start(n), 64)
        variant = jnp.where(n == 0, 0, jnp.where(n == blocks_per_seq - 1, 2, 1))
        return seq, q0, w0, variant

    def lanes_of(x, width):
        return jnp.concatenate([x] * (width // 128), axis=1)

    def stage_a(j, par):
        seq, q0, w0, variant = locate(j)
        qs = _stack_heads(q_ref[seq, pl.ds(q0, mq), :])
        kwin = k_ref[seq, pl.ds(w0, kw), :]
        s = lax.dot_general(qs, kwin, (((1,), (1,)), ((), ())), preferred_element_type=f32)
        s_refs[par][...] = s + jnp.concatenate([bias_ref[variant]] * (rows_s // rows_b), axis=0)
        for r0 in range(0, rows_s, row_chunk):
            mx = jnp.max(s_refs[par][r0:r0 + row_chunk, :], axis=-1, keepdims=True)
            st_refs[par][ST_MAX, r0:r0 + row_chunk, :] = jnp.broadcast_to(mx, (row_chunk, 128))

    def stage_b(par):
        for r0 in range(0, rows_s, row_chunk):
            mx = st_refs[par][ST_MAX, r0:r0 + row_chunk, :]
            p = jnp.exp2(s_refs[par][r0:r0 + row_chunk, :] - lanes_of(mx, kw))
            l = jnp.broadcast_to(jnp.sum(p, axis=-1, keepdims=True), (row_chunk, 128))
            p_refs[par][r0:r0 + row_chunk, :] = p.astype(p_refs[par].dtype)
            st_refs[par][ST_INV, r0:r0 + row_chunk, :] = 1.0 / l
            if with_lse:
                st_refs[par][ST_LSE, r0:r0 + row_chunk, :] = mx + jnp.log2(l)

    def stage_c(j, par):
        seq, q0, w0, _ = locate(j)
        vwin = v_ref[seq, pl.ds(w0, kw), :]
        pv = jnp.dot(p_refs[par][...], vwin, preferred_element_type=f32)
        out = _unstack_heads(pv, mq) * _unstack_heads(st_refs[par][ST_INV], mq)
        o_ref[seq, pl.ds(q0, mq), :] = out.astype(o_ref.dtype)
        if with_lse:
            lse_ref[seq, pl.ds(q0, mq), :] = _unstack_heads(st_refs[par][ST_LSE], mq)

    stage_a(0, 0)
    stage_a(1, 1)
    stage_b(0)

    def body(i, carry):
        for par in (0, 1):
            j = 2 * i + par
            stage_a(j, par)
            stage_b(1 - par)
            stage_c(j - 2, par)
        return carry

    lax.fori_loop(1, n_blocks // 2, body, 0)
    stage_b(1)
    stage_c(n_blocks - 2, 0)
    stage_c(n_blocks - 1, 1)


def _windowed_attention(arrays, chunks, bias, bias_spec, grid, batch_of, lane_chunk_of, out_width, *, name,
                        mq, kw, window_start, row_chunk, with_lse):
    b, r, sub_len, _ = arrays[0].shape
    blocks_per_seq = sub_len // mq

    def spec(chunk):
        return pl.BlockSpec((None, r, sub_len, LANES),
                            lambda *g: (batch_of(*g), 0, 0, chunk + lane_chunk_of(*g)))

    out_shape = [jax.ShapeDtypeStruct((b, r, sub_len, out_width), jnp.bfloat16)]
    out_specs = [spec(0)]
    if with_lse:
        out_shape.append(jax.ShapeDtypeStruct((b, r, sub_len, LANES), jnp.float32))
        out_specs.append(spec(0))
    rows_s = HEADS_PER_CALL * mq
    scratch = ([pltpu.VMEM((rows_s, kw), jnp.float32)] * 2 + [pltpu.VMEM((rows_s, kw), jnp.bfloat16)] * 2
               + [pltpu.VMEM((3 if with_lse else 2, rows_s, 128), jnp.float32)] * 2)
    return pl.pallas_call(
        functools.partial(_attn_kernel, mq=mq, kw=kw, blocks_per_seq=blocks_per_seq, window_start=window_start,
                          row_chunk=row_chunk, with_lse=with_lse),
        out_shape=out_shape,
        grid=grid,
        in_specs=[spec(c) for c in chunks] + [bias_spec],
        out_specs=out_specs,
        scratch_shapes=scratch,
        compiler_params=pltpu.CompilerParams(
            dimension_semantics=("arbitrary",) * len(grid), vmem_limit_bytes=VMEM_LIMIT),
        name=name,
    )(*arrays, bias)


def _na_attention(nat, bias):
    b, seq, c = nat.shape
    rows = seq // GRID_W
    n_groups = NA_HEADS // HEADS_PER_CALL
    arr = nat.reshape(b, 1, seq, c)
    (out,) = _windowed_attention(
        (arr, arr, arr), (NA_Q0, NA_K0, NA_V0), bias,
        pl.BlockSpec((3, None) + bias.shape[2:], lambda g, bi: (0, g, 0, 0)),
        grid=(n_groups, b), batch_of=lambda g, bi: bi, lane_chunk_of=lambda g, bi: g, out_width=NA_WIDTH,
        name="na_attention", mq=NA_Q_ROWS * GRID_W, kw=NA_K_ROWS * GRID_W,
        window_start=lambda n: _na_window_start(n * NA_Q_ROWS, rows) * GRID_W, row_chunk=32, with_lse=False)
    return out.reshape(b, seq, NA_WIDTH)


def _dil_mask_tables():
    i = np.arange(DIL_Q)[:, None]
    j = np.arange(DIL_K)[None, :]
    tabs = []
    for off in (0, DIL_RADIUS, DIL_K - DIL_Q):
        tabs.append(np.where(np.abs(off + i - j) <= DIL_RADIUS, 0.0, NEG_INF))
    return jnp.asarray(np.stack(tabs), dtype=jnp.float32)


def _dil_attention(arr, mask, chunk0):
    b, dil, sub_len, _ = arr.shape
    return _windowed_attention(
        (arr, arr, arr), (chunk0, chunk0 + 1, chunk0 + 2), mask,
        pl.BlockSpec(mask.shape, lambda bi: (0, 0, 0)),
        grid=(b,), batch_of=lambda bi: bi, lane_chunk_of=lambda bi: 0, out_width=LANES,
        name=f"dil_attention_d{dil}", mq=DIL_Q, kw=DIL_K,
        window_start=lambda n: jnp.clip(n * DIL_Q - DIL_RADIUS, 0, sub_len - DIL_K), row_chunk=DIL_Q, with_lse=True)


def _post_kernel(x_ref, yna_ref, o1_ref, o2_ref, o3_ref, l1_ref, l2_ref, l3_ref, p_ref,
                 gmix_ref, wgate_ref, wbna_ref, wbdil_ref, wout_ref,
                 gmlp_ref, wup_ref, wdown_ref, gple_ref, wpg_ref, wpp_ref, gfin_ref,
                 out_ref, *stage_refs, final_norm, ff_chunk, sub_rows):
    f32, bf16 = jnp.float32, jnp.bfloat16
    bm, d = x_ref.shape
    d_ff = wup_ref.shape[1]

    stages = iter(stage_refs)
    token_order = []
    for ref in (o1_ref, o2_ref, o3_ref, l1_ref, l2_ref, l3_ref):
        dil = ref.shape[0]
        if dil == 1:
            token_order.append(lambda r0, ref=ref: ref[0, r0:r0 + sub_rows, :].astype(f32))
            continue
        stage_ref = next(stages)
        for rho in range(dil):
            blk = ref[rho].astype(f32)
            for half in range(LANES // 128):
                stage_ref[half, pl.ds(rho, bm // dil, stride=dil), :] = blk[:, half * 128:(half + 1) * 128]
        token_order.append(lambda r0, stage_ref=stage_ref: jnp.concatenate(
            [stage_ref[half, r0:r0 + sub_rows, :] for half in range(LANES // 128)], axis=1))

    for r0 in range(0, bm, sub_rows):
        x = x_ref[r0:r0 + sub_rows, :]
        a = _rms(x, gmix_ref[...]).astype(bf16)
        gate_na = _sigmoid(jnp.dot(a, wgate_ref[:, :d], preferred_element_type=f32))
        gate_dil = _sigmoid(jnp.dot(a, wgate_ref[:, d:], preferred_element_type=f32))

        o1, o2, o3, l1, l2, l3 = [get(r0) for get in token_order]
        mx = jnp.maximum(jnp.maximum(l1, l2), l3)
        e1, e2, e3 = jnp.exp2(l1 - mx), jnp.exp2(l2 - mx), jnp.exp2(l3 - mx)
        inv = 1.0 / (e1 + e2 + e3)
        ydil = (e1 * o1 + e2 * o2 + e3 * o3) * inv

        mixed = (gate_na * jnp.dot(yna_ref[r0:r0 + sub_rows, :], wbna_ref[...], preferred_element_type=f32)
                 + gate_dil * jnp.dot(ydil.astype(bf16), wbdil_ref[...], preferred_element_type=f32))
        h = x + jnp.dot(mixed.astype(bf16), wout_ref[...], preferred_element_type=f32)

        c = _rms(h, gmlp_ref[...]).astype(bf16)
        acc = jnp.zeros_like(h)
        for f in range(d_ff // ff_chunk):
            u = jnp.dot(c, wup_ref[:, f * ff_chunk:(f + 1) * ff_chunk], preferred_element_type=f32)
            u = jnp.square(jnp.maximum(u, 0.0)).astype(bf16)
            acc = acc + jnp.dot(u, wdown_ref[f * ff_chunk:(f + 1) * ff_chunk, :], preferred_element_type=f32)
        h = h + acc

        e = _rms(h, gple_ref[...]).astype(bf16)
        pg = _sigmoid(jnp.dot(e, wpg_ref[...], preferred_element_type=f32))
        pp = jnp.dot(p_ref[r0:r0 + sub_rows, :].astype(bf16), wpp_ref[...], preferred_element_type=f32)
        h = h + pg * pp
        if final_norm:
            h = _rms(h, gfin_ref[...])
        out_ref[r0:r0 + sub_rows, :] = h


def _post_block(x3, yna, outs, lses, p3, weights, final_norm, bm=512, sub_rows=512, ff_chunk=1024):
    b, s, d = x3.shape

    def rows(arr):
        return pl.BlockSpec((None, bm, arr.shape[-1]), lambda bi, t: (bi, t, 0))

    def residues(arr):
        dil = arr.shape[1]
        return pl.BlockSpec((None, dil, bm // dil, arr.shape[-1]), lambda bi, t: (bi, 0, t, 0))

    def whole(arr):
        return pl.BlockSpec(arr.shape, lambda bi, t: (0, 0), pipeline_mode=pl.Buffered(1))

    n_stages = sum(a.shape[1] > 1 for a in (*outs, *lses))
    return pl.pallas_call(
        functools.partial(_post_kernel, final_norm=final_norm, ff_chunk=ff_chunk, sub_rows=sub_rows),
        out_shape=jax.ShapeDtypeStruct((b, s, d), jnp.float32),
        grid=(b, s // bm),
        in_specs=([rows(x3), rows(yna)] + [residues(a) for a in (*outs, *lses)] + [rows(p3)]
                  + [whole(w) for w in weights]),
        out_specs=rows(x3),
        scratch_shapes=[pltpu.VMEM((LANES // 128, bm, 128), jnp.float32)] * n_stages,
        compiler_params=pltpu.CompilerParams(
            dimension_semantics=("arbitrary",) * 2, vmem_limit_bytes=VMEM_LIMIT),
        name="post_block",
    )(x3, yna, *outs, *lses, p3, *weights)


def kernel(x, p, positions, g_mix, w_in, rpb, w_branch_na, w_branch_dil, w_out, g_mlp, w_up, w_down,
           g_ple, w_ple_gate, w_ple_proj, g_final):
    b, s, d = x.shape
    depth = w_in.shape[0]
    bf16 = jnp.bfloat16
    cos_t, sin_t = _rope_tables(positions)
    dil_mask = _dil_mask_tables()
    h = x
    for i in range(depth):
        w_qkv = w_in[i][:, :QKV_WIDTH].astype(bf16)
        w_gate = w_in[i][:, QKV_WIDTH:].astype(bf16)
        nat, *dil_arrays = _qkv_project(h, g_mix[i].reshape(1, d), w_qkv, cos_t, sin_t)
        yna = _na_attention(nat, _na_bias_tables(rpb[i], s // GRID_W))
        outs, lses = [], []
        for g, (window, dil) in enumerate(DIL_GROUPS):
            assert window // (2 * dil) == DIL_RADIUS
            if dil == 1:
                o, lse = _dil_attention(nat.reshape(b, 1, s, nat.shape[-1]), dil_mask, DIL_Q0)
            else:
                o, lse = _dil_attention(dil_arrays[g - 1], dil_mask, 0)
            outs.append(o)
            lses.append(lse)
        weights = [
            g_mix[i].reshape(1, d), w_gate, w_branch_na[i].astype(bf16), w_branch_dil[i].astype(bf16),
            w_out[i].astype(bf16), g_mlp[i].reshape(1, d), w_up[i].astype(bf16), w_down[i].astype(bf16),
            g_ple[i].reshape(1, d), w_ple_gate[i].astype(bf16), w_ple_proj[i].astype(bf16),
            g_final.reshape(1, d),
        ]
        h = _post_block(h, yna, outs, lses, p[i], weights, final_norm=(i == depth - 1))
    return h
```

```python
import functools

import jax
import jax.numpy as jnp
import numpy as np
from jax import lax
from jax.experimental import pallas as pl
from jax.experimental.pallas import tpu as pltpu

HEAD_DIM = 64
GRID_W = 64
NA_HEADS = 8
NA_WIN_ROWS = 8
NA_WIN_COLS = 16
DIL_GROUPS = ((128, 1), (512, 4), (2048, 16))
DIL_HEADS_PER_GROUP = 4
ROPE_THETA = 10000.0
RMS_EPS = 1e-6
NEG_INF = -1e30
LOG2E = 1.4426950408889634

LANES = 256
HEADS_PER_CALL = LANES // HEAD_DIM
NA_WIDTH = NA_HEADS * HEAD_DIM
DIL_WIDTH = DIL_HEADS_PER_GROUP * len(DIL_GROUPS) * HEAD_DIM
QKV_WIDTH = 3 * NA_WIDTH + 3 * DIL_WIDTH
N_QKV_CHUNKS = QKV_WIDTH // LANES
NA_Q0, NA_K0, NA_V0 = 0, 2, 4
DIL_Q0, DIL_K0, DIL_V0 = 6, 9, 12

NA_Q_ROWS = 4
NA_K_ROWS = 12
DIL_Q = 128
DIL_K = 256
DIL_RADIUS = 64

VMEM_LIMIT = 56 * 1024 * 1024


def _rms(x, g):
    ms = jnp.mean(x * x, axis=-1, keepdims=True)
    return x * lax.rsqrt(ms + RMS_EPS) * g


def _sigmoid(x):
    return 1.0 / (1.0 + jnp.exp(-x))


def _first_head_of_tile(rows):
    assert HEAD_DIM * 2 == 128
    return lax.broadcasted_iota(jnp.int32, (rows, 128), 1) < HEAD_DIM


def _stack_heads(q):
    m = q.shape[0]
    first = _first_head_of_tile(m)
    zero = jnp.zeros((m, 128), q.dtype)
    slabs = []
    for h in range(HEADS_PER_CALL):
        t = h // 2
        tile = q[:, t * 128:(t + 1) * 128]
        kept = jnp.where(first, tile, zero) if h % 2 == 0 else jnp.where(first, zero, tile)
        slabs.append(jnp.concatenate([kept if u == t else zero for u in range(LANES // 128)], axis=1))
    return jnp.concatenate(slabs, axis=0)


def _unstack_heads(x, m):
    first = _first_head_of_tile(m)
    tiles = []
    for t in range(LANES // 128):
        lanes = slice(t * 128, (t + 1) * 128) if x.shape[1] == LANES else slice(0, 128)
        tiles.append(jnp.where(first, x[2 * t * m:(2 * t + 1) * m, lanes], x[(2 * t + 1) * m:(2 * t + 2) * m, lanes]))
    return jnp.concatenate(tiles, axis=1)


def _rope_kernel(pos_ref, freq_ref, cos_ref, sin_ref):
    ang = pos_ref[...] * freq_ref[...]
    cos_ref[...] = jnp.cos(ang)
    sin_ref[...] = jnp.sin(ang)


def _rope_tables(positions):
    half = HEAD_DIM // 2
    n = positions.size
    inv_freq = ROPE_THETA ** (-jnp.arange(half, dtype=jnp.float32) / half)
    blk = 4096
    return pl.pallas_call(
        _rope_kernel,
        out_shape=(jax.ShapeDtypeStruct((half, n), jnp.float32),) * 2,
        grid=(n // blk,),
        in_specs=[pl.BlockSpec((1, blk), lambda i: (0, i)), pl.BlockSpec((half, 1), lambda i: (0, 0))],
        out_specs=(pl.BlockSpec((half, blk), lambda i: (0, i)),) * 2,
        name="rope_tables",
    )(positions.reshape(1, n).astype(jnp.float32), inv_freq.reshape(half, 1))


def _qkv_kernel(x_ref, g_ref, w_ref, cos_ref, sin_ref, nat_ref, *rest):
    dil_refs, stage_ref = rest[:-1], rest[-1]
    bm = x_ref.shape[0]
    a = _rms(x_ref[...], g_ref[...]).astype(jnp.bfloat16)
    cos_t, sin_t = cos_ref[...], sin_ref[...]
    cos = jnp.concatenate([cos_t] * 4, axis=0).T
    sin = jnp.concatenate([-sin_t, sin_t] * 2, axis=0).T
    cos = jnp.concatenate([cos] * (LANES // 128), axis=1)
    sin = jnp.concatenate([sin] * (LANES // 128), axis=1)
    first_half = (lax.broadcasted_iota(jnp.int32, cos.shape, 1) % HEAD_DIM) < (HEAD_DIM // 2)
    scale = HEAD_DIM ** -0.5 * LOG2E
    n_groups = len(DIL_GROUPS)
    for c in range(N_QKV_CHUNKS):
        acc = jnp.dot(a, w_ref[:, c * LANES:(c + 1) * LANES], preferred_element_type=jnp.float32)
        if DIL_Q0 <= c < DIL_V0:
            swapped = jnp.where(first_half,
                                pltpu.roll(acc, LANES - HEAD_DIM // 2, axis=1),
                                pltpu.roll(acc, HEAD_DIM // 2, axis=1))
            acc = acc * cos + swapped * sin
        if NA_Q0 <= c < NA_K0 or DIL_Q0 <= c < DIL_K0:
            acc = acc * scale
        group, kind = (c - DIL_Q0) % n_groups, (c - DIL_Q0) // n_groups
        if c < DIL_Q0:
            nat_ref[:, c * LANES:(c + 1) * LANES] = acc.astype(nat_ref.dtype)
        elif DIL_GROUPS[group][1] == 1:
            nat_ref[:, (DIL_Q0 + kind) * LANES:(DIL_Q0 + kind + 1) * LANES] = acc.astype(nat_ref.dtype)
        else:
            dil = DIL_GROUPS[group][1]
            o_ref = dil_refs[group - 1]
            for half in range(LANES // 128):
                stage_ref[half] = acc[:, half * 128:(half + 1) * 128]
            for rho in range(dil):
                for half in range(LANES // 128):
                    lane0 = kind * LANES + half * 128
                    o_ref[rho, :, lane0:lane0 + 128] = (
                        stage_ref[half, pl.ds(rho, bm // dil, stride=dil), :].astype(o_ref.dtype))


def _qkv_project(x3, g, w_qkv, cos_t, sin_t, bm=1024):
    b, s, d = x3.shape
    assert DIL_GROUPS[0][1] == 1
    half = HEAD_DIM // 2
    tiles = s // bm
    nat_width = (DIL_Q0 + 3) * LANES
    out_shape = [jax.ShapeDtypeStruct((b, s, nat_width), jnp.bfloat16)]
    out_specs = [pl.BlockSpec((None, bm, nat_width), lambda bi, t: (bi, t, 0))]
    for _, dil in DIL_GROUPS[1:]:
        out_shape.append(jax.ShapeDtypeStruct((b, dil, s // dil, 3 * LANES), jnp.bfloat16))
        out_specs.append(pl.BlockSpec((None, dil, bm // dil, 3 * LANES), lambda bi, t: (bi, 0, t, 0)))
    return pl.pallas_call(
        _qkv_kernel,
        out_shape=out_shape,
        grid=(b, s // bm),
        in_specs=[
            pl.BlockSpec((None, bm, d), lambda bi, t: (bi, t, 0)),
            pl.BlockSpec((1, d), lambda bi, t: (0, 0)),
            pl.BlockSpec((d, QKV_WIDTH), lambda bi, t: (0, 0), pipeline_mode=pl.Buffered(1)),
            pl.BlockSpec((half, bm), lambda bi, t: (0, bi * tiles + t)),
            pl.BlockSpec((half, bm), lambda bi, t: (0, bi * tiles + t)),
        ],
        out_specs=out_specs,
        scratch_shapes=[pltpu.VMEM((LANES // 128, bm, 128), jnp.float32)],
        compiler_params=pltpu.CompilerParams(
            dimension_semantics=("arbitrary",) * 2, vmem_limit_bytes=VMEM_LIMIT),
        name="qkv_project",
    )(x3, g, w_qkv, cos_t, sin_t)


def _na_window_start(row0, rows):
    return jnp.clip(row0 - NA_WIN_ROWS // 2, 0, rows - NA_K_ROWS)


def _na_bias_kernel(rpb_ref, out_ref, *, rows):
    n_off_c = 2 * NA_WIN_COLS - 1
    qc = lax.broadcasted_iota(jnp.int32, (GRID_W, 128), 0)
    kc = lax.broadcasted_iota(jnp.int32, (GRID_W, 128), 1)
    cs = jnp.clip(qc - NA_WIN_COLS // 2, 0, GRID_W - NA_WIN_COLS)
    col_valid = (kc >= cs) & (kc < cs + NA_WIN_COLS)
    neg = jnp.full((GRID_W, GRID_W), NEG_INF, jnp.float32)
    for h in range(HEADS_PER_CALL):
        toeplitz = []
        for ro in range(2 * NA_WIN_ROWS - 1):
            row = jnp.broadcast_to(rpb_ref[h, ro:ro + 1, :], (GRID_W, 128))
            t = pltpu.roll(row, 128 - (n_off_c // 2), axis=1, stride=1, stride_axis=0)
            toeplitz.append(jnp.where(col_valid, t * LOG2E, NEG_INF)[:, :GRID_W])
        for variant, r0 in enumerate((0, NA_Q_ROWS, rows - NA_Q_ROWS)):
            w0 = int(np.clip(r0 - NA_WIN_ROWS // 2, 0, rows - NA_K_ROWS))
            for qr in range(NA_Q_ROWS):
                r = r0 + qr
                rs = int(np.clip(r - NA_WIN_ROWS // 2, 0, rows - NA_WIN_ROWS))
                blocks = [toeplitz[w0 + kr - r + NA_WIN_ROWS - 1] if rs <= w0 + kr < rs + NA_WIN_ROWS else neg
                          for kr in range(NA_K_ROWS)]
                row0 = (h * NA_Q_ROWS + qr) * GRID_W
                out_ref[variant, row0:row0 + GRID_W, :] = jnp.concatenate(blocks, axis=1)


def _na_bias_tables(rpb, rows):
    h, n_ro, n_co = rpb.shape
    rpb_pad = jnp.pad(rpb.astype(jnp.float32), ((0, 0), (0, 16 - n_ro), (0, 128 - n_co)))
    n_groups = h // HEADS_PER_CALL
    q, k = HEADS_PER_CALL * NA_Q_ROWS * GRID_W, NA_K_ROWS * GRID_W
    return pl.pallas_call(
        functools.partial(_na_bias_kernel, rows=rows),
        out_shape=jax.ShapeDtypeStruct((3, n_groups, q, k), jnp.float32),
        grid=(n_groups,),
        in_specs=[pl.BlockSpec((HEADS_PER_CALL, 16, 128), lambda g: (g, 0, 0))],
        out_specs=pl.BlockSpec((3, None, q, k), lambda g: (0, g, 0, 0)),
        compiler_params=pltpu.CompilerParams(dimension_semantics=("arbitrary",), vmem_limit_bytes=VMEM_LIMIT),
        name="na_bias_tables",
    )(rpb_pad)


def _attn_kernel(q_ref, k_ref, v_ref, bias_ref, o_ref, *rest, mq, kw, blocks_per_seq, window_start, row_chunk,
                 with_lse, interior_dead_tiles):
    f32 = jnp.float32
    if with_lse:
        lse_ref, *scratch = rest
    else:
        scratch = rest
    s_refs, p_refs, st_refs = scratch[0:2], scratch[2:4], scratch[4:6]
    ST_MAX, ST_INV, ST_LSE = 0, 1, 2
    n_blocks = q_ref.shape[0] * blocks_per_seq
    rows_s = HEADS_PER_CALL * mq
    rows_b = bias_ref.shape[1]
    log2_bps = blocks_per_seq.bit_length() - 1
    assert blocks_per_seq == 1 << log2_bps and n_blocks % 2 == 0 and n_blocks >= 2
    assert rows_s % rows_b == 0 and kw % 128 == 0

    def locate(j):
        j = jnp.int32(j)
        seq = lax.shift_right_logical(j, log2_bps)
        n = j & (blocks_per_seq - 1)
        q0 = pl.multiple_of(n * mq, mq)
        w0 = pl.multiple_of(window_start(n), 64)
        variant = jnp.where(n == 0, 0, jnp.where(n == blocks_per_seq - 1, 2, 1))
        return seq, q0, w0, variant

    def lanes_of(x, width):
        return jnp.concatenate([x] * (width // 128), axis=1)

    def stage_a(j, par):
        seq, q0, w0, variant = locate(j)
        qs = _stack_heads(q_ref[seq, pl.ds(q0, mq), :])
        kwin = k_ref[seq, pl.ds(w0, kw), :]
        s = lax.dot_general(qs, kwin, (((1,), (1,)), ((), ())), preferred_element_type=f32)
        s_refs[par][...] = s + jnp.concatenate([bias_ref[variant]] * (rows_s // rows_b), axis=0)
        for r0 in range(0, rows_s, row_chunk):
            mx = jnp.max(s_refs[par][r0:r0 + row_chunk, :], axis=-1, keepdims=True)
            st_refs[par][ST_MAX, r0:r0 + row_chunk, :] = jnp.broadcast_to(mx, (row_chunk, 128))

    def stage_b(par, p_ref, dead_tiles):
        for r0 in range(0, rows_s, row_chunk):
            mx = st_refs[par][ST_MAX, r0:r0 + row_chunk, :]
            live = [t for t in range(kw // 128) if t not in dead_tiles(r0)]
            s = jnp.concatenate([s_refs[par][r0:r0 + row_chunk, t * 128:(t + 1) * 128] for t in live], axis=1)
            p = jnp.exp2(s - lanes_of(mx, s.shape[1]))
            l = jnp.broadcast_to(jnp.sum(p, axis=-1, keepdims=True), (row_chunk, 128))
            p = p.astype(p_ref.dtype)
            for i, t in enumerate(live):
                p_ref[r0:r0 + row_chunk, t * 128:(t + 1) * 128] = p[:, i * 128:(i + 1) * 128]
            st_refs[par][ST_INV, r0:r0 + row_chunk, :] = 1.0 / l
            if with_lse:
                st_refs[par][ST_LSE, r0:r0 + row_chunk, :] = mx + jnp.log2(l)

    def stage_c(j, par, p_ref):
        seq, q0, w0, _ = locate(j)
        vwin = v_ref[seq, pl.ds(w0, kw), :]
        pv = jnp.dot(p_ref[...], vwin, preferred_element_type=f32)
        out = _unstack_heads(pv, mq) * _unstack_heads(st_refs[par][ST_INV], mq)
        o_ref[seq, pl.ds(q0, mq), :] = out.astype(o_ref.dtype)
        if with_lse:
            lse_ref[seq, pl.ds(q0, mq), :] = _unstack_heads(st_refs[par][ST_LSE], mq)

    def no_dead_tiles(r0):
        return ()

    def steady_state(first_pair, dead_tiles):
        def body(i, carry):
            for par in (0, 1):
                j = 2 * i + par
                stage_a(j, par)
                stage_b(1 - par, p_refs[1 - par], dead_tiles)
                stage_c(j - 2, par, p_refs[par])
            return carry

        lax.fori_loop(first_pair, n_blocks // 2, body, 0)

    if interior_dead_tiles is None:
        stage_a(0, 0)
        stage_a(1, 1)
        stage_b(0, p_refs[0], no_dead_tiles)
        steady_state(1, no_dead_tiles)
        stage_b(1, p_refs[1], no_dead_tiles)
        stage_c(n_blocks - 2, 0, p_refs[0])
        stage_c(n_blocks - 1, 1, p_refs[1])
    else:
        p_edge = scratch[6]
        assert q_ref.shape[0] == 1 and n_blocks >= 6
        for p_ref in p_refs:
            for r0 in range(0, rows_s, row_chunk):
                for t in interior_dead_tiles(r0):
                    p_ref[r0:r0 + row_chunk, t * 128:(t + 1) * 128] = jnp.zeros((row_chunk, 128), p_ref.dtype)
        stage_a(0, 0)
        stage_a(1, 1)
        stage_b(0, p_edge, no_dead_tiles)
        stage_a(2, 0)
        stage_b(1, p_refs[1], interior_dead_tiles)
        stage_c(0, 0, p_edge)
        stage_a(3, 1)
        stage_b(0, p_refs[0], interior_dead_tiles)
        stage_c(1, 1, p_refs[1])
        steady_state(2, interior_dead_tiles)
        stage_b(1, p_edge, no_dead_tiles)
        stage_c(n_blocks - 2, 0, p_refs[0])
        stage_c(n_blocks - 1, 1, p_edge)


def _windowed_attention(arrays, chunks, bias, bias_spec, grid, batch_of, lane_chunk_of, out_width, *, name,
                        mq, kw, window_start, row_chunk, with_lse, interior_dead_tiles=None):
    b, r, sub_len, _ = arrays[0].shape
    blocks_per_seq = sub_len // mq

    def spec(chunk):
        return pl.BlockSpec((None, r, sub_len, LANES),
                            lambda *g: (batch_of(*g), 0, 0, chunk + lane_chunk_of(*g)))

    out_shape = [jax.ShapeDtypeStruct((b, r, sub_len, out_width), jnp.bfloat16)]
    out_specs = [spec(0)]
    if with_lse:
        out_shape.append(jax.ShapeDtypeStruct((b, r, sub_len, LANES), jnp.float32))
        out_specs.append(spec(0))
    rows_s = HEADS_PER_CALL * mq
    scratch = ([pltpu.VMEM((rows_s, kw), jnp.float32)] * 2 + [pltpu.VMEM((rows_s, kw), jnp.bfloat16)] * 2
               + [pltpu.VMEM((3 if with_lse else 2, rows_s, 128), jnp.float32)] * 2)
    if interior_dead_tiles is not None:
        scratch.append(pltpu.VMEM((rows_s, kw), jnp.bfloat16))
    return pl.pallas_call(
        functools.partial(_attn_kernel, mq=mq, kw=kw, blocks_per_seq=blocks_per_seq, window_start=window_start,
                          row_chunk=row_chunk, with_lse=with_lse, interior_dead_tiles=interior_dead_tiles),
        out_shape=out_shape,
        grid=grid,
        in_specs=[spec(c) for c in chunks] + [bias_spec],
        out_specs=out_specs,
        scratch_shapes=scratch,
        compiler_params=pltpu.CompilerParams(
            dimension_semantics=("arbitrary",) * len(grid), vmem_limit_bytes=VMEM_LIMIT),
        name=name,
    )(*arrays, bias)


def _na_attention(nat, bias):
    b, seq, c = nat.shape
    rows = seq // GRID_W
    n_groups = NA_HEADS // HEADS_PER_CALL
    arr = nat.reshape(b, 1, seq, c)
    m_q = NA_Q_ROWS * GRID_W
    for blk in range(1, rows // NA_Q_ROWS - 1):
        r0 = blk * NA_Q_ROWS
        assert 0 <= r0 - NA_WIN_ROWS // 2 <= rows - NA_K_ROWS
        assert r0 + NA_Q_ROWS - 1 - NA_WIN_ROWS // 2 <= rows - NA_WIN_ROWS

    def interior_dead_tiles(row):
        qr = (row % m_q) // GRID_W
        per_tile = 128 // GRID_W
        return tuple(t for t in range(NA_K_ROWS // per_tile)
                     if not any(0 <= kr - qr < NA_WIN_ROWS for kr in range(t * per_tile, (t + 1) * per_tile)))

    (out,) = _windowed_attention(
        (arr, arr, arr), (NA_Q0, NA_K0, NA_V0), bias,
        pl.BlockSpec((3, None) + bias.shape[2:], lambda g, bi: (0, g, 0, 0)),
        grid=(n_groups, b), batch_of=lambda g, bi: bi, lane_chunk_of=lambda g, bi: g, out_width=NA_WIDTH,
        name="na_attention", mq=m_q, kw=NA_K_ROWS * GRID_W,
        window_start=lambda n: _na_window_start(n * NA_Q_ROWS, rows) * GRID_W, row_chunk=32, with_lse=False,
        interior_dead_tiles=interior_dead_tiles)
    return out.reshape(b, seq, NA_WIDTH)


def _dil_mask_tables():
    i = np.arange(DIL_Q)[:, None]
    j = np.arange(DIL_K)[None, :]
    tabs = []
    for off in (0, DIL_RADIUS, DIL_K - DIL_Q):
        tabs.append(np.where(np.abs(off + i - j) <= DIL_RADIUS, 0.0, NEG_INF))
    return jnp.asarray(np.stack(tabs), dtype=jnp.float32)


def _dil_attention(arr, mask, chunk0):
    b, dil, sub_len, _ = arr.shape
    return _windowed_attention(
        (arr, arr, arr), (chunk0, chunk0 + 1, chunk0 + 2), mask,
        pl.BlockSpec(mask.shape, lambda bi: (0, 0, 0)),
        grid=(b,), batch_of=lambda bi: bi, lane_chunk_of=lambda bi: 0, out_width=LANES,
        name=f"dil_attention_d{dil}", mq=DIL_Q, kw=DIL_K,
        window_start=lambda n: jnp.clip(n * DIL_Q - DIL_RADIUS, 0, sub_len - DIL_K), row_chunk=DIL_Q, with_lse=True)


def _post_kernel(x_ref, yna_ref, o1_ref, o2_ref, o3_ref, l1_ref, l2_ref, l3_ref, p_ref,
                 gmix_ref, wgate_ref, wbna_ref, wbdil_ref, wout_ref,
                 gmlp_ref, wup_ref, wdown_ref, gple_ref, wpg_ref, wpp_ref, gfin_ref,
                 out_ref, *stage_refs, final_norm, ff_chunk, sub_rows):
    f32, bf16 = jnp.float32, jnp.bfloat16
    bm, d = x_ref.shape
    d_ff = wup_ref.shape[1]

    stages = iter(stage_refs)
    token_order = []
    for ref in (o1_ref, o2_ref, o3_ref, l1_ref, l2_ref, l3_ref):
        dil = ref.shape[0]
        if dil == 1:
            token_order.append(lambda r0, ref=ref: ref[0, r0:r0 + sub_rows, :].astype(f32))
            continue
        stage_ref = next(stages)
        for rho in range(dil):
            blk = ref[rho].astype(f32)
            for half in range(LANES // 128):
                stage_ref[half, pl.ds(rho, bm // dil, stride=dil), :] = blk[:, half * 128:(half + 1) * 128]
        token_order.append(lambda r0, stage_ref=stage_ref: jnp.concatenate(
            [stage_ref[half, r0:r0 + sub_rows, :] for half in range(LANES // 128)], axis=1))

    def dot(lhs, w):
        return jnp.dot(lhs, w, preferred_element_type=f32)

    chains = [dict(r0=r0) for r0 in range(0, bm, sub_rows)]
    for c in chains:
        r0 = c["r0"]
        c["x"] = x_ref[r0:r0 + sub_rows, :]
        a = _rms(c["x"], gmix_ref[...]).astype(bf16)
        c["gate_na"] = _sigmoid(dot(a, wgate_ref[:, :d]))
        c["gate_dil"] = _sigmoid(dot(a, wgate_ref[:, d:]))
    for c in chains:
        r0 = c["r0"]
        o1, o2, o3, l1, l2, l3 = [get(r0) for get in token_order]
        mx = jnp.maximum(jnp.maximum(l1, l2), l3)
        e1, e2, e3 = jnp.exp2(l1 - mx), jnp.exp2(l2 - mx), jnp.exp2(l3 - mx)
        ydil = (e1 * o1 + e2 * o2 + e3 * o3) * (1.0 / (e1 + e2 + e3))
        c["mixed"] = (c.pop("gate_na") * dot(yna_ref[r0:r0 + sub_rows, :], wbna_ref[...])
                      + c.pop("gate_dil") * dot(ydil.astype(bf16), wbdil_ref[...]))
    for c in chains:
        c["h"] = c.pop("x") + dot(c.pop("mixed").astype(bf16), wout_ref[...])
        c["c"] = _rms(c["h"], gmlp_ref[...]).astype(bf16)
        c["acc"] = jnp.zeros_like(c["h"])
    for f in range(d_ff // ff_chunk):
        for c in chains:
            u = dot(c["c"], wup_ref[:, f * ff_chunk:(f + 1) * ff_chunk])
            c["u"] = jnp.square(jnp.maximum(u, 0.0)).astype(bf16)
        for c in chains:
            c["acc"] = c["acc"] + dot(c.pop("u"), wdown_ref[f * ff_chunk:(f + 1) * ff_chunk, :])
    for c in chains:
        r0 = c["r0"]
        c["h"] = c["h"] + c.pop("acc")
        e = _rms(c["h"], gple_ref[...]).astype(bf16)
        c["pg"] = _sigmoid(dot(e, wpg_ref[...]))
        c["pp"] = dot(p_ref[r0:r0 + sub_rows, :].astype(bf16), wpp_ref[...])
    for c in chains:
        r0 = c["r0"]
        h = c["h"] + c["pg"] * c["pp"]
        if final_norm:
            h = _rms(h, gfin_ref[...])
        out_ref[r0:r0 + sub_rows, :] = h


def _post_block(x3, yna, outs, lses, p3, weights, final_norm, bm=512, sub_rows=256, ff_chunk=1024):
    b, s, d = x3.shape

    def rows(arr):
        return pl.BlockSpec((None, bm, arr.shape[-1]), lambda bi, t: (bi, t, 0))

    def residues(arr):
        dil = arr.shape[1]
        return pl.BlockSpec((None, dil, bm // dil, arr.shape[-1]), lambda bi, t: (bi, 0, t, 0))

    def whole(arr):
        return pl.BlockSpec(arr.shape, lambda bi, t: (0, 0), pipeline_mode=pl.Buffered(1))

    n_stages = sum(a.shape[1] > 1 for a in (*outs, *lses))
    return pl.pallas_call(
        functools.partial(_post_kernel, final_norm=final_norm, ff_chunk=ff_chunk, sub_rows=sub_rows),
        out_shape=jax.ShapeDtypeStruct((b, s, d), jnp.float32),
        grid=(b, s // bm),
        in_specs=([rows(x3), rows(yna)] + [residues(a) for a in (*outs, *lses)] + [rows(p3)]
                  + [whole(w) for w in weights]),
        out_specs=rows(x3),
        scratch_shapes=[pltpu.VMEM((LANES // 128, bm, 128), jnp.float32)] * n_stages,
        compiler_params=pltpu.CompilerParams(
            dimension_semantics=("arbitrary",) * 2, vmem_limit_bytes=VMEM_LIMIT),
        name="post_block",
    )(x3, yna, *outs, *lses, p3, *weights)


def kernel(x, p, positions, g_mix, w_in, rpb, w_branch_na, w_branch_dil, w_out, g_mlp, w_up, w_down,
           g_ple, w_ple_gate, w_ple_proj, g_final):
    b, s, d = x.shape
    depth = w_in.shape[0]
    bf16 = jnp.bfloat16
    cos_t, sin_t = _rope_tables(positions)
    dil_mask = _dil_mask_tables()
    h = x
    for i in range(depth):
        w_qkv = w_in[i][:, :QKV_WIDTH].astype(bf16)
        w_gate = w_in[i][:, QKV_WIDTH:].astype(bf16)
        nat, *dil_arrays = _qkv_project(h, g_mix[i].reshape(1, d), w_qkv, cos_t, sin_t)
        yna = _na_attention(nat, _na_bias_tables(rpb[i], s // GRID_W))
        outs, lses = [], []
        for g, (window, dil) in enumerate(DIL_GROUPS):
            assert window // (2 * dil) == DIL_RADIUS
            if dil == 1:
                o, lse = _dil_attention(nat.reshape(b, 1, s, nat.shape[-1]), dil_mask, DIL_Q0)
            else:
                o, lse = _dil_attention(dil_arrays[g - 1], dil_mask, 0)
            outs.append(o)
            lses.append(lse)
        weights = [
            g_mix[i].reshape(1, d), w_gate, w_branch_na[i].astype(bf16), w_branch_dil[i].astype(bf16),
            w_out[i].astype(bf16), g_mlp[i].reshape(1, d), w_up[i].astype(bf16), w_down[i].astype(bf16),
            g_ple[i].reshape(1, d), w_ple_gate[i].astype(bf16), w_ple_proj[i].astype(bf16),
            g_final.reshape(1, d),
        ]
        h = _post_block(h, yna, outs, lses, p[i], weights, final_norm=(i == depth - 1))
    return h
```

```python
import functools

import jax
import jax.numpy as jnp
import numpy as np
from jax import lax
from jax.experimental import pallas as pl
from jax.experimental.pallas import tpu as pltpu

HEAD_DIM = 64
GRID_W = 64
NA_HEADS = 8
NA_WIN_ROWS = 8
NA_WIN_COLS = 16
DIL_GROUPS = ((128, 1), (512, 4), (2048, 16))
DIL_HEADS_PER_GROUP = 4
ROPE_THETA = 10000.0
RMS_EPS = 1e-6
NEG_INF = -1e30
LOG2E = 1.4426950408889634

LANES = 256
HEADS_PER_CALL = LANES // HEAD_DIM
NA_WIDTH = NA_HEADS * HEAD_DIM
DIL_WIDTH = DIL_HEADS_PER_GROUP * len(DIL_GROUPS) * HEAD_DIM
QKV_WIDTH = 3 * NA_WIDTH + 3 * DIL_WIDTH
N_QKV_CHUNKS = QKV_WIDTH // LANES
NA_Q0, NA_K0, NA_V0 = 0, 2, 4
DIL_Q0, DIL_K0, DIL_V0 = 6, 9, 12

NA_Q_ROWS = 4
NA_K_ROWS = 12
DIL_Q = 128
DIL_K = 256
DIL_RADIUS = 64

VMEM_LIMIT = 56 * 1024 * 1024


def _rms(x, g):
    ms = jnp.mean(x * x, axis=-1, keepdims=True)
    return x * lax.rsqrt(ms + RMS_EPS) * g


def _sigmoid(x):
    return 1.0 / (1.0 + jnp.exp(-x))


def _first_head_of_tile(rows):
    assert HEAD_DIM * 2 == 128
    return lax.broadcasted_iota(jnp.int32, (rows, 128), 1) < HEAD_DIM


def _stack_head_pair(q_tile):
    first = _first_head_of_tile(q_tile.shape[0])
    zero = jnp.zeros_like(q_tile)
    return jnp.concatenate([jnp.where(first, q_tile, zero), jnp.where(first, zero, q_tile)], axis=0)


def _stack_heads(q):
    zero = jnp.zeros((2 * q.shape[0], 128), q.dtype)
    pairs = [_stack_head_pair(q[:, t * 128:(t + 1) * 128]) for t in range(LANES // 128)]
    return jnp.concatenate([jnp.concatenate([pairs[0], zero], axis=1),
                            jnp.concatenate([zero, pairs[1]], axis=1)], axis=0)


def _unstack_heads(x, m):
    first = _first_head_of_tile(m)
    tiles = []
    for t in range(LANES // 128):
        lanes = slice(t * 128, (t + 1) * 128) if x.shape[1] == LANES else slice(0, 128)
        tiles.append(jnp.where(first, x[2 * t * m:(2 * t + 1) * m, lanes], x[(2 * t + 1) * m:(2 * t + 2) * m, lanes]))
    return jnp.concatenate(tiles, axis=1)


def _rope_kernel(pos_ref, freq_ref, cos_ref, sin_ref):
    ang = pos_ref[...] * freq_ref[...]
    cos_ref[...] = jnp.cos(ang)
    sin_ref[...] = jnp.sin(ang)


def _rope_tables(positions):
    half = HEAD_DIM // 2
    n = positions.size
    inv_freq = ROPE_THETA ** (-jnp.arange(half, dtype=jnp.float32) / half)
    blk = 4096
    return pl.pallas_call(
        _rope_kernel,
        out_shape=(jax.ShapeDtypeStruct((half, n), jnp.float32),) * 2,
        grid=(n // blk,),
        in_specs=[pl.BlockSpec((1, blk), lambda i: (0, i)), pl.BlockSpec((half, 1), lambda i: (0, 0))],
        out_specs=(pl.BlockSpec((half, blk), lambda i: (0, i)),) * 2,
        name="rope_tables",
    )(positions.reshape(1, n).astype(jnp.float32), inv_freq.reshape(half, 1))


def _qkv_kernel(x_ref, g_ref, w_ref, cos_ref, sin_ref, nat_ref, *rest):
    dil_refs, stage_ref = rest[:-1], rest[-1]
    bm = x_ref.shape[0]
    a = _rms(x_ref[...], g_ref[...]).astype(jnp.bfloat16)
    cos_t, sin_t = cos_ref[...], sin_ref[...]
    cos = jnp.concatenate([cos_t] * 4, axis=0).T
    sin = jnp.concatenate([-sin_t, sin_t] * 2, axis=0).T
    cos = jnp.concatenate([cos] * (LANES // 128), axis=1)
    sin = jnp.concatenate([sin] * (LANES // 128), axis=1)
    first_half = (lax.broadcasted_iota(jnp.int32, cos.shape, 1) % HEAD_DIM) < (HEAD_DIM // 2)
    scale = HEAD_DIM ** -0.5 * LOG2E
    n_groups = len(DIL_GROUPS)
    for c in range(N_QKV_CHUNKS):
        acc = jnp.dot(a, w_ref[:, c * LANES:(c + 1) * LANES], preferred_element_type=jnp.float32)
        if DIL_Q0 <= c < DIL_V0:
            swapped = jnp.where(first_half,
                                pltpu.roll(acc, LANES - HEAD_DIM // 2, axis=1),
                                pltpu.roll(acc, HEAD_DIM // 2, axis=1))
            acc = acc * cos + swapped * sin
        if NA_Q0 <= c < NA_K0 or DIL_Q0 <= c < DIL_K0:
            acc = acc * scale
        group, kind = (c - DIL_Q0) % n_groups, (c - DIL_Q0) // n_groups
        if c < DIL_Q0:
            nat_ref[:, c * LANES:(c + 1) * LANES] = acc.astype(nat_ref.dtype)
        elif DIL_GROUPS[group][1] == 1:
            nat_ref[:, (DIL_Q0 + kind) * LANES:(DIL_Q0 + kind + 1) * LANES] = acc.astype(nat_ref.dtype)
        else:
            dil = DIL_GROUPS[group][1]
            o_ref = dil_refs[group - 1]
            for half in range(LANES // 128):
                stage_ref[half] = acc[:, half * 128:(half + 1) * 128]
            for rho in range(dil):
                for half in range(LANES // 128):
                    lane0 = kind * LANES + half * 128
                    o_ref[rho, :, lane0:lane0 + 128] = (
                        stage_ref[half, pl.ds(rho, bm // dil, stride=dil), :].astype(o_ref.dtype))


def _qkv_project(x3, g, w_qkv, cos_t, sin_t, bm=1024):
    b, s, d = x3.shape
    assert DIL_GROUPS[0][1] == 1
    half = HEAD_DIM // 2
    tiles = s // bm
    nat_width = (DIL_Q0 + 3) * LANES
    out_shape = [jax.ShapeDtypeStruct((b, s, nat_width), jnp.bfloat16)]
    out_specs = [pl.BlockSpec((None, bm, nat_width), lambda bi, t: (bi, t, 0))]
    for _, dil in DIL_GROUPS[1:]:
        out_shape.append(jax.ShapeDtypeStruct((b, dil, s // dil, 3 * LANES), jnp.bfloat16))
        out_specs.append(pl.BlockSpec((None, dil, bm // dil, 3 * LANES), lambda bi, t: (bi, 0, t, 0)))
    return pl.pallas_call(
        _qkv_kernel,
        out_shape=out_shape,
        grid=(b, s // bm),
        in_specs=[
            pl.BlockSpec((None, bm, d), lambda bi, t: (bi, t, 0)),
            pl.BlockSpec((1, d), lambda bi, t: (0, 0)),
            pl.BlockSpec((d, QKV_WIDTH), lambda bi, t: (0, 0), pipeline_mode=pl.Buffered(1)),
            pl.BlockSpec((half, bm), lambda bi, t: (0, bi * tiles + t)),
            pl.BlockSpec((half, bm), lambda bi, t: (0, bi * tiles + t)),
        ],
        out_specs=out_specs,
        scratch_shapes=[pltpu.VMEM((LANES // 128, bm, 128), jnp.float32)],
        compiler_params=pltpu.CompilerParams(
            dimension_semantics=("arbitrary",) * 2, vmem_limit_bytes=VMEM_LIMIT),
        name="qkv_project",
    )(x3, g, w_qkv, cos_t, sin_t)


def _na_window_start(row0, rows):
    return jnp.clip(row0 - NA_WIN_ROWS // 2, 0, rows - NA_K_ROWS)


def _na_bias_kernel(rpb_ref, out_ref, *, rows):
    n_off_c = 2 * NA_WIN_COLS - 1
    qc = lax.broadcasted_iota(jnp.int32, (GRID_W, 128), 0)
    kc = lax.broadcasted_iota(jnp.int32, (GRID_W, 128), 1)
    cs = jnp.clip(qc - NA_WIN_COLS // 2, 0, GRID_W - NA_WIN_COLS)
    col_valid = (kc >= cs) & (kc < cs + NA_WIN_COLS)
    neg = jnp.full((GRID_W, GRID_W), NEG_INF, jnp.float32)
    for h in range(HEADS_PER_CALL):
        toeplitz = []
        for ro in range(2 * NA_WIN_ROWS - 1):
            row = jnp.broadcast_to(rpb_ref[h, ro:ro + 1, :], (GRID_W, 128))
            t = pltpu.roll(row, 128 - (n_off_c // 2), axis=1, stride=1, stride_axis=0)
            toeplitz.append(jnp.where(col_valid, t * LOG2E, NEG_INF)[:, :GRID_W])
        for variant, r0 in enumerate((0, NA_Q_ROWS, rows - NA_Q_ROWS)):
            w0 = int(np.clip(r0 - NA_WIN_ROWS // 2, 0, rows - NA_K_ROWS))
            for qr in range(NA_Q_ROWS):
                r = r0 + qr
                rs = int(np.clip(r - NA_WIN_ROWS // 2, 0, rows - NA_WIN_ROWS))
                blocks = [toeplitz[w0 + kr - r + NA_WIN_ROWS - 1] if rs <= w0 + kr < rs + NA_WIN_ROWS else neg
                          for kr in range(NA_K_ROWS)]
                row0 = (h * NA_Q_ROWS + qr) * GRID_W
                out_ref[variant, row0:row0 + GRID_W, :] = jnp.concatenate(blocks, axis=1)


def _na_bias_tables(rpb, rows):
    h, n_ro, n_co = rpb.shape
    rpb_pad = jnp.pad(rpb.astype(jnp.float32), ((0, 0), (0, 16 - n_ro), (0, 128 - n_co)))
    n_groups = h // HEADS_PER_CALL
    q, k = HEADS_PER_CALL * NA_Q_ROWS * GRID_W, NA_K_ROWS * GRID_W
    return pl.pallas_call(
        functools.partial(_na_bias_kernel, rows=rows),
        out_shape=jax.ShapeDtypeStruct((3, n_groups, q, k), jnp.float32),
        grid=(n_groups,),
        in_specs=[pl.BlockSpec((HEADS_PER_CALL, 16, 128), lambda g: (g, 0, 0))],
        out_specs=pl.BlockSpec((3, None, q, k), lambda g: (0, g, 0, 0)),
        compiler_params=pltpu.CompilerParams(dimension_semantics=("arbitrary",), vmem_limit_bytes=VMEM_LIMIT),
        name="na_bias_tables",
    )(rpb_pad)


def _attn_kernel(q_ref, k_ref, v_ref, bias_ref, o_ref, *rest, mq, kw, blocks_per_seq, window_start, row_chunk,
                 with_lse, interior_dead_tiles, split_heads, mxu_row_sums):
    f32 = jnp.float32
    if with_lse:
        lse_ref, *scratch = rest
    else:
        scratch = rest
    s_refs, p_refs, st_refs = scratch[0:2], scratch[2:4], scratch[4:6]
    ST_MAX, ST_INV, ST_LSE = 0, 1, 2
    n_blocks = q_ref.shape[0] * blocks_per_seq
    rows_s = HEADS_PER_CALL * mq
    rows_b = bias_ref.shape[1]
    log2_bps = blocks_per_seq.bit_length() - 1
    assert blocks_per_seq == 1 << log2_bps and n_blocks % 2 == 0 and n_blocks >= 2
    assert rows_s % rows_b == 0 and kw % 128 == 0

    def locate(j):
        j = jnp.int32(j)
        seq = lax.shift_right_logical(j, log2_bps)
        n = j & (blocks_per_seq - 1)
        q0 = pl.multiple_of(n * mq, mq)
        w0 = pl.multiple_of(window_start(n), 64)
        variant = jnp.where(n == 0, 0, jnp.where(n == blocks_per_seq - 1, 2, 1))
        return seq, q0, w0, variant

    def lanes_of(x, width):
        return jnp.concatenate([x] * (width // 128), axis=1)

    def stage_a(j, par):
        seq, q0, w0, variant = locate(j)
        bias = jnp.concatenate([bias_ref[variant]] * (rows_s // rows_b), axis=0)
        nt = (((1,), (1,)), ((), ()))
        if split_heads:
            for t in range(LANES // 128):
                lanes = slice(t * 128, (t + 1) * 128)
                qs = _stack_head_pair(q_ref[seq, pl.ds(q0, mq), lanes])
                s = lax.dot_general(qs, k_ref[seq, pl.ds(w0, kw), lanes], nt, preferred_element_type=f32)
                rows = slice(2 * t * mq, (2 * t + 2) * mq)
                s_refs[par][rows, :] = s + bias[rows]
        else:
            qs = _stack_heads(q_ref[seq, pl.ds(q0, mq), :])
            s = lax.dot_general(qs, k_ref[seq, pl.ds(w0, kw), :], nt, preferred_element_type=f32)
            s_refs[par][...] = s + bias
        for r0 in range(0, rows_s, row_chunk):
            mx = jnp.max(s_refs[par][r0:r0 + row_chunk, :], axis=-1, keepdims=True)
            st_refs[par][ST_MAX, r0:r0 + row_chunk, :] = jnp.broadcast_to(mx, (row_chunk, 128))

    def stage_b(par, p_ref, dead_tiles):
        for r0 in range(0, rows_s, row_chunk):
            mx = st_refs[par][ST_MAX, r0:r0 + row_chunk, :]
            live = [t for t in range(kw // 128) if t not in dead_tiles(r0)]
            s = jnp.concatenate([s_refs[par][r0:r0 + row_chunk, t * 128:(t + 1) * 128] for t in live], axis=1)
            p = jnp.exp2(s - lanes_of(mx, s.shape[1]))
            if not mxu_row_sums:
                l = jnp.broadcast_to(jnp.sum(p, axis=-1, keepdims=True), (row_chunk, 128))
                st_refs[par][ST_INV, r0:r0 + row_chunk, :] = 1.0 / l
                if with_lse:
                    st_refs[par][ST_LSE, r0:r0 + row_chunk, :] = mx + jnp.log2(l)
            p = p.astype(p_ref.dtype)
            for i, t in enumerate(live):
                p_ref[r0:r0 + row_chunk, t * 128:(t + 1) * 128] = p[:, i * 128:(i + 1) * 128]

    def stage_c(j, par, p_ref):
        seq, q0, w0, _ = locate(j)
        if mxu_row_sums:
            first = _first_head_of_tile(mq)
            ones = jnp.ones((kw, 128), p_ref.dtype)
            tiles = []
            for t in range(LANES // 128):
                rhs = jnp.concatenate([v_ref[seq, pl.ds(w0, kw), t * 128:(t + 1) * 128], ones], axis=1)
                pv = jnp.dot(p_ref[2 * t * mq:(2 * t + 2) * mq, :], rhs, preferred_element_type=f32)
                den = jnp.where(first, pv[:mq, 128:], pv[mq:, 128:])
                tiles.append(jnp.where(first, pv[:mq, :128], pv[mq:, :128]) * (1.0 / den))
            o_ref[seq, pl.ds(q0, mq), :] = jnp.concatenate(tiles, axis=1).astype(o_ref.dtype)
            return
        if split_heads:
            first = _first_head_of_tile(mq)
            tiles = []
            for t in range(LANES // 128):
                vwin = v_ref[seq, pl.ds(w0, kw), t * 128:(t + 1) * 128]
                pv = jnp.dot(p_ref[2 * t * mq:(2 * t + 2) * mq, :], vwin, preferred_element_type=f32)
                tiles.append(jnp.where(first, pv[:mq], pv[mq:]))
            out = jnp.concatenate(tiles, axis=1)
        else:
            pv = jnp.dot(p_ref[...], v_ref[seq, pl.ds(w0, kw), :], preferred_element_type=f32)
            out = _unstack_heads(pv, mq)
        out = out * _unstack_heads(st_refs[par][ST_INV], mq)
        o_ref[seq, pl.ds(q0, mq), :] = out.astype(o_ref.dtype)
        if with_lse:
            lse_ref[seq, pl.ds(q0, mq), :] = _unstack_heads(st_refs[par][ST_LSE], mq)

    def no_dead_tiles(r0):
        return ()

    def steady_state(first_pair, dead_tiles):
        def body(i, carry):
            for par in (0, 1):
                j = 2 * i + par
                if mxu_row_sums:
                    stage_c(j - 2, par, p_refs[par])
                stage_a(j, par)
                stage_b(1 - par, p_refs[1 - par], dead_tiles)
                if not mxu_row_sums:
                    stage_c(j - 2, par, p_refs[par])
            return carry

        lax.fori_loop(first_pair, n_blocks // 2, body, 0)

    if interior_dead_tiles is None:
        stage_a(0, 0)
        stage_a(1, 1)
        stage_b(0, p_refs[0], no_dead_tiles)
        steady_state(1, no_dead_tiles)
        stage_b(1, p_refs[1], no_dead_tiles)
        stage_c(n_blocks - 2, 0, p_refs[0])
        stage_c(n_blocks - 1, 1, p_refs[1])
    else:
        p_edge = scratch[6]
        assert q_ref.shape[0] == 1 and n_blocks >= 6
        for p_ref in p_refs:
            for r0 in range(0, rows_s, row_chunk):
                for t in interior_dead_tiles(r0):
                    p_ref[r0:r0 + row_chunk, t * 128:(t + 1) * 128] = jnp.zeros((row_chunk, 128), p_ref.dtype)
        stage_a(0, 0)
        stage_a(1, 1)
        stage_b(0, p_edge, no_dead_tiles)
        stage_a(2, 0)
        stage_b(1, p_refs[1], interior_dead_tiles)
        stage_c(0, 0, p_edge)
        stage_a(3, 1)
        stage_b(0, p_refs[0], interior_dead_tiles)
        stage_c(1, 1, p_refs[1])
        steady_state(2, interior_dead_tiles)
        stage_b(1, p_edge, no_dead_tiles)
        stage_c(n_blocks - 2, 0, p_refs[0])
        stage_c(n_blocks - 1, 1, p_edge)


def _windowed_attention(arrays, chunks, bias, bias_spec, grid, batch_of, lane_chunk_of, out_width, *, name,
                        mq, kw, window_start, row_chunk, with_lse, split_heads, mxu_row_sums=False,
                        interior_dead_tiles=None):
    b, r, sub_len, _ = arrays[0].shape
    blocks_per_seq = sub_len // mq

    def spec(chunk):
        return pl.BlockSpec((None, r, sub_len, LANES),
                            lambda *g: (batch_of(*g), 0, 0, chunk + lane_chunk_of(*g)))

    out_shape = [jax.ShapeDtypeStruct((b, r, sub_len, out_width), jnp.bfloat16)]
    out_specs = [spec(0)]
    if with_lse:
        out_shape.append(jax.ShapeDtypeStruct((b, r, sub_len, LANES), jnp.float32))
        out_specs.append(spec(0))
    rows_s = HEADS_PER_CALL * mq
    scratch = ([pltpu.VMEM((rows_s, kw), jnp.float32)] * 2 + [pltpu.VMEM((rows_s, kw), jnp.bfloat16)] * 2
               + [pltpu.VMEM((3 if with_lse else 2, rows_s, 128), jnp.float32)] * 2)
    if interior_dead_tiles is not None:
        scratch.append(pltpu.VMEM((rows_s, kw), jnp.bfloat16))
    return pl.pallas_call(
        functools.partial(_attn_kernel, mq=mq, kw=kw, blocks_per_seq=blocks_per_seq, window_start=window_start,
                          row_chunk=row_chunk, with_lse=with_lse, interior_dead_tiles=interior_dead_tiles,
                          split_heads=split_heads, mxu_row_sums=mxu_row_sums),
        out_shape=out_shape,
        grid=grid,
        in_specs=[spec(c) for c in chunks] + [bias_spec],
        out_specs=out_specs,
        scratch_shapes=scratch,
        compiler_params=pltpu.CompilerParams(
            dimension_semantics=("arbitrary",) * len(grid), vmem_limit_bytes=VMEM_LIMIT),
        name=name,
    )(*arrays, bias)


def _na_attention(nat, bias):
    b, seq, c = nat.shape
    rows = seq // GRID_W
    n_groups = NA_HEADS // HEADS_PER_CALL
    arr = nat.reshape(b, 1, seq, c)
    m_q = NA_Q_ROWS * GRID_W
    for blk in range(1, rows // NA_Q_ROWS - 1):
        r0 = blk * NA_Q_ROWS
        assert 0 <= r0 - NA_WIN_ROWS // 2 <= rows - NA_K_ROWS
        assert r0 + NA_Q_ROWS - 1 - NA_WIN_ROWS // 2 <= rows - NA_WIN_ROWS

    def interior_dead_tiles(row):
        qr = (row % m_q) // GRID_W
        per_tile = 128 // GRID_W
        return tuple(t for t in range(NA_K_ROWS // per_tile)
                     if not any(0 <= kr - qr < NA_WIN_ROWS for kr in range(t * per_tile, (t + 1) * per_tile)))

    (out,) = _windowed_attention(
        (arr, arr, arr), (NA_Q0, NA_K0, NA_V0), bias,
        pl.BlockSpec((3, None) + bias.shape[2:], lambda g, bi: (0, g, 0, 0)),
        grid=(n_groups, b), batch_of=lambda g, bi: bi, lane_chunk_of=lambda g, bi: g, out_width=NA_WIDTH,
        name="na_attention", mq=m_q, kw=NA_K_ROWS * GRID_W,
        window_start=lambda n: _na_window_start(n * NA_Q_ROWS, rows) * GRID_W, row_chunk=32, with_lse=False,
        split_heads=False, mxu_row_sums=True, interior_dead_tiles=interior_dead_tiles)
    return out.reshape(b, seq, NA_WIDTH)


def _dil_mask_tables():
    i = np.arange(DIL_Q)[:, None]
    j = np.arange(DIL_K)[None, :]
    tabs = []
    for off in (0, DIL_RADIUS, DIL_K - DIL_Q):
        tabs.append(np.where(np.abs(off + i - j) <= DIL_RADIUS, 0.0, NEG_INF))
    return jnp.asarray(np.stack(tabs), dtype=jnp.float32)


def _dil_attention(arr, mask, chunk0):
    b, dil, sub_len, _ = arr.shape
    return _windowed_attention(
        (arr, arr, arr), (chunk0, chunk0 + 1, chunk0 + 2), mask,
        pl.BlockSpec(mask.shape, lambda bi: (0, 0, 0)),
        grid=(b,), batch_of=lambda bi: bi, lane_chunk_of=lambda bi: 0, out_width=LANES,
        name=f"dil_attention_d{dil}", mq=DIL_Q, kw=DIL_K,
        window_start=lambda n: jnp.clip(n * DIL_Q - DIL_RADIUS, 0, sub_len - DIL_K), row_chunk=DIL_Q, with_lse=True,
        split_heads=True)


def _post_kernel(x_ref, yna_ref, o1_ref, o2_ref, o3_ref, l1_ref, l2_ref, l3_ref, p_ref,
                 gmix_ref, wgate_ref, wbna_ref, wbdil_ref, wout_ref,
                 gmlp_ref, wup_ref, wdown_ref, gple_ref, wpg_ref, wpp_ref, gfin_ref,
                 out_ref, *stage_refs, final_norm, ff_chunk, sub_rows):
    f32, bf16 = jnp.float32, jnp.bfloat16
    bm, d = x_ref.shape
    d_ff = wup_ref.shape[1]

    stages = iter(stage_refs)
    token_order = []
    for ref in (o1_ref, o2_ref, o3_ref, l1_ref, l2_ref, l3_ref):
        dil = ref.shape[0]
        if dil == 1:
            token_order.append(lambda r0, ref=ref: ref[0, r0:r0 + sub_rows, :].astype(f32))
            continue
        stage_ref = next(stages)
        for rho in range(dil):
            blk = ref[rho].astype(f32)
            for half in range(LANES // 128):
                stage_ref[half, pl.ds(rho, bm // dil, stride=dil), :] = blk[:, half * 128:(half + 1) * 128]
        token_order.append(lambda r0, stage_ref=stage_ref: jnp.concatenate(
            [stage_ref[half, r0:r0 + sub_rows, :] for half in range(LANES // 128)], axis=1))

    def dot(lhs, w):
        return jnp.dot(lhs, w, preferred_element_type=f32)

    chains = [dict(r0=r0) for r0 in range(0, bm, sub_rows)]
    for c in chains:
        r0 = c["r0"]
        c["x"] = x_ref[r0:r0 + sub_rows, :]
        a = _rms(c["x"], gmix_ref[...]).astype(bf16)
        c["gate_na"] = _sigmoid(dot(a, wgate_ref[:, :d]))
        c["gate_dil"] = _sigmoid(dot(a, wgate_ref[:, d:]))
    for c in chains:
        r0 = c["r0"]
        o1, o2, o3, l1, l2, l3 = [get(r0) for get in token_order]
        mx = jnp.maximum(jnp.maximum(l1, l2), l3)
        e1, e2, e3 = jnp.exp2(l1 - mx), jnp.exp2(l2 - mx), jnp.exp2(l3 - mx)
        ydil = (e1 * o1 + e2 * o2 + e3 * o3) * (1.0 / (e1 + e2 + e3))
        c["mixed"] = (c.pop("gate_na") * dot(yna_ref[r0:r0 + sub_rows, :], wbna_ref[...])
                      + c.pop("gate_dil") * dot(ydil.astype(bf16), wbdil_ref[...]))
    for c in chains:
        c["h"] = c.pop("x") + dot(c.pop("mixed").astype(bf16), wout_ref[...])
        c["c"] = _rms(c["h"], gmlp_ref[...]).astype(bf16)
        c["acc"] = jnp.zeros_like(c["h"])
    for f in range(d_ff // ff_chunk):
        for c in chains:
            u = dot(c["c"], wup_ref[:, f * ff_chunk:(f + 1) * ff_chunk])
            c["u"] = jnp.square(jnp.maximum(u, 0.0)).astype(bf16)
        for c in chains:
            c["acc"] = c["acc"] + dot(c.pop("u"), wdown_ref[f * ff_chunk:(f + 1) * ff_chunk, :])
    for c in chains:
        r0 = c["r0"]
        c["h"] = c["h"] + c.pop("acc")
        e = _rms(c["h"], gple_ref[...]).astype(bf16)
        c["pg"] = _sigmoid(dot(e, wpg_ref[...]))
        c["pp"] = dot(p_ref[r0:r0 + sub_rows, :].astype(bf16), wpp_ref[...])
    for c in chains:
        r0 = c["r0"]
        h = c["h"] + c["pg"] * c["pp"]
        if final_norm:
            h = _rms(h, gfin_ref[...])
        out_ref[r0:r0 + sub_rows, :] = h


def _post_block(x3, yna, outs, lses, p3, weights, final_norm, bm=512, sub_rows=256, ff_chunk=1024):
    b, s, d = x3.shape

    def rows(arr):
        return pl.BlockSpec((None, bm, arr.shape[-1]), lambda bi, t: (bi, t, 0))

    def residues(arr):
        dil = arr.shape[1]
        return pl.BlockSpec((None, dil, bm // dil, arr.shape[-1]), lambda bi, t: (bi, 0, t, 0))

    def whole(arr):
        return pl.BlockSpec(arr.shape, lambda bi, t: (0, 0), pipeline_mode=pl.Buffered(1))

    n_stages = sum(a.shape[1] > 1 for a in (*outs, *lses))
    return pl.pallas_call(
        functools.partial(_post_kernel, final_norm=final_norm, ff_chunk=ff_chunk, sub_rows=sub_rows),
        out_shape=jax.ShapeDtypeStruct((b, s, d), jnp.float32),
        grid=(b, s // bm),
        in_specs=([rows(x3), rows(yna)] + [residues(a) for a in (*outs, *lses)] + [rows(p3)]
                  + [whole(w) for w in weights]),
        out_specs=rows(x3),
        scratch_shapes=[pltpu.VMEM((LANES // 128, bm, 128), jnp.float32)] * n_stages,
        compiler_params=pltpu.CompilerParams(
            dimension_semantics=("arbitrary",) * 2, vmem_limit_bytes=VMEM_LIMIT),
        name="post_block",
    )(x3, yna, *outs, *lses, p3, *weights)


def kernel(x, p, positions, g_mix, w_in, rpb, w_branch_na, w_branch_dil, w_out, g_mlp, w_up, w_down,
           g_ple, w_ple_gate, w_ple_proj, g_final):
    b, s, d = x.shape
    depth = w_in.shape[0]
    bf16 = jnp.bfloat16
    cos_t, sin_t = _rope_tables(positions)
    dil_mask = _dil_mask_tables()
    h = x
    for i in range(depth):
        w_qkv = w_in[i][:, :QKV_WIDTH].astype(bf16)
        w_gate = w_in[i][:, QKV_WIDTH:].astype(bf16)
        nat, *dil_arrays = _qkv_project(h, g_mix[i].reshape(1, d), w_qkv, cos_t, sin_t)
        yna = _na_attention(nat, _na_bias_tables(rpb[i], s // GRID_W))
        outs, lses = [], []
        for g, (window, dil) in enumerate(DIL_GROUPS):
            assert window // (2 * dil) == DIL_RADIUS
            if dil == 1:
                o, lse = _dil_attention(nat.reshape(b, 1, s, nat.shape[-1]), dil_mask, DIL_Q0)
            else:
                o, lse = _dil_attention(dil_arrays[g - 1], dil_mask, 0)
            outs.append(o)
            lses.append(lse)
        weights = [
            g_mix[i].reshape(1, d), w_gate, w_branch_na[i].astype(bf16), w_branch_dil[i].astype(bf16),
            w_out[i].astype(bf16), g_mlp[i].reshape(1, d), w_up[i].astype(bf16), w_down[i].astype(bf16),
            g_ple[i].reshape(1, d), w_ple_gate[i].astype(bf16), w_ple_proj[i].astype(bf16),
            g_final.reshape(1, d),
        ]
        h = _post_block(h, yna, outs, lses, p[i], weights, final_norm=(i == depth - 1))
    return h
```

```python
import functools

import jax
import jax.numpy as jnp
import numpy as np
from jax import lax
from jax.experimental import pallas as pl
from jax.experimental.pallas import tpu as pltpu

HEAD_DIM = 64
GRID_W = 64
NA_HEADS = 8
NA_WIN_ROWS = 8
NA_WIN_COLS = 16
DIL_GROUPS = ((128, 1), (512, 4), (2048, 16))
DIL_HEADS_PER_GROUP = 4
ROPE_THETA = 10000.0
RMS_EPS = 1e-6
NEG_INF = -1e30
LOG2E = 1.4426950408889634

LANES = 256
HEADS_PER_CALL = LANES // HEAD_DIM
NA_WIDTH = NA_HEADS * HEAD_DIM
DIL_WIDTH = DIL_HEADS_PER_GROUP * len(DIL_GROUPS) * HEAD_DIM
QKV_WIDTH = 3 * NA_WIDTH + 3 * DIL_WIDTH
N_QKV_CHUNKS = QKV_WIDTH // LANES
NA_Q0, NA_K0, NA_V0 = 0, 2, 4
DIL_Q0, DIL_K0, DIL_V0 = 6, 9, 12

NA_Q_ROWS = 4
NA_K_ROWS = 12
DIL_Q = 128
DIL_K = 256
DIL_RADIUS = 64

VMEM_LIMIT = 56 * 1024 * 1024


def _rms(x, g):
    ms = jnp.mean(x * x, axis=-1, keepdims=True)
    return x * lax.rsqrt(ms + RMS_EPS) * g


def _sigmoid(x):
    return 1.0 / (1.0 + jnp.exp(-x))


def _first_head_of_tile(rows):
    assert HEAD_DIM * 2 == 128
    return lax.broadcasted_iota(jnp.int32, (rows, 128), 1) < HEAD_DIM


def _stack_head_pair(q_tile):
    first = _first_head_of_tile(q_tile.shape[0])
    zero = jnp.zeros_like(q_tile)
    return jnp.concatenate([jnp.where(first, q_tile, zero), jnp.where(first, zero, q_tile)], axis=0)


def _stack_heads(q):
    zero = jnp.zeros((2 * q.shape[0], 128), q.dtype)
    pairs = [_stack_head_pair(q[:, t * 128:(t + 1) * 128]) for t in range(LANES // 128)]
    return jnp.concatenate([jnp.concatenate([pairs[0], zero], axis=1),
                            jnp.concatenate([zero, pairs[1]], axis=1)], axis=0)


def _unstack_heads(x, m):
    first = _first_head_of_tile(m)
    tiles = []
    for t in range(LANES // 128):
        lanes = slice(t * 128, (t + 1) * 128) if x.shape[1] == LANES else slice(0, 128)
        tiles.append(jnp.where(first, x[2 * t * m:(2 * t + 1) * m, lanes], x[(2 * t + 1) * m:(2 * t + 2) * m, lanes]))
    return jnp.concatenate(tiles, axis=1)


def _rope_kernel(pos_ref, freq_ref, cos_ref, sin_ref):
    ang = pos_ref[...] * freq_ref[...]
    cos_ref[...] = jnp.cos(ang)
    sin_ref[...] = jnp.sin(ang)


def _rope_tables(positions):
    half = HEAD_DIM // 2
    n = positions.size
    inv_freq = ROPE_THETA ** (-jnp.arange(half, dtype=jnp.float32) / half)
    blk = 4096
    return pl.pallas_call(
        _rope_kernel,
        out_shape=(jax.ShapeDtypeStruct((half, n), jnp.float32),) * 2,
        grid=(n // blk,),
        in_specs=[pl.BlockSpec((1, blk), lambda i: (0, i)), pl.BlockSpec((half, 1), lambda i: (0, 0))],
        out_specs=(pl.BlockSpec((half, blk), lambda i: (0, i)),) * 2,
        name="rope_tables",
    )(positions.reshape(1, n).astype(jnp.float32), inv_freq.reshape(half, 1))


def _qkv_kernel(x_ref, g_ref, w_ref, cos_ref, sin_ref, nat_ref, *rest):
    dil_refs, stage_ref = rest[:-1], rest[-1]
    bm = x_ref.shape[0]
    a = _rms(x_ref[...], g_ref[...]).astype(jnp.bfloat16)
    cos_t, sin_t = cos_ref[...], sin_ref[...]
    cos = jnp.concatenate([cos_t] * 4, axis=0).T
    sin = jnp.concatenate([-sin_t, sin_t] * 2, axis=0).T
    cos = jnp.concatenate([cos] * (LANES // 128), axis=1)
    sin = jnp.concatenate([sin] * (LANES // 128), axis=1)
    first_half = (lax.broadcasted_iota(jnp.int32, cos.shape, 1) % HEAD_DIM) < (HEAD_DIM // 2)
    scale = HEAD_DIM ** -0.5 * LOG2E
    n_groups = len(DIL_GROUPS)
    for c in range(N_QKV_CHUNKS):
        acc = jnp.dot(a, w_ref[:, c * LANES:(c + 1) * LANES], preferred_element_type=jnp.float32)
        if DIL_Q0 <= c < DIL_V0:
            swapped = jnp.where(first_half,
                                pltpu.roll(acc, LANES - HEAD_DIM // 2, axis=1),
                                pltpu.roll(acc, HEAD_DIM // 2, axis=1))
            acc = acc * cos + swapped * sin
        if NA_Q0 <= c < NA_K0 or DIL_Q0 <= c < DIL_K0:
            acc = acc * scale
        group, kind = (c - DIL_Q0) % n_groups, (c - DIL_Q0) // n_groups
        if c < DIL_Q0:
            nat_ref[c, 0] = acc.astype(nat_ref.dtype)
        elif DIL_GROUPS[group][1] == 1:
            nat_ref[DIL_Q0 + kind, 0] = acc.astype(nat_ref.dtype)
        else:
            dil = DIL_GROUPS[group][1]
            o_ref = dil_refs[group - 1]
            for half in range(LANES // 128):
                stage_ref[half] = acc[:, half * 128:(half + 1) * 128]
            for rho in range(dil):
                for half in range(LANES // 128):
                    o_ref[kind, rho, :, half * 128:(half + 1) * 128] = (
                        stage_ref[half, pl.ds(rho, bm // dil, stride=dil), :].astype(o_ref.dtype))


def _qkv_project(x3, g, w_qkv, cos_t, sin_t, bm=1024):
    b, s, d = x3.shape
    assert DIL_GROUPS[0][1] == 1
    half = HEAD_DIM // 2
    tiles = s // bm
    out_shape = [jax.ShapeDtypeStruct((b, DIL_Q0 + 3, 1, s, LANES), jnp.bfloat16)]
    out_specs = [pl.BlockSpec((None, DIL_Q0 + 3, 1, bm, LANES), lambda bi, t: (bi, 0, 0, t, 0))]
    for _, dil in DIL_GROUPS[1:]:
        out_shape.append(jax.ShapeDtypeStruct((b, 3, dil, s // dil, LANES), jnp.bfloat16))
        out_specs.append(pl.BlockSpec((None, 3, dil, bm // dil, LANES), lambda bi, t: (bi, 0, 0, t, 0)))
    return pl.pallas_call(
        _qkv_kernel,
        out_shape=out_shape,
        grid=(b, s // bm),
        in_specs=[
            pl.BlockSpec((None, bm, d), lambda bi, t: (bi, t, 0)),
            pl.BlockSpec((1, d), lambda bi, t: (0, 0)),
            pl.BlockSpec((d, QKV_WIDTH), lambda bi, t: (0, 0), pipeline_mode=pl.Buffered(1)),
            pl.BlockSpec((half, bm), lambda bi, t: (0, bi * tiles + t)),
            pl.BlockSpec((half, bm), lambda bi, t: (0, bi * tiles + t)),
        ],
        out_specs=out_specs,
        scratch_shapes=[pltpu.VMEM((LANES // 128, bm, 128), jnp.float32)],
        compiler_params=pltpu.CompilerParams(
            dimension_semantics=("arbitrary",) * 2, vmem_limit_bytes=VMEM_LIMIT),
        name="qkv_project",
    )(x3, g, w_qkv, cos_t, sin_t)


def _na_window_start(row0, rows):
    return jnp.clip(row0 - NA_WIN_ROWS // 2, 0, rows - NA_K_ROWS)


def _na_bias_kernel(rpb_ref, out_ref, *, rows):
    n_off_c = 2 * NA_WIN_COLS - 1
    qc = lax.broadcasted_iota(jnp.int32, (GRID_W, 128), 0)
    kc = lax.broadcasted_iota(jnp.int32, (GRID_W, 128), 1)
    cs = jnp.clip(qc - NA_WIN_COLS // 2, 0, GRID_W - NA_WIN_COLS)
    col_valid = (kc >= cs) & (kc < cs + NA_WIN_COLS)
    neg = jnp.full((GRID_W, GRID_W), NEG_INF, jnp.float32)
    for h in range(HEADS_PER_CALL):
        toeplitz = []
        for ro in range(2 * NA_WIN_ROWS - 1):
            row = jnp.broadcast_to(rpb_ref[h, ro:ro + 1, :], (GRID_W, 128))
            t = pltpu.roll(row, 128 - (n_off_c // 2), axis=1, stride=1, stride_axis=0)
            toeplitz.append(jnp.where(col_valid, t * LOG2E, NEG_INF)[:, :GRID_W])
        for variant, r0 in enumerate((0, NA_Q_ROWS, rows - NA_Q_ROWS)):
            w0 = int(np.clip(r0 - NA_WIN_ROWS // 2, 0, rows - NA_K_ROWS))
            for qr in range(NA_Q_ROWS):
                r = r0 + qr
                rs = int(np.clip(r - NA_WIN_ROWS // 2, 0, rows - NA_WIN_ROWS))
                blocks = [toeplitz[w0 + kr - r + NA_WIN_ROWS - 1] if rs <= w0 + kr < rs + NA_WIN_ROWS else neg
                          for kr in range(NA_K_ROWS)]
                row0 = (h * NA_Q_ROWS + qr) * GRID_W
                out_ref[variant, row0:row0 + GRID_W, :] = jnp.concatenate(blocks, axis=1)


def _na_bias_tables(rpb, rows):
    h, n_ro, n_co = rpb.shape
    rpb_pad = jnp.pad(rpb.astype(jnp.float32), ((0, 0), (0, 16 - n_ro), (0, 128 - n_co)))
    n_groups = h // HEADS_PER_CALL
    q, k = HEADS_PER_CALL * NA_Q_ROWS * GRID_W, NA_K_ROWS * GRID_W
    return pl.pallas_call(
        functools.partial(_na_bias_kernel, rows=rows),
        out_shape=jax.ShapeDtypeStruct((3, n_groups, q, k), jnp.float32),
        grid=(n_groups,),
        in_specs=[pl.BlockSpec((HEADS_PER_CALL, 16, 128), lambda g: (g, 0, 0))],
        out_specs=pl.BlockSpec((3, None, q, k), lambda g: (0, g, 0, 0)),
        compiler_params=pltpu.CompilerParams(dimension_semantics=("arbitrary",), vmem_limit_bytes=VMEM_LIMIT),
        name="na_bias_tables",
    )(rpb_pad)


def _attn_kernel(q_ref, k_ref, v_ref, bias_ref, o_ref, *rest, mq, kw, blocks_per_seq, window_start, row_chunk,
                 with_lse, interior_dead_tiles, split_heads, mxu_row_sums):
    f32 = jnp.float32
    if with_lse:
        lse_ref, *scratch = rest
    else:
        scratch = rest
    s_refs, p_refs, st_refs = scratch[0:2], scratch[2:4], scratch[4:6]
    ST_MAX, ST_INV, ST_LSE = 0, 1, 2
    n_blocks = q_ref.shape[0] * blocks_per_seq
    rows_s = HEADS_PER_CALL * mq
    rows_b = bias_ref.shape[1]
    log2_bps = blocks_per_seq.bit_length() - 1
    assert blocks_per_seq == 1 << log2_bps and n_blocks % 2 == 0 and n_blocks >= 2
    assert rows_s % rows_b == 0 and kw % 128 == 0

    def locate(j):
        j = jnp.int32(j)
        seq = lax.shift_right_logical(j, log2_bps)
        n = j & (blocks_per_seq - 1)
        q0 = pl.multiple_of(n * mq, mq)
        w0 = pl.multiple_of(window_start(n), 64)
        variant = jnp.where(n == 0, 0, jnp.where(n == blocks_per_seq - 1, 2, 1))
        return seq, q0, w0, variant

    def lanes_of(x, width):
        return jnp.concatenate([x] * (width // 128), axis=1)

    def stage_a(j, par):
        seq, q0, w0, variant = locate(j)
        bias = jnp.concatenate([bias_ref[variant]] * (rows_s // rows_b), axis=0)
        nt = (((1,), (1,)), ((), ()))
        if split_heads:
            for t in range(LANES // 128):
                lanes = slice(t * 128, (t + 1) * 128)
                qs = _stack_head_pair(q_ref[seq, pl.ds(q0, mq), lanes])
                s = lax.dot_general(qs, k_ref[seq, pl.ds(w0, kw), lanes], nt, preferred_element_type=f32)
                rows = slice(2 * t * mq, (2 * t + 2) * mq)
                s_refs[par][rows, :] = s + bias[rows]
        else:
            qs = _stack_heads(q_ref[seq, pl.ds(q0, mq), :])
            s = lax.dot_general(qs, k_ref[seq, pl.ds(w0, kw), :], nt, preferred_element_type=f32)
            s_refs[par][...] = s + bias
        for r0 in range(0, rows_s, row_chunk):
            mx = jnp.max(s_refs[par][r0:r0 + row_chunk, :], axis=-1, keepdims=True)
            st_refs[par][ST_MAX, r0:r0 + row_chunk, :] = jnp.broadcast_to(mx, (row_chunk, 128))

    def stage_b(par, p_ref, dead_tiles):
        for r0 in range(0, rows_s, row_chunk):
            mx = st_refs[par][ST_MAX, r0:r0 + row_chunk, :]
            live = [t for t in range(kw // 128) if t not in dead_tiles(r0)]
            s = jnp.concatenate([s_refs[par][r0:r0 + row_chunk, t * 128:(t + 1) * 128] for t in live], axis=1)
            p = jnp.exp2(s - lanes_of(mx, s.shape[1]))
            if not mxu_row_sums:
                l = jnp.broadcast_to(jnp.sum(p, axis=-1, keepdims=True), (row_chunk, 128))
                st_refs[par][ST_INV, r0:r0 + row_chunk, :] = 1.0 / l
                if with_lse:
                    st_refs[par][ST_LSE, r0:r0 + row_chunk, :] = mx + jnp.log2(l)
            p = p.astype(p_ref.dtype)
            for i, t in enumerate(live):
                p_ref[r0:r0 + row_chunk, t * 128:(t + 1) * 128] = p[:, i * 128:(i + 1) * 128]

    def stage_c(j, par, p_ref):
        seq, q0, w0, _ = locate(j)
        if mxu_row_sums:
            first = _first_head_of_tile(mq)
            ones = jnp.ones((kw, 128), p_ref.dtype)
            tiles = []
            for t in range(LANES // 128):
                rhs = jnp.concatenate([v_ref[seq, pl.ds(w0, kw), t * 128:(t + 1) * 128], ones], axis=1)
                pv = jnp.dot(p_ref[2 * t * mq:(2 * t + 2) * mq, :], rhs, preferred_element_type=f32)
                den = jnp.where(first, pv[:mq, 128:], pv[mq:, 128:])
                tiles.append(jnp.where(first, pv[:mq, :128], pv[mq:, :128]) * (1.0 / den))
            o_ref[seq, pl.ds(q0, mq), :] = jnp.concatenate(tiles, axis=1).astype(o_ref.dtype)
            return
        if split_heads:
            first = _first_head_of_tile(mq)
            tiles = []
            for t in range(LANES // 128):
                vwin = v_ref[seq, pl.ds(w0, kw), t * 128:(t + 1) * 128]
                pv = jnp.dot(p_ref[2 * t * mq:(2 * t + 2) * mq, :], vwin, preferred_element_type=f32)
                tiles.append(jnp.where(first, pv[:mq], pv[mq:]))
            out = jnp.concatenate(tiles, axis=1)
        else:
            pv = jnp.dot(p_ref[...], v_ref[seq, pl.ds(w0, kw), :], preferred_element_type=f32)
            out = _unstack_heads(pv, mq)
        out = out * _unstack_heads(st_refs[par][ST_INV], mq)
        o_ref[seq, pl.ds(q0, mq), :] = out.astype(o_ref.dtype)
        if with_lse:
            lse_ref[seq, pl.ds(q0, mq), :] = _unstack_heads(st_refs[par][ST_LSE], mq)

    def no_dead_tiles(r0):
        return ()

    def steady_state(first_pair, dead_tiles):
        def body(i, carry):
            for par in (0, 1):
                j = 2 * i + par
                if mxu_row_sums:
                    stage_c(j - 2, par, p_refs[par])
                stage_a(j, par)
                stage_b(1 - par, p_refs[1 - par], dead_tiles)
                if not mxu_row_sums:
                    stage_c(j - 2, par, p_refs[par])
            return carry

        lax.fori_loop(first_pair, n_blocks // 2, body, 0)

    if interior_dead_tiles is None:
        stage_a(0, 0)
        stage_a(1, 1)
        stage_b(0, p_refs[0], no_dead_tiles)
        steady_state(1, no_dead_tiles)
        stage_b(1, p_refs[1], no_dead_tiles)
        stage_c(n_blocks - 2, 0, p_refs[0])
        stage_c(n_blocks - 1, 1, p_refs[1])
    else:
        p_edge = scratch[6]
        assert q_ref.shape[0] == 1 and n_blocks >= 6
        for p_ref in p_refs:
            for r0 in range(0, rows_s, row_chunk):
                for t in interior_dead_tiles(r0):
                    p_ref[r0:r0 + row_chunk, t * 128:(t + 1) * 128] = jnp.zeros((row_chunk, 128), p_ref.dtype)
        stage_a(0, 0)
        stage_a(1, 1)
        stage_b(0, p_edge, no_dead_tiles)
        stage_a(2, 0)
        stage_b(1, p_refs[1], interior_dead_tiles)
        stage_c(0, 0, p_edge)
        stage_a(3, 1)
        stage_b(0, p_refs[0], interior_dead_tiles)
        stage_c(1, 1, p_refs[1])
        steady_state(2, interior_dead_tiles)
        stage_b(1, p_edge, no_dead_tiles)
        stage_c(n_blocks - 2, 0, p_refs[0])
        stage_c(n_blocks - 1, 1, p_edge)


def _windowed_attention(arrays, chunks, bias, bias_spec, grid, batch_of, chunk_of, out_chunks, *, name,
                        mq, kw, window_start, row_chunk, with_lse, split_heads, mxu_row_sums=False,
                        interior_dead_tiles=None):
    b, _, r, sub_len, _ = arrays[0].shape
    blocks_per_seq = sub_len // mq

    def spec(chunk):
        return pl.BlockSpec((None, None, r, sub_len, LANES),
                            lambda *g: (batch_of(*g), chunk + chunk_of(*g), 0, 0, 0))

    out_shape = [jax.ShapeDtypeStruct((b, out_chunks, r, sub_len, LANES), jnp.bfloat16)]
    out_specs = [spec(0)]
    if with_lse:
        out_shape.append(jax.ShapeDtypeStruct((b, out_chunks, r, sub_len, LANES), jnp.float32))
        out_specs.append(spec(0))
    rows_s = HEADS_PER_CALL * mq
    scratch = ([pltpu.VMEM((rows_s, kw), jnp.float32)] * 2 + [pltpu.VMEM((rows_s, kw), jnp.bfloat16)] * 2
               + [pltpu.VMEM((3 if with_lse else 2, rows_s, 128), jnp.float32)] * 2)
    if interior_dead_tiles is not None:
        scratch.append(pltpu.VMEM((rows_s, kw), jnp.bfloat16))
    return pl.pallas_call(
        functools.partial(_attn_kernel, mq=mq, kw=kw, blocks_per_seq=blocks_per_seq, window_start=window_start,
                          row_chunk=row_chunk, with_lse=with_lse, interior_dead_tiles=interior_dead_tiles,
                          split_heads=split_heads, mxu_row_sums=mxu_row_sums),
        out_shape=out_shape,
        grid=grid,
        in_specs=[spec(c) for c in chunks] + [bias_spec],
        out_specs=out_specs,
        scratch_shapes=scratch,
        compiler_params=pltpu.CompilerParams(
            dimension_semantics=("arbitrary",) * len(grid), vmem_limit_bytes=VMEM_LIMIT),
        name=name,
    )(*arrays, bias)


def _na_attention(nat, bias):
    b, _, _, seq, _ = nat.shape
    rows = seq // GRID_W
    n_groups = NA_HEADS // HEADS_PER_CALL
    m_q = NA_Q_ROWS * GRID_W
    for blk in range(1, rows // NA_Q_ROWS - 1):
        r0 = blk * NA_Q_ROWS
        assert 0 <= r0 - NA_WIN_ROWS // 2 <= rows - NA_K_ROWS
        assert r0 + NA_Q_ROWS - 1 - NA_WIN_ROWS // 2 <= rows - NA_WIN_ROWS

    def interior_dead_tiles(row):
        qr = (row % m_q) // GRID_W
        per_tile = 128 // GRID_W
        return tuple(t for t in range(NA_K_ROWS // per_tile)
                     if not any(0 <= kr - qr < NA_WIN_ROWS for kr in range(t * per_tile, (t + 1) * per_tile)))

    (out,) = _windowed_attention(
        (nat, nat, nat), (NA_Q0, NA_K0, NA_V0), bias,
        pl.BlockSpec((3, None) + bias.shape[2:], lambda g, bi: (0, g, 0, 0)),
        grid=(n_groups, b), batch_of=lambda g, bi: bi, chunk_of=lambda g, bi: g, out_chunks=n_groups,
        name="na_attention", mq=m_q, kw=NA_K_ROWS * GRID_W,
        window_start=lambda n: _na_window_start(n * NA_Q_ROWS, rows) * GRID_W, row_chunk=32, with_lse=False,
        split_heads=False, mxu_row_sums=True, interior_dead_tiles=interior_dead_tiles)
    return out.reshape(b, n_groups, seq, LANES)


def _dil_mask_tables():
    i = np.arange(DIL_Q)[:, None]
    j = np.arange(DIL_K)[None, :]
    tabs = []
    for off in (0, DIL_RADIUS, DIL_K - DIL_Q):
        tabs.append(np.where(np.abs(off + i - j) <= DIL_RADIUS, 0.0, NEG_INF))
    return jnp.asarray(np.stack(tabs), dtype=jnp.float32)


def _dil_attention(arr, mask, chunk0):
    b, _, dil, sub_len, _ = arr.shape
    o, lse = _windowed_attention(
        (arr, arr, arr), (chunk0, chunk0 + 1, chunk0 + 2), mask,
        pl.BlockSpec(mask.shape, lambda bi: (0, 0, 0)),
        grid=(b,), batch_of=lambda bi: bi, chunk_of=lambda bi: 0, out_chunks=1,
        name=f"dil_attention_d{dil}", mq=DIL_Q, kw=DIL_K,
        window_start=lambda n: jnp.clip(n * DIL_Q - DIL_RADIUS, 0, sub_len - DIL_K), row_chunk=DIL_Q, with_lse=True,
        split_heads=True)
    return o.reshape(b, dil, sub_len, LANES), lse.reshape(b, dil, sub_len, LANES)


def _post_kernel(x_ref, yna_ref, o1_ref, o2_ref, o3_ref, l1_ref, l2_ref, l3_ref, p_ref,
                 gmix_ref, wgate_ref, wbna_ref, wbdil_ref, wout_ref,
                 gmlp_ref, wup_ref, wdown_ref, gple_ref, wpg_ref, wpp_ref, gfin_ref,
                 out_ref, *stage_refs, final_norm, ff_chunk, sub_rows):
    f32, bf16 = jnp.float32, jnp.bfloat16
    bm, d = x_ref.shape
    d_ff = wup_ref.shape[1]

    stages = iter(stage_refs)
    token_order = []
    for ref in (o1_ref, o2_ref, o3_ref, l1_ref, l2_ref, l3_ref):
        dil = ref.shape[0]
        if dil == 1:
            token_order.append(lambda r0, ref=ref: ref[0, r0:r0 + sub_rows, :].astype(f32))
            continue
        stage_ref = next(stages)
        for rho in range(dil):
            blk = ref[rho].astype(f32)
            for half in range(LANES // 128):
                stage_ref[half, pl.ds(rho, bm // dil, stride=dil), :] = blk[:, half * 128:(half + 1) * 128]
        token_order.append(lambda r0, stage_ref=stage_ref: jnp.concatenate(
            [stage_ref[half, r0:r0 + sub_rows, :] for half in range(LANES // 128)], axis=1))

    def dot(lhs, w):
        return jnp.dot(lhs, w, preferred_element_type=f32)

    chains = [dict(r0=r0) for r0 in range(0, bm, sub_rows)]
    for c in chains:
        r0 = c["r0"]
        c["x"] = x_ref[r0:r0 + sub_rows, :]
        a = _rms(c["x"], gmix_ref[...]).astype(bf16)
        c["gate_na"] = _sigmoid(dot(a, wgate_ref[:, :d]))
        c["gate_dil"] = _sigmoid(dot(a, wgate_ref[:, d:]))
    for c in chains:
        r0 = c["r0"]
        o1, o2, o3, l1, l2, l3 = [get(r0) for get in token_order]
        mx = jnp.maximum(jnp.maximum(l1, l2), l3)
        e1, e2, e3 = jnp.exp2(l1 - mx), jnp.exp2(l2 - mx), jnp.exp2(l3 - mx)
        ydil = (e1 * o1 + e2 * o2 + e3 * o3) * (1.0 / (e1 + e2 + e3))
        yna = jnp.concatenate([yna_ref[g, r0:r0 + sub_rows, :] for g in range(yna_ref.shape[0])], axis=1)
        c["mixed"] = (c.pop("gate_na") * dot(yna, wbna_ref[...])
                      + c.pop("gate_dil") * dot(ydil.astype(bf16), wbdil_ref[...]))
    for c in chains:
        c["h"] = c.pop("x") + dot(c.pop("mixed").astype(bf16), wout_ref[...])
        c["c"] = _rms(c["h"], gmlp_ref[...]).astype(bf16)
        c["acc"] = jnp.zeros_like(c["h"])
    for f in range(d_ff // ff_chunk):
        for c in chains:
            u = dot(c["c"], wup_ref[:, f * ff_chunk:(f + 1) * ff_chunk])
            c["u"] = jnp.square(jnp.maximum(u, 0.0)).astype(bf16)
        for c in chains:
            c["acc"] = c["acc"] + dot(c.pop("u"), wdown_ref[f * ff_chunk:(f + 1) * ff_chunk, :])
    for c in chains:
        r0 = c["r0"]
        c["h"] = c["h"] + c.pop("acc")
        e = _rms(c["h"], gple_ref[...]).astype(bf16)
        c["pg"] = _sigmoid(dot(e, wpg_ref[...]))
        c["pp"] = dot(p_ref[r0:r0 + sub_rows, :].astype(bf16), wpp_ref[...])
    for c in chains:
        r0 = c["r0"]
        h = c["h"] + c["pg"] * c["pp"]
        if final_norm:
            h = _rms(h, gfin_ref[...])
        out_ref[r0:r0 + sub_rows, :] = h


def _post_block(x3, yna, outs, lses, p3, weights, final_norm, bm=512, sub_rows=256, ff_chunk=1024):
    b, s, d = x3.shape

    def rows(arr):
        return pl.BlockSpec((None, bm, arr.shape[-1]), lambda bi, t: (bi, t, 0))

    def residues(arr):
        lead = arr.shape[1]
        rows_per = bm if arr is yna else bm // lead
        return pl.BlockSpec((None, lead, rows_per, arr.shape[-1]), lambda bi, t: (bi, 0, t, 0))

    def whole(arr):
        return pl.BlockSpec(arr.shape, lambda bi, t: (0, 0), pipeline_mode=pl.Buffered(1))

    n_stages = sum(a.shape[1] > 1 for a in (*outs, *lses))
    return pl.pallas_call(
        functools.partial(_post_kernel, final_norm=final_norm, ff_chunk=ff_chunk, sub_rows=sub_rows),
        out_shape=jax.ShapeDtypeStruct((b, s, d), jnp.float32),
        grid=(b, s // bm),
        in_specs=([rows(x3), residues(yna)] + [residues(a) for a in (*outs, *lses)] + [rows(p3)]
                  + [whole(w) for w in weights]),
        out_specs=rows(x3),
        scratch_shapes=[pltpu.VMEM((LANES // 128, bm, 128), jnp.float32)] * n_stages,
        compiler_params=pltpu.CompilerParams(
            dimension_semantics=("arbitrary",) * 2, vmem_limit_bytes=VMEM_LIMIT),
        name="post_block",
    )(x3, yna, *outs, *lses, p3, *weights)


def kernel(x, p, positions, g_mix, w_in, rpb, w_branch_na, w_branch_dil, w_out, g_mlp, w_up, w_down,
           g_ple, w_ple_gate, w_ple_proj, g_final):
    b, s, d = x.shape
    depth = w_in.shape[0]
    bf16 = jnp.bfloat16
    cos_t, sin_t = _rope_tables(positions)
    dil_mask = _dil_mask_tables()
    h = x
    for i in range(depth):
        w_qkv = w_in[i][:, :QKV_WIDTH].astype(bf16)
        w_gate = w_in[i][:, QKV_WIDTH:].astype(bf16)
        nat, *dil_arrays = _qkv_project(h, g_mix[i].reshape(1, d), w_qkv, cos_t, sin_t)
        yna = _na_attention(nat, _na_bias_tables(rpb[i], s // GRID_W))
        outs, lses = [], []
        for g, (window, dil) in enumerate(DIL_GROUPS):
            assert window // (2 * dil) == DIL_RADIUS
            if dil == 1:
                o, lse = _dil_attention(nat, dil_mask, DIL_Q0)
            else:
                o, lse = _dil_attention(dil_arrays[g - 1], dil_mask, 0)
            outs.append(o)
            lses.append(lse)
        weights = [
            g_mix[i].reshape(1, d), w_gate, w_branch_na[i].astype(bf16), w_branch_dil[i].astype(bf16),
            w_out[i].astype(bf16), g_mlp[i].reshape(1, d), w_up[i].astype(bf16), w_down[i].astype(bf16),
            g_ple[i].reshape(1, d), w_ple_gate[i].astype(bf16), w_ple_proj[i].astype(bf16),
            g_final.reshape(1, d),
        ]
        h = _post_block(h, yna, outs, lses, p[i], weights, final_norm=(i == depth - 1))
    return h
```

```python
import functools

import jax
import jax.numpy as jnp
import numpy as np
from jax import lax
from jax.experimental import pallas as pl
from jax.experimental.pallas import tpu as pltpu

HEAD_DIM = 64
GRID_W = 64
NA_HEADS = 8
NA_WIN_ROWS = 8
NA_WIN_COLS = 16
DIL_GROUPS = ((128, 1), (512, 4), (2048, 16))
DIL_HEADS_PER_GROUP = 4
ROPE_THETA = 10000.0
RMS_EPS = 1e-6
NEG_INF = -1e30
LOG2E = 1.4426950408889634

LANES = 256
HEADS_PER_CALL = LANES // HEAD_DIM
NA_WIDTH = NA_HEADS * HEAD_DIM
DIL_WIDTH = DIL_HEADS_PER_GROUP * len(DIL_GROUPS) * HEAD_DIM
QKV_WIDTH = 3 * NA_WIDTH + 3 * DIL_WIDTH
N_QKV_CHUNKS = QKV_WIDTH // LANES
NA_Q0, NA_K0, NA_V0 = 0, 2, 4
DIL_Q0, DIL_K0, DIL_V0 = 6, 9, 12

NA_Q_ROWS = 4
NA_K_ROWS = 12
DIL_Q = 128
DIL_K = 256
DIL_RADIUS = 64

VMEM_LIMIT = 56 * 1024 * 1024


def _rms(x, g):
    ms = jnp.mean(x * x, axis=-1, keepdims=True)
    return x * lax.rsqrt(ms + RMS_EPS) * g


def _sigmoid(x):
    return 1.0 / (1.0 + jnp.exp(-x))


def _first_head_of_tile(rows):
    assert HEAD_DIM * 2 == 128
    return lax.broadcasted_iota(jnp.int32, (rows, 128), 1) < HEAD_DIM


def _stack_head_pair(q_tile):
    first = _first_head_of_tile(q_tile.shape[0])
    zero = jnp.zeros_like(q_tile)
    return jnp.concatenate([jnp.where(first, q_tile, zero), jnp.where(first, zero, q_tile)], axis=0)


def _stack_heads(q):
    zero = jnp.zeros((2 * q.shape[0], 128), q.dtype)
    pairs = [_stack_head_pair(q[:, t * 128:(t + 1) * 128]) for t in range(LANES // 128)]
    return jnp.concatenate([jnp.concatenate([pairs[0], zero], axis=1),
                            jnp.concatenate([zero, pairs[1]], axis=1)], axis=0)


def _unstack_heads(x, m):
    first = _first_head_of_tile(m)
    tiles = []
    for t in range(LANES // 128):
        lanes = slice(t * 128, (t + 1) * 128) if x.shape[1] == LANES else slice(0, 128)
        tiles.append(jnp.where(first, x[2 * t * m:(2 * t + 1) * m, lanes], x[(2 * t + 1) * m:(2 * t + 2) * m, lanes]))
    return jnp.concatenate(tiles, axis=1)


WEIGHT_STAGE_ELEMS = 1 << 18


def _load_weight_bf16(w_hbm, layer, col0, w_vmem):
    n_rows, n_cols = w_vmem.shape
    rc = min(n_rows, 1 << ((WEIGHT_STAGE_ELEMS // n_cols).bit_length() - 1))
    assert n_rows % rc == 0 and rc % 16 == 0
    n_chunks = n_rows // rc

    def body(stage, sems):
        def copy(i):
            return pltpu.make_async_copy(w_hbm.at[layer, pl.ds(i * rc, rc), pl.ds(col0, n_cols)],
                                         stage.at[i % 2], sems.at[i % 2])

        copy(0).start()
        for i in range(n_chunks):
            if i + 1 < n_chunks:
                copy(i + 1).start()
            copy(i).wait()
            w_vmem[i * rc:(i + 1) * rc, :] = stage[i % 2].astype(w_vmem.dtype)

    pl.run_scoped(body, pltpu.VMEM((2, rc, n_cols), jnp.float32), pltpu.SemaphoreType.DMA((2,)))


def _is_first_step():
    return (pl.program_id(0) == 0) & (pl.program_id(1) == 0)


def _qkv_kernel(x_ref, g_ref, win_hbm, pos_ref, freq_ref, nat_ref, *rest, layer):
    dil_refs, stage_ref, w_ref = rest[:-2], rest[-2], rest[-1]
    bm = x_ref.shape[0]

    @pl.when(_is_first_step())
    def _():
        _load_weight_bf16(win_hbm, layer, 0, w_ref)

    a = _rms(x_ref[...], g_ref[...]).astype(jnp.bfloat16)
    ang = pos_ref[...] * freq_ref[...]
    cos_t, sin_t = jnp.cos(ang), jnp.sin(ang)
    cos = jnp.concatenate([cos_t] * 4, axis=0).T
    sin = jnp.concatenate([-sin_t, sin_t] * 2, axis=0).T
    cos = jnp.concatenate([cos] * (LANES // 128), axis=1)
    sin = jnp.concatenate([sin] * (LANES // 128), axis=1)
    first_half = (lax.broadcasted_iota(jnp.int32, cos.shape, 1) % HEAD_DIM) < (HEAD_DIM // 2)
    scale = HEAD_DIM ** -0.5 * LOG2E
    n_groups = len(DIL_GROUPS)
    for c in range(N_QKV_CHUNKS):
        acc = jnp.dot(a, w_ref[:, c * LANES:(c + 1) * LANES], preferred_element_type=jnp.float32)
        if DIL_Q0 <= c < DIL_V0:
            swapped = jnp.where(first_half,
                                pltpu.roll(acc, LANES - HEAD_DIM // 2, axis=1),
                                pltpu.roll(acc, HEAD_DIM // 2, axis=1))
            acc = acc * cos + swapped * sin
        if NA_Q0 <= c < NA_K0 or DIL_Q0 <= c < DIL_K0:
            acc = acc * scale
        group, kind = (c - DIL_Q0) % n_groups, (c - DIL_Q0) // n_groups
        if c < DIL_Q0:
            nat_ref[c, 0] = acc.astype(nat_ref.dtype)
        elif DIL_GROUPS[group][1] == 1:
            nat_ref[DIL_Q0 + kind, 0] = acc.astype(nat_ref.dtype)
        else:
            dil = DIL_GROUPS[group][1]
            o_ref = dil_refs[group - 1]
            for half in range(LANES // 128):
                stage_ref[half] = acc[:, half * 128:(half + 1) * 128]
            for rho in range(dil):
                for half in range(LANES // 128):
                    o_ref[kind, rho, :, half * 128:(half + 1) * 128] = (
                        stage_ref[half, pl.ds(rho, bm // dil, stride=dil), :].astype(o_ref.dtype))


def _qkv_project(x3, g, w_in, layer, positions, bm=1024):
    b, s, d = x3.shape
    assert DIL_GROUPS[0][1] == 1
    half = HEAD_DIM // 2
    tiles = s // bm
    inv_freq = ROPE_THETA ** (-jnp.arange(half, dtype=jnp.float32) / half)
    pos = positions.reshape(1, b * s).astype(jnp.float32)
    out_shape = [jax.ShapeDtypeStruct((b, DIL_Q0 + 3, 1, s, LANES), jnp.bfloat16)]
    out_specs = [pl.BlockSpec((None, DIL_Q0 + 3, 1, bm, LANES), lambda bi, t: (bi, 0, 0, t, 0))]
    for _, dil in DIL_GROUPS[1:]:
        out_shape.append(jax.ShapeDtypeStruct((b, 3, dil, s // dil, LANES), jnp.bfloat16))
        out_specs.append(pl.BlockSpec((None, 3, dil, bm // dil, LANES), lambda bi, t: (bi, 0, 0, t, 0)))
    return pl.pallas_call(
        functools.partial(_qkv_kernel, layer=layer),
        out_shape=out_shape,
        grid=(b, s // bm),
        in_specs=[
            pl.BlockSpec((None, bm, d), lambda bi, t: (bi, t, 0)),
            pl.BlockSpec((1, d), lambda bi, t: (0, 0)),
            pl.BlockSpec(memory_space=pl.ANY),
            pl.BlockSpec((1, bm), lambda bi, t: (0, bi * tiles + t)),
            pl.BlockSpec((half, 1), lambda bi, t: (0, 0)),
        ],
        out_specs=out_specs,
        scratch_shapes=[pltpu.VMEM((LANES // 128, bm, 128), jnp.float32), pltpu.VMEM((d, QKV_WIDTH), jnp.bfloat16)],
        compiler_params=pltpu.CompilerParams(
            dimension_semantics=("arbitrary",) * 2, vmem_limit_bytes=VMEM_LIMIT),
        name="qkv_project",
    )(x3, g, w_in, pos, inv_freq.reshape(half, 1))


def _na_window_start(row0, rows):
    return jnp.clip(row0 - NA_WIN_ROWS // 2, 0, rows - NA_K_ROWS)


def _na_bias_kernel(rpb_ref, out_ref, *, rows):
    n_off_c = 2 * NA_WIN_COLS - 1
    qc = lax.broadcasted_iota(jnp.int32, (GRID_W, 128), 0)
    kc = lax.broadcasted_iota(jnp.int32, (GRID_W, 128), 1)
    cs = jnp.clip(qc - NA_WIN_COLS // 2, 0, GRID_W - NA_WIN_COLS)
    col_valid = (kc >= cs) & (kc < cs + NA_WIN_COLS)
    neg = jnp.full((GRID_W, GRID_W), NEG_INF, jnp.float32)
    for h in range(HEADS_PER_CALL):
        toeplitz = []
        for ro in range(2 * NA_WIN_ROWS - 1):
            row = jnp.broadcast_to(rpb_ref[h, ro:ro + 1, :], (GRID_W, 128))
            t = pltpu.roll(row, 128 - (n_off_c // 2), axis=1, stride=1, stride_axis=0)
            toeplitz.append(jnp.where(col_valid, t * LOG2E, NEG_INF)[:, :GRID_W])
        for variant, r0 in enumerate((0, NA_Q_ROWS, rows - NA_Q_ROWS)):
            w0 = int(np.clip(r0 - NA_WIN_ROWS // 2, 0, rows - NA_K_ROWS))
            for qr in range(NA_Q_ROWS):
                r = r0 + qr
                rs = int(np.clip(r - NA_WIN_ROWS // 2, 0, rows - NA_WIN_ROWS))
                blocks = [toeplitz[w0 + kr - r + NA_WIN_ROWS - 1] if rs <= w0 + kr < rs + NA_WIN_ROWS else neg
                          for kr in range(NA_K_ROWS)]
                row0 = (h * NA_Q_ROWS + qr) * GRID_W
                out_ref[variant, row0:row0 + GRID_W, :] = jnp.concatenate(blocks, axis=1)


def _na_bias_tables(rpb, rows):
    h, n_ro, n_co = rpb.shape
    rpb_pad = jnp.pad(rpb.astype(jnp.float32), ((0, 0), (0, 16 - n_ro), (0, 128 - n_co)))
    n_groups = h // HEADS_PER_CALL
    q, k = HEADS_PER_CALL * NA_Q_ROWS * GRID_W, NA_K_ROWS * GRID_W
    return pl.pallas_call(
        functools.partial(_na_bias_kernel, rows=rows),
        out_shape=jax.ShapeDtypeStruct((3, n_groups, q, k), jnp.float32),
        grid=(n_groups,),
        in_specs=[pl.BlockSpec((HEADS_PER_CALL, 16, 128), lambda g: (g, 0, 0))],
        out_specs=pl.BlockSpec((3, None, q, k), lambda g: (0, g, 0, 0)),
        compiler_params=pltpu.CompilerParams(dimension_semantics=("arbitrary",), vmem_limit_bytes=VMEM_LIMIT),
        name="na_bias_tables",
    )(rpb_pad)


def _attn_kernel(q_ref, k_ref, v_ref, bias_ref, o_ref, *rest, mq, kw, blocks_per_seq, window_start, row_chunk,
                 with_lse, interior_dead_tiles, split_heads, mxu_row_sums):
    f32 = jnp.float32
    if with_lse:
        lse_ref, *scratch = rest
    else:
        scratch = rest
    s_refs, p_refs, st_refs = scratch[0:2], scratch[2:4], scratch[4:6]
    ST_MAX, ST_INV, ST_LSE = 0, 1, 2
    n_blocks = q_ref.shape[0] * blocks_per_seq
    rows_s = HEADS_PER_CALL * mq
    rows_b = bias_ref.shape[1]
    log2_bps = blocks_per_seq.bit_length() - 1
    assert blocks_per_seq == 1 << log2_bps and n_blocks % 2 == 0 and n_blocks >= 2
    assert rows_s % rows_b == 0 and kw % 128 == 0

    def locate(j):
        j = jnp.int32(j)
        seq = lax.shift_right_logical(j, log2_bps)
        n = j & (blocks_per_seq - 1)
        q0 = pl.multiple_of(n * mq, mq)
        w0 = pl.multiple_of(window_start(n), 64)
        variant = jnp.where(n == 0, 0, jnp.where(n == blocks_per_seq - 1, 2, 1))
        return seq, q0, w0, variant

    def lanes_of(x, width):
        return jnp.concatenate([x] * (width // 128), axis=1)

    def stage_a(j, par):
        seq, q0, w0, variant = locate(j)
        bias = jnp.concatenate([bias_ref[variant]] * (rows_s // rows_b), axis=0)
        nt = (((1,), (1,)), ((), ()))
        if split_heads:
            for t in range(LANES // 128):
                lanes = slice(t * 128, (t + 1) * 128)
                qs = _stack_head_pair(q_ref[seq, pl.ds(q0, mq), lanes])
                s = lax.dot_general(qs, k_ref[seq, pl.ds(w0, kw), lanes], nt, preferred_element_type=f32)
                rows = slice(2 * t * mq, (2 * t + 2) * mq)
                s_refs[par][rows, :] = s + bias[rows]
        else:
            qs = _stack_heads(q_ref[seq, pl.ds(q0, mq), :])
            s = lax.dot_general(qs, k_ref[seq, pl.ds(w0, kw), :], nt, preferred_element_type=f32)
            s_refs[par][...] = s + bias
        for r0 in range(0, rows_s, row_chunk):
            mx = jnp.max(s_refs[par][r0:r0 + row_chunk, :], axis=-1, keepdims=True)
            st_refs[par][ST_MAX, r0:r0 + row_chunk, :] = jnp.broadcast_to(mx, (row_chunk, 128))

    def stage_b(par, p_ref, dead_tiles):
        for r0 in range(0, rows_s, row_chunk):
            mx = st_refs[par][ST_MAX, r0:r0 + row_chunk, :]
            live = [t for t in range(kw // 128) if t not in dead_tiles(r0)]
            s = jnp.concatenate([s_refs[par][r0:r0 + row_chunk, t * 128:(t + 1) * 128] for t in live], axis=1)
            p = jnp.exp2(s - lanes_of(mx, s.shape[1]))
            if not mxu_row_sums:
                l = jnp.broadcast_to(jnp.sum(p, axis=-1, keepdims=True), (row_chunk, 128))
                st_refs[par][ST_INV, r0:r0 + row_chunk, :] = 1.0 / l
                if with_lse:
                    st_refs[par][ST_LSE, r0:r0 + row_chunk, :] = mx + jnp.log2(l)
            p = p.astype(p_ref.dtype)
            for i, t in enumerate(live):
                p_ref[r0:r0 + row_chunk, t * 128:(t + 1) * 128] = p[:, i * 128:(i + 1) * 128]

    def stage_c(j, par, p_ref):
        seq, q0, w0, _ = locate(j)
        if mxu_row_sums:
            first = _first_head_of_tile(mq)
            ones = jnp.ones((kw, 128), p_ref.dtype)
            tiles = []
            for t in range(LANES // 128):
                rhs = jnp.concatenate([v_ref[seq, pl.ds(w0, kw), t * 128:(t + 1) * 128], ones], axis=1)
                pv = jnp.dot(p_ref[2 * t * mq:(2 * t + 2) * mq, :], rhs, preferred_element_type=f32)
                den = jnp.where(first, pv[:mq, 128:], pv[mq:, 128:])
                tiles.append(jnp.where(first, pv[:mq, :128], pv[mq:, :128]) * (1.0 / den))
            o_ref[seq, pl.ds(q0, mq), :] = jnp.concatenate(tiles, axis=1).astype(o_ref.dtype)
            return
        if split_heads:
            first = _first_head_of_tile(mq)
            tiles = []
            for t in range(LANES // 128):
                vwin = v_ref[seq, pl.ds(w0, kw), t * 128:(t + 1) * 128]
                pv = jnp.dot(p_ref[2 * t * mq:(2 * t + 2) * mq, :], vwin, preferred_element_type=f32)
                tiles.append(jnp.where(first, pv[:mq], pv[mq:]))
            out = jnp.concatenate(tiles, axis=1)
        else:
            pv = jnp.dot(p_ref[...], v_ref[seq, pl.ds(w0, kw), :], preferred_element_type=f32)
            out = _unstack_heads(pv, mq)
        out = out * _unstack_heads(st_refs[par][ST_INV], mq)
        o_ref[seq, pl.ds(q0, mq), :] = out.astype(o_ref.dtype)
        if with_lse:
            lse_ref[seq, pl.ds(q0, mq), :] = _unstack_heads(st_refs[par][ST_LSE], mq)

    def no_dead_tiles(r0):
        return ()

    def steady_state(first_pair, dead_tiles):
        def body(i, carry):
            for par in (0, 1):
                j = 2 * i + par
                if mxu_row_sums:
                    stage_c(j - 2, par, p_refs[par])
                stage_a(j, par)
                stage_b(1 - par, p_refs[1 - par], dead_tiles)
                if not mxu_row_sums:
                    stage_c(j - 2, par, p_refs[par])
            return carry

        lax.fori_loop(first_pair, n_blocks // 2, body, 0)

    if interior_dead_tiles is None:
        stage_a(0, 0)
        stage_a(1, 1)
        stage_b(0, p_refs[0], no_dead_tiles)
        steady_state(1, no_dead_tiles)
        stage_b(1, p_refs[1], no_dead_tiles)
        stage_c(n_blocks - 2, 0, p_refs[0])
        stage_c(n_blocks - 1, 1, p_refs[1])
    else:
        p_edge = scratch[6]
        assert q_ref.shape[0] == 1 and n_blocks >= 6
        for p_ref in p_refs:
            for r0 in range(0, rows_s, row_chunk):
                for t in interior_dead_tiles(r0):
                    p_ref[r0:r0 + row_chunk, t * 128:(t + 1) * 128] = jnp.zeros((row_chunk, 128), p_ref.dtype)
        stage_a(0, 0)
        stage_a(1, 1)
        stage_b(0, p_edge, no_dead_tiles)
        stage_a(2, 0)
        stage_b(1, p_refs[1], interior_dead_tiles)
        stage_c(0, 0, p_edge)
        stage_a(3, 1)
        stage_b(0, p_refs[0], interior_dead_tiles)
        stage_c(1, 1, p_refs[1])
        steady_state(2, interior_dead_tiles)
        stage_b(1, p_edge, no_dead_tiles)
        stage_c(n_blocks - 2, 0, p_refs[0])
        stage_c(n_blocks - 1, 1, p_edge)


def _windowed_attention(arrays, chunks, bias, bias_spec, grid, batch_of, chunk_of, out_chunks, *, name,
                        mq, kw, window_start, row_chunk, with_lse, split_heads, mxu_row_sums=False,
                        interior_dead_tiles=None):
    b, _, r, sub_len, _ = arrays[0].shape
    blocks_per_seq = sub_len // mq

    def spec(chunk):
        return pl.BlockSpec((None, None, r, sub_len, LANES),
                            lambda *g: (batch_of(*g), chunk + chunk_of(*g), 0, 0, 0))

    out_shape = [jax.ShapeDtypeStruct((b, out_chunks, r, sub_len, LANES), jnp.bfloat16)]
    out_specs = [spec(0)]
    if with_lse:
        out_shape.append(jax.ShapeDtypeStruct((b, out_chunks, r, sub_len, LANES), jnp.float32))
        out_specs.append(spec(0))
    rows_s = HEADS_PER_CALL * mq
    scratch = ([pltpu.VMEM((rows_s, kw), jnp.float32)] * 2 + [pltpu.VMEM((rows_s, kw), jnp.bfloat16)] * 2
               + [pltpu.VMEM((3 if with_lse else 2, rows_s, 128), jnp.float32)] * 2)
    if interior_dead_tiles is not None:
        scratch.append(pltpu.VMEM((rows_s, kw), jnp.bfloat16))
    return pl.pallas_call(
        functools.partial(_attn_kernel, mq=mq, kw=kw, blocks_per_seq=blocks_per_seq, window_start=window_start,
                          row_chunk=row_chunk, with_lse=with_lse, interior_dead_tiles=interior_dead_tiles,
                          split_heads=split_heads, mxu_row_sums=mxu_row_sums),
        out_shape=out_shape,
        grid=grid,
        in_specs=[spec(c) for c in chunks] + [bias_spec],
        out_specs=out_specs,
        scratch_shapes=scratch,
        compiler_params=pltpu.CompilerParams(
            dimension_semantics=("arbitrary",) * len(grid), vmem_limit_bytes=VMEM_LIMIT),
        name=name,
    )(*arrays, bias)


def _na_attention(nat, bias):
    b, _, _, seq, _ = nat.shape
    rows = seq // GRID_W
    n_groups = NA_HEADS // HEADS_PER_CALL
    m_q = NA_Q_ROWS * GRID_W
    for blk in range(1, rows // NA_Q_ROWS - 1):
        r0 = blk * NA_Q_ROWS
        assert 0 <= r0 - NA_WIN_ROWS // 2 <= rows - NA_K_ROWS
        assert r0 + NA_Q_ROWS - 1 - NA_WIN_ROWS // 2 <= rows - NA_WIN_ROWS

    def interior_dead_tiles(row):
        qr = (row % m_q) // GRID_W
        per_tile = 128 // GRID_W
        return tuple(t for t in range(NA_K_ROWS // per_tile)
                     if not any(0 <= kr - qr < NA_WIN_ROWS for kr in range(t * per_tile, (t + 1) * per_tile)))

    (out,) = _windowed_attention(
        (nat, nat, nat), (NA_Q0, NA_K0, NA_V0), bias,
        pl.BlockSpec((3, None) + bias.shape[2:], lambda g, bi: (0, g, 0, 0)),
        grid=(n_groups, b), batch_of=lambda g, bi: bi, chunk_of=lambda g, bi: g, out_chunks=n_groups,
        name="na_attention", mq=m_q, kw=NA_K_ROWS * GRID_W,
        window_start=lambda n: _na_window_start(n * NA_Q_ROWS, rows) * GRID_W, row_chunk=32, with_lse=False,
        split_heads=False, mxu_row_sums=True, interior_dead_tiles=interior_dead_tiles)
    return out.reshape(b, n_groups, seq, LANES)


def _dil_mask_tables():
    i = np.arange(DIL_Q)[:, None]
    j = np.arange(DIL_K)[None, :]
    tabs = []
    for off in (0, DIL_RADIUS, DIL_K - DIL_Q):
        tabs.append(np.where(np.abs(off + i - j) <= DIL_RADIUS, 0.0, NEG_INF))
    return jnp.asarray(np.stack(tabs), dtype=jnp.float32)


def _dil_attention(arr, mask, chunk0):
    b, _, dil, sub_len, _ = arr.shape
    o, lse = _windowed_attention(
        (arr, arr, arr), (chunk0, chunk0 + 1, chunk0 + 2), mask,
        pl.BlockSpec(mask.shape, lambda bi: (0, 0, 0)),
        grid=(b,), batch_of=lambda bi: bi, chunk_of=lambda bi: 0, out_chunks=1,
        name=f"dil_attention_d{dil}", mq=DIL_Q, kw=DIL_K,
        window_start=lambda n: jnp.clip(n * DIL_Q - DIL_RADIUS, 0, sub_len - DIL_K), row_chunk=DIL_Q, with_lse=True,
        split_heads=True)
    return o.reshape(b, dil, sub_len, LANES), lse.reshape(b, dil, sub_len, LANES)


def _post_kernel(x_ref, yna_ref, o1_ref, o2_ref, o3_ref, l1_ref, l2_ref, l3_ref, p_ref,
                 gmix_ref, gmlp_ref, gple_ref, gfin_ref,
                 win_hbm, wbna_hbm, wbdil_hbm, wout_hbm, wup_hbm, wdown_hbm, wpg_hbm, wpp_hbm,
                 out_ref, *scratch, layer, final_norm, ff_chunk, sub_rows):
    f32, bf16 = jnp.float32, jnp.bfloat16
    bm, d = x_ref.shape
    stage_refs = scratch[:-8]
    wgate_ref, wbna_ref, wbdil_ref, wout_ref, wup_ref, wdown_ref, wpg_ref, wpp_ref = scratch[-8:]
    d_ff = wup_ref.shape[1]

    @pl.when(_is_first_step())
    def _():
        _load_weight_bf16(win_hbm, layer, QKV_WIDTH, wgate_ref)
        for w_hbm, w_vmem in ((wbna_hbm, wbna_ref), (wbdil_hbm, wbdil_ref), (wout_hbm, wout_ref),
                              (wup_hbm, wup_ref), (wdown_hbm, wdown_ref), (wpg_hbm, wpg_ref), (wpp_hbm, wpp_ref)):
            _load_weight_bf16(w_hbm, layer, 0, w_vmem)

    stages = iter(stage_refs)
    token_order = []
    for ref in (o1_ref, o2_ref, o3_ref, l1_ref, l2_ref, l3_ref):
        dil = ref.shape[0]
        if dil == 1:
            token_order.append(lambda r0, ref=ref: ref[0, r0:r0 + sub_rows, :].astype(f32))
            continue
        stage_ref = next(stages)
        for rho in range(dil):
            blk = ref[rho].astype(f32)
            for half in range(LANES // 128):
                stage_ref[half, pl.ds(rho, bm // dil, stride=dil), :] = blk[:, half * 128:(half + 1) * 128]
        token_order.append(lambda r0, stage_ref=stage_ref: jnp.concatenate(
            [stage_ref[half, r0:r0 + sub_rows, :] for half in range(LANES // 128)], axis=1))

    def dot(lhs, w):
        return jnp.dot(lhs, w, preferred_element_type=f32)

    chains = [dict(r0=r0) for r0 in range(0, bm, sub_rows)]
    for c in chains:
        r0 = c["r0"]
        c["x"] = x_ref[r0:r0 + sub_rows, :]
        a = _rms(c["x"], gmix_ref[...]).astype(bf16)
        c["gate_na"] = _sigmoid(dot(a, wgate_ref[:, :d]))
        c["gate_dil"] = _sigmoid(dot(a, wgate_ref[:, d:]))
    for c in chains:
        r0 = c["r0"]
        o1, o2, o3, l1, l2, l3 = [get(r0) for get in token_order]
        mx = jnp.maximum(jnp.maximum(l1, l2), l3)
        e1, e2, e3 = jnp.exp2(l1 - mx), jnp.exp2(l2 - mx), jnp.exp2(l3 - mx)
        ydil = (e1 * o1 + e2 * o2 + e3 * o3) * (1.0 / (e1 + e2 + e3))
        yna = jnp.concatenate([yna_ref[g, r0:r0 + sub_rows, :] for g in range(yna_ref.shape[0])], axis=1)
        c["mixed"] = (c.pop("gate_na") * dot(yna, wbna_ref[...])
                      + c.pop("gate_dil") * dot(ydil.astype(bf16), wbdil_ref[...]))
    for c in chains:
        c["h"] = c.pop("x") + dot(c.pop("mixed").astype(bf16), wout_ref[...])
        c["c"] = _rms(c["h"], gmlp_ref[...]).astype(bf16)
        c["acc"] = jnp.zeros_like(c["h"])
    for f in range(d_ff // ff_chunk):
        for c in chains:
            u = dot(c["c"], wup_ref[:, f * ff_chunk:(f + 1) * ff_chunk])
            c["u"] = jnp.square(jnp.maximum(u, 0.0)).astype(bf16)
        for c in chains:
            c["acc"] = c["acc"] + dot(c.pop("u"), wdown_ref[f * ff_chunk:(f + 1) * ff_chunk, :])
    for c in chains:
        r0 = c["r0"]
        c["h"] = c["h"] + c.pop("acc")
        e = _rms(c["h"], gple_ref[...]).astype(bf16)
        c["pg"] = _sigmoid(dot(e, wpg_ref[...]))
        c["pp"] = dot(p_ref[r0:r0 + sub_rows, :].astype(bf16), wpp_ref[...])
    for c in chains:
        r0 = c["r0"]
        h = c["h"] + c["pg"] * c["pp"]
        if final_norm:
            h = _rms(h, gfin_ref[...])
        out_ref[r0:r0 + sub_rows, :] = h


def _post_block(x3, yna, outs, lses, p3, gains, weights, layer, final_norm, bm=512, sub_rows=256, ff_chunk=1024):
    b, s, d = x3.shape
    w_in = weights[0]
    resident = [(d, w_in.shape[2] - QKV_WIDTH)] + [w.shape[1:] for w in weights[1:]]

    def rows(arr):
        return pl.BlockSpec((None, bm, arr.shape[-1]), lambda bi, t: (bi, t, 0))

    def residues(arr):
        lead = arr.shape[1]
        rows_per = bm if arr is yna else bm // lead
        return pl.BlockSpec((None, lead, rows_per, arr.shape[-1]), lambda bi, t: (bi, 0, t, 0))

    def whole(arr):
        return pl.BlockSpec(arr.shape, lambda bi, t: (0, 0))

    n_stages = sum(a.shape[1] > 1 for a in (*outs, *lses))
    return pl.pallas_call(
        functools.partial(_post_kernel, layer=layer, final_norm=final_norm, ff_chunk=ff_chunk, sub_rows=sub_rows),
        out_shape=jax.ShapeDtypeStruct((b, s, d), jnp.float32),
        grid=(b, s // bm),
        in_specs=([rows(x3), residues(yna)] + [residues(a) for a in (*outs, *lses)] + [rows(p3)]
                  + [whole(g) for g in gains] + [pl.BlockSpec(memory_space=pl.ANY)] * len(weights)),
        out_specs=rows(x3),
        scratch_shapes=([pltpu.VMEM((LANES // 128, bm, 128), jnp.float32)] * n_stages
                        + [pltpu.VMEM(shape, jnp.bfloat16) for shape in resident]),
        compiler_params=pltpu.CompilerParams(
            dimension_semantics=("arbitrary",) * 2, vmem_limit_bytes=VMEM_LIMIT),
        name="post_block",
    )(x3, yna, *outs, *lses, p3, *gains, *weights)


def kernel(x, p, positions, g_mix, w_in, rpb, w_branch_na, w_branch_dil, w_out, g_mlp, w_up, w_down,
           g_ple, w_ple_gate, w_ple_proj, g_final):
    b, s, d = x.shape
    depth = w_in.shape[0]
    bf16 = jnp.bfloat16
    dil_mask = _dil_mask_tables()
    h = x
    for i in range(depth):
        nat, *dil_arrays = _qkv_project(h, g_mix[i].reshape(1, d), w_in, i, positions)
        yna = _na_attention(nat, _na_bias_tables(rpb[i], s // GRID_W))
        outs, lses = [], []
        for g, (window, dil) in enumerate(DIL_GROUPS):
            assert window // (2 * dil) == DIL_RADIUS
            if dil == 1:
                o, lse = _dil_attention(nat, dil_mask, DIL_Q0)
            else:
                o, lse = _dil_attention(dil_arrays[g - 1], dil_mask, 0)
            outs.append(o)
            lses.append(lse)
        gains = [g_mix[i].reshape(1, d), g_mlp[i].reshape(1, d), g_ple[i].reshape(1, d), g_final.reshape(1, d)]
        weights = [w_in, w_branch_na, w_branch_dil, w_out, w_up, w_down, w_ple_gate, w_ple_proj]
        h = _post_block(h, yna, outs, lses, p[i], gains, weights, layer=i, final_norm=(i == depth - 1))
    return h
```

```python
import functools

import jax
import jax.numpy as jnp
import numpy as np
from jax import lax
from jax.experimental import pallas as pl
from jax.experimental.pallas import tpu as pltpu

HEAD_DIM = 64
GRID_W = 64
NA_HEADS = 8
NA_WIN_ROWS = 8
NA_WIN_COLS = 16
DIL_GROUPS = ((128, 1), (512, 4), (2048, 16))
DIL_HEADS_PER_GROUP = 4
ROPE_THETA = 10000.0
RMS_EPS = 1e-6
NEG_INF = -1e30
LOG2E = 1.4426950408889634

LANES = 256
HEADS_PER_CALL = LANES // HEAD_DIM
NA_WIDTH = NA_HEADS * HEAD_DIM
DIL_WIDTH = DIL_HEADS_PER_GROUP * len(DIL_GROUPS) * HEAD_DIM
QKV_WIDTH = 3 * NA_WIDTH + 3 * DIL_WIDTH
N_QKV_CHUNKS = QKV_WIDTH // LANES
NA_Q0, NA_K0, NA_V0 = 0, 2, 4
DIL_Q0, DIL_K0, DIL_V0 = 6, 9, 12

NA_Q_ROWS = 4
NA_K_ROWS = 12
DIL_Q = 128
DIL_K = 256
DIL_RADIUS = 64

VMEM_LIMIT = 56 * 1024 * 1024


def _rms(x, g):
    ms = jnp.mean(x * x, axis=-1, keepdims=True)
    return x * lax.rsqrt(ms + RMS_EPS) * g


def _sigmoid(x):
    return 1.0 / (1.0 + jnp.exp(-x))


def _first_head_of_tile(rows):
    assert HEAD_DIM * 2 == 128
    return lax.broadcasted_iota(jnp.int32, (rows, 128), 1) < HEAD_DIM


def _stack_head_pair(q_tile):
    first = _first_head_of_tile(q_tile.shape[0])
    zero = jnp.zeros_like(q_tile)
    return jnp.concatenate([jnp.where(first, q_tile, zero), jnp.where(first, zero, q_tile)], axis=0)


def _stack_heads(q):
    zero = jnp.zeros((2 * q.shape[0], 128), q.dtype)
    pairs = [_stack_head_pair(q[:, t * 128:(t + 1) * 128]) for t in range(LANES // 128)]
    return jnp.concatenate([jnp.concatenate([pairs[0], zero], axis=1),
                            jnp.concatenate([zero, pairs[1]], axis=1)], axis=0)


def _unstack_heads(x, m):
    first = _first_head_of_tile(m)
    tiles = []
    for t in range(LANES // 128):
        lanes = slice(t * 128, (t + 1) * 128) if x.shape[1] == LANES else slice(0, 128)
        tiles.append(jnp.where(first, x[2 * t * m:(2 * t + 1) * m, lanes], x[(2 * t + 1) * m:(2 * t + 2) * m, lanes]))
    return jnp.concatenate(tiles, axis=1)


WEIGHT_STAGE_ELEMS = 1 << 18


WEIGHT_STAGE_SLOTS = 3


def _load_weight_bf16(w_hbm, layer, col0, w_vmem):
    n_rows, n_cols = w_vmem.shape
    rc = min(n_rows, 1 << ((WEIGHT_STAGE_ELEMS // n_cols).bit_length() - 1))
    assert n_rows % rc == 0 and rc % 16 == 0
    n_chunks = n_rows // rc
    slots = WEIGHT_STAGE_SLOTS

    def body(stage, sems):
        def copy(i):
            return pltpu.make_async_copy(w_hbm.at[layer, pl.ds(i * rc, rc), pl.ds(col0, n_cols)],
                                         stage.at[i % slots], sems.at[i % slots])

        for i in range(min(slots - 1, n_chunks)):
            copy(i).start()
        for i in range(n_chunks):
            if i + slots - 1 < n_chunks:
                copy(i + slots - 1).start()
            copy(i).wait()
            w_vmem[i * rc:(i + 1) * rc, :] = stage[i % slots].astype(w_vmem.dtype)

    pl.run_scoped(body, pltpu.VMEM((slots, rc, n_cols), jnp.float32), pltpu.SemaphoreType.DMA((slots,)))


def _is_first_step():
    return (pl.program_id(0) == 0) & (pl.program_id(1) == 0)


def _qkv_kernel(x_ref, g_ref, win_hbm, pos_ref, freq_ref, nat_ref, *rest, layer):
    dil_refs, stage_ref, w_ref = rest[:-2], rest[-2], rest[-1]
    bm = x_ref.shape[0]

    @pl.when(_is_first_step())
    def _():
        _load_weight_bf16(win_hbm, layer, 0, w_ref)

    a = _rms(x_ref[...], g_ref[...]).astype(jnp.bfloat16)
    ang = pos_ref[...] * freq_ref[...]
    cos_t, sin_t = jnp.cos(ang), jnp.sin(ang)
    cos = jnp.concatenate([cos_t] * 4, axis=0).T
    sin = jnp.concatenate([-sin_t, sin_t] * 2, axis=0).T
    cos = jnp.concatenate([cos] * (LANES // 128), axis=1)
    sin = jnp.concatenate([sin] * (LANES // 128), axis=1)
    first_half = (lax.broadcasted_iota(jnp.int32, cos.shape, 1) % HEAD_DIM) < (HEAD_DIM // 2)
    scale = HEAD_DIM ** -0.5 * LOG2E
    n_groups = len(DIL_GROUPS)
    for c in range(N_QKV_CHUNKS):
        acc = jnp.dot(a, w_ref[:, c * LANES:(c + 1) * LANES], preferred_element_type=jnp.float32)
        if DIL_Q0 <= c < DIL_V0:
            swapped = jnp.where(first_half,
                                pltpu.roll(acc, LANES - HEAD_DIM // 2, axis=1),
                                pltpu.roll(acc, HEAD_DIM // 2, axis=1))
            acc = acc * cos + swapped * sin
        if NA_Q0 <= c < NA_K0 or DIL_Q0 <= c < DIL_K0:
            acc = acc * scale
        group, kind = (c - DIL_Q0) % n_groups, (c - DIL_Q0) // n_groups
        if c < DIL_Q0:
            nat_ref[c, 0] = acc.astype(nat_ref.dtype)
        elif DIL_GROUPS[group][1] == 1:
            nat_ref[DIL_Q0 + kind, 0] = acc.astype(nat_ref.dtype)
        else:
            dil = DIL_GROUPS[group][1]
            o_ref = dil_refs[group - 1]
            for half in range(LANES // 128):
                stage_ref[half] = acc[:, half * 128:(half + 1) * 128]
            for rho in range(dil):
                for half in range(LANES // 128):
                    o_ref[kind, rho, :, half * 128:(half + 1) * 128] = (
                        stage_ref[half, pl.ds(rho, bm // dil, stride=dil), :].astype(o_ref.dtype))


def _qkv_project(x3, g, w_in, layer, positions, bm=1024):
    b, s, d = x3.shape
    assert DIL_GROUPS[0][1] == 1
    half = HEAD_DIM // 2
    tiles = s // bm
    inv_freq = ROPE_THETA ** (-jnp.arange(half, dtype=jnp.float32) / half)
    pos = positions.reshape(1, b * s).astype(jnp.float32)
    out_shape = [jax.ShapeDtypeStruct((b, DIL_Q0 + 3, 1, s, LANES), jnp.bfloat16)]
    out_specs = [pl.BlockSpec((None, DIL_Q0 + 3, 1, bm, LANES), lambda bi, t: (bi, 0, 0, t, 0))]
    for _, dil in DIL_GROUPS[1:]:
        out_shape.append(jax.ShapeDtypeStruct((b, 3, dil, s // dil, LANES), jnp.bfloat16))
        out_specs.append(pl.BlockSpec((None, 3, dil, bm // dil, LANES), lambda bi, t: (bi, 0, 0, t, 0)))
    return pl.pallas_call(
        functools.partial(_qkv_kernel, layer=layer),
        out_shape=out_shape,
        grid=(b, s // bm),
        in_specs=[
            pl.BlockSpec((None, bm, d), lambda bi, t: (bi, t, 0)),
            pl.BlockSpec((1, d), lambda bi, t: (0, 0)),
            pl.BlockSpec(memory_space=pl.ANY),
            pl.BlockSpec((1, bm), lambda bi, t: (0, bi * tiles + t)),
            pl.BlockSpec((half, 1), lambda bi, t: (0, 0)),
        ],
        out_specs=out_specs,
        scratch_shapes=[pltpu.VMEM((LANES // 128, bm, 128), jnp.float32), pltpu.VMEM((d, QKV_WIDTH), jnp.bfloat16)],
        compiler_params=pltpu.CompilerParams(
            dimension_semantics=("arbitrary",) * 2, vmem_limit_bytes=VMEM_LIMIT),
        name="qkv_project",
    )(x3, g, w_in, pos, inv_freq.reshape(half, 1))


def _na_window_start(row0, rows):
    return jnp.clip(row0 - NA_WIN_ROWS // 2, 0, rows - NA_K_ROWS)


def _na_bias_kernel(rpb_ref, out_ref, *, rows):
    n_off_c = 2 * NA_WIN_COLS - 1
    qc = lax.broadcasted_iota(jnp.int32, (GRID_W, 128), 0)
    kc = lax.broadcasted_iota(jnp.int32, (GRID_W, 128), 1)
    cs = jnp.clip(qc - NA_WIN_COLS // 2, 0, GRID_W - NA_WIN_COLS)
    col_valid = (kc >= cs) & (kc < cs + NA_WIN_COLS)
    neg = jnp.full((GRID_W, GRID_W), NEG_INF, jnp.float32)
    for h in range(HEADS_PER_CALL):
        toeplitz = []
        for ro in range(2 * NA_WIN_ROWS - 1):
            row = jnp.broadcast_to(rpb_ref[h, ro:ro + 1, :], (GRID_W, 128))
            t = pltpu.roll(row, 128 - (n_off_c // 2), axis=1, stride=1, stride_axis=0)
            toeplitz.append(jnp.where(col_valid, t * LOG2E, NEG_INF)[:, :GRID_W])
        for variant, r0 in enumerate((0, NA_Q_ROWS, rows - NA_Q_ROWS)):
            w0 = int(np.clip(r0 - NA_WIN_ROWS // 2, 0, rows - NA_K_ROWS))
            for qr in range(NA_Q_ROWS):
                r = r0 + qr
                rs = int(np.clip(r - NA_WIN_ROWS // 2, 0, rows - NA_WIN_ROWS))
                blocks = [toeplitz[w0 + kr - r + NA_WIN_ROWS - 1] if rs <= w0 + kr < rs + NA_WIN_ROWS else neg
                          for kr in range(NA_K_ROWS)]
                row0 = (h * NA_Q_ROWS + qr) * GRID_W
                out_ref[variant, row0:row0 + GRID_W, :] = jnp.concatenate(blocks, axis=1)


def _na_bias_tables(rpb, rows):
    h, n_ro, n_co = rpb.shape
    rpb_pad = jnp.pad(rpb.astype(jnp.float32), ((0, 0), (0, 16 - n_ro), (0, 128 - n_co)))
    n_groups = h // HEADS_PER_CALL
    q, k = HEADS_PER_CALL * NA_Q_ROWS * GRID_W, NA_K_ROWS * GRID_W
    return pl.pallas_call(
        functools.partial(_na_bias_kernel, rows=rows),
        out_shape=jax.ShapeDtypeStruct((3, n_groups, q, k), jnp.float32),
        grid=(n_groups,),
        in_specs=[pl.BlockSpec((HEADS_PER_CALL, 16, 128), lambda g: (g, 0, 0))],
        out_specs=pl.BlockSpec((3, None, q, k), lambda g: (0, g, 0, 0)),
        compiler_params=pltpu.CompilerParams(dimension_semantics=("arbitrary",), vmem_limit_bytes=VMEM_LIMIT),
        name="na_bias_tables",
    )(rpb_pad)


def _attn_kernel(q_ref, k_ref, v_ref, bias_ref, o_ref, *rest, mq, kw, blocks_per_seq, window_start, row_chunk,
                 with_lse, interior_dead_tiles, split_heads, mxu_row_sums):
    f32 = jnp.float32
    if with_lse:
        lse_ref, *scratch = rest
    else:
        scratch = rest
    s_refs, p_refs, st_refs = scratch[0:2], scratch[2:4], scratch[4:6]
    ST_MAX, ST_INV, ST_LSE = 0, 1, 2
    n_blocks = q_ref.shape[0] * blocks_per_seq
    rows_s = HEADS_PER_CALL * mq
    rows_b = bias_ref.shape[1]
    log2_bps = blocks_per_seq.bit_length() - 1
    assert blocks_per_seq == 1 << log2_bps and n_blocks % 2 == 0 and n_blocks >= 2
    assert rows_s % rows_b == 0 and kw % 128 == 0

    def locate(j):
        j = jnp.int32(j)
        seq = lax.shift_right_logical(j, log2_bps)
        n = j & (blocks_per_seq - 1)
        q0 = pl.multiple_of(n * mq, mq)
        w0 = pl.multiple_of(window_start(n), 64)
        variant = jnp.where(n == 0, 0, jnp.where(n == blocks_per_seq - 1, 2, 1))
        return seq, q0, w0, variant

    def lanes_of(x, width):
        return jnp.concatenate([x] * (width // 128), axis=1)

    def stage_a(j, par):
        seq, q0, w0, variant = locate(j)
        bias = jnp.concatenate([bias_ref[variant]] * (rows_s // rows_b), axis=0)
        nt = (((1,), (1,)), ((), ()))
        if split_heads:
            for t in range(LANES // 128):
                lanes = slice(t * 128, (t + 1) * 128)
                qs = _stack_head_pair(q_ref[seq, pl.ds(q0, mq), lanes])
                s = lax.dot_general(qs, k_ref[seq, pl.ds(w0, kw), lanes], nt, preferred_element_type=f32)
                rows = slice(2 * t * mq, (2 * t + 2) * mq)
                s_refs[par][rows, :] = s + bias[rows]
        else:
            qs = _stack_heads(q_ref[seq, pl.ds(q0, mq), :])
            s = lax.dot_general(qs, k_ref[seq, pl.ds(w0, kw), :], nt, preferred_element_type=f32)
            s_refs[par][...] = s + bias
        for r0 in range(0, rows_s, row_chunk):
            mx = jnp.max(s_refs[par][r0:r0 + row_chunk, :], axis=-1, keepdims=True)
            st_refs[par][ST_MAX, r0:r0 + row_chunk, :] = jnp.broadcast_to(mx, (row_chunk, 128))

    def stage_b(par, p_ref, dead_tiles):
        for r0 in range(0, rows_s, row_chunk):
            mx = st_refs[par][ST_MAX, r0:r0 + row_chunk, :]
            live = [t for t in range(kw // 128) if t not in dead_tiles(r0)]
            s = jnp.concatenate([s_refs[par][r0:r0 + row_chunk, t * 128:(t + 1) * 128] for t in live], axis=1)
            p = jnp.exp2(s - lanes_of(mx, s.shape[1]))
            if not mxu_row_sums:
                l = jnp.broadcast_to(jnp.sum(p, axis=-1, keepdims=True), (row_chunk, 128))
                st_refs[par][ST_INV, r0:r0 + row_chunk, :] = 1.0 / l
                if with_lse:
                    st_refs[par][ST_LSE, r0:r0 + row_chunk, :] = mx + jnp.log2(l)
            p = p.astype(p_ref.dtype)
            for i, t in enumerate(live):
                p_ref[r0:r0 + row_chunk, t * 128:(t + 1) * 128] = p[:, i * 128:(i + 1) * 128]

    def stage_c(j, par, p_ref):
        seq, q0, w0, _ = locate(j)
        if mxu_row_sums:
            first = _first_head_of_tile(mq)
            ones = jnp.ones((kw, 128), p_ref.dtype)
            tiles = []
            for t in range(LANES // 128):
                rhs = jnp.concatenate([v_ref[seq, pl.ds(w0, kw), t * 128:(t + 1) * 128], ones], axis=1)
                pv = jnp.dot(p_ref[2 * t * mq:(2 * t + 2) * mq, :], rhs, preferred_element_type=f32)
                den = jnp.where(first, pv[:mq, 128:], pv[mq:, 128:])
                tiles.append(jnp.where(first, pv[:mq, :128], pv[mq:, :128]) * (1.0 / den))
            o_ref[seq, pl.ds(q0, mq), :] = jnp.concatenate(tiles, axis=1).astype(o_ref.dtype)
            return
        if split_heads:
            first = _first_head_of_tile(mq)
            tiles = []
            for t in range(LANES // 128):
                vwin = v_ref[seq, pl.ds(w0, kw), t * 128:(t + 1) * 128]
                pv = jnp.dot(p_ref[2 * t * mq:(2 * t + 2) * mq, :], vwin, preferred_element_type=f32)
                tiles.append(jnp.where(first, pv[:mq], pv[mq:]))
            out = jnp.concatenate(tiles, axis=1)
        else:
            pv = jnp.dot(p_ref[...], v_ref[seq, pl.ds(w0, kw), :], preferred_element_type=f32)
            out = _unstack_heads(pv, mq)
        out = out * _unstack_heads(st_refs[par][ST_INV], mq)
        o_ref[seq, pl.ds(q0, mq), :] = out.astype(o_ref.dtype)
        if with_lse:
            lse_ref[seq, pl.ds(q0, mq), :] = _unstack_heads(st_refs[par][ST_LSE], mq)

    def no_dead_tiles(r0):
        return ()

    def steady_state(first_pair, dead_tiles):
        def body(i, carry):
            for par in (0, 1):
                j = 2 * i + par
                if mxu_row_sums:
                    stage_c(j - 2, par, p_refs[par])
                stage_a(j, par)
                stage_b(1 - par, p_refs[1 - par], dead_tiles)
                if not mxu_row_sums:
                    stage_c(j - 2, par, p_refs[par])
            return carry

        lax.fori_loop(first_pair, n_blocks // 2, body, 0)

    if interior_dead_tiles is None:
        stage_a(0, 0)
        stage_a(1, 1)
        stage_b(0, p_refs[0], no_dead_tiles)
        steady_state(1, no_dead_tiles)
        stage_b(1, p_refs[1], no_dead_tiles)
        stage_c(n_blocks - 2, 0, p_refs[0])
        stage_c(n_blocks - 1, 1, p_refs[1])
    else:
        p_edge = scratch[6]
        assert q_ref.shape[0] == 1 and n_blocks >= 6
        for p_ref in p_refs:
            for r0 in range(0, rows_s, row_chunk):
                for t in interior_dead_tiles(r0):
                    p_ref[r0:r0 + row_chunk, t * 128:(t + 1) * 128] = jnp.zeros((row_chunk, 128), p_ref.dtype)
        stage_a(0, 0)
        stage_a(1, 1)
        stage_b(0, p_edge, no_dead_tiles)
        stage_a(2, 0)
        stage_b(1, p_refs[1], interior_dead_tiles)
        stage_c(0, 0, p_edge)
        stage_a(3, 1)
        stage_b(0, p_refs[0], interior_dead_tiles)
        stage_c(1, 1, p_refs[1])
        steady_state(2, interior_dead_tiles)
        stage_b(1, p_edge, no_dead_tiles)
        stage_c(n_blocks - 2, 0, p_refs[0])
        stage_c(n_blocks - 1, 1, p_edge)


def _windowed_attention(arrays, chunks, bias, bias_spec, grid, batch_of, chunk_of, out_chunks, *, name,
                        mq, kw, window_start, row_chunk, with_lse, split_heads, mxu_row_sums=False,
                        interior_dead_tiles=None):
    b, _, r, sub_len, _ = arrays[0].shape
    blocks_per_seq = sub_len // mq

    def spec(chunk):
        return pl.BlockSpec((None, None, r, sub_len, LANES),
                            lambda *g: (batch_of(*g), chunk + chunk_of(*g), 0, 0, 0))

    out_shape = [jax.ShapeDtypeStruct((b, out_chunks, r, sub_len, LANES), jnp.bfloat16)]
    out_specs = [spec(0)]
    if with_lse:
        out_shape.append(jax.ShapeDtypeStruct((b, out_chunks, r, sub_len, LANES), jnp.float32))
        out_specs.append(spec(0))
    rows_s = HEADS_PER_CALL * mq
    scratch = ([pltpu.VMEM((rows_s, kw), jnp.float32)] * 2 + [pltpu.VMEM((rows_s, kw), jnp.bfloat16)] * 2
               + [pltpu.VMEM((3 if with_lse else 2, rows_s, 128), jnp.float32)] * 2)
    if interior_dead_tiles is not None:
        scratch.append(pltpu.VMEM((rows_s, kw), jnp.bfloat16))
    return pl.pallas_call(
        functools.partial(_attn_kernel, mq=mq, kw=kw, blocks_per_seq=blocks_per_seq, window_start=window_start,
                          row_chunk=row_chunk, with_lse=with_lse, interior_dead_tiles=interior_dead_tiles,
                          split_heads=split_heads, mxu_row_sums=mxu_row_sums),
        out_shape=out_shape,
        grid=grid,
        in_specs=[spec(c) for c in chunks] + [bias_spec],
        out_specs=out_specs,
        scratch_shapes=scratch,
        compiler_params=pltpu.CompilerParams(
            dimension_semantics=("arbitrary",) * len(grid), vmem_limit_bytes=VMEM_LIMIT),
        name=name,
    )(*arrays, bias)


def _na_attention(nat, bias):
    b, _, _, seq, _ = nat.shape
    rows = seq // GRID_W
    n_groups = NA_HEADS // HEADS_PER_CALL
    m_q = NA_Q_ROWS * GRID_W
    for blk in range(1, rows // NA_Q_ROWS - 1):
        r0 = blk * NA_Q_ROWS
        assert 0 <= r0 - NA_WIN_ROWS // 2 <= rows - NA_K_ROWS
        assert r0 + NA_Q_ROWS - 1 - NA_WIN_ROWS // 2 <= rows - NA_WIN_ROWS

    def interior_dead_tiles(row):
        qr = (row % m_q) // GRID_W
        per_tile = 128 // GRID_W
        return tuple(t for t in range(NA_K_ROWS // per_tile)
                     if not any(0 <= kr - qr < NA_WIN_ROWS for kr in range(t * per_tile, (t + 1) * per_tile)))

    (out,) = _windowed_attention(
        (nat, nat, nat), (NA_Q0, NA_K0, NA_V0), bias,
        pl.BlockSpec((3, None) + bias.shape[2:], lambda g, bi: (0, g, 0, 0)),
        grid=(n_groups, b), batch_of=lambda g, bi: bi, chunk_of=lambda g, bi: g, out_chunks=n_groups,
        name="na_attention", mq=m_q, kw=NA_K_ROWS * GRID_W,
        window_start=lambda n: _na_window_start(n * NA_Q_ROWS, rows) * GRID_W, row_chunk=32, with_lse=False,
        split_heads=False, mxu_row_sums=True, interior_dead_tiles=interior_dead_tiles)
    return out.reshape(b, n_groups, seq, LANES)


def _dil_mask_tables():
    i = np.arange(DIL_Q)[:, None]
    j = np.arange(DIL_K)[None, :]
    tabs = []
    for off in (0, DIL_RADIUS, DIL_K - DIL_Q):
        tabs.append(np.where(np.abs(off + i - j) <= DIL_RADIUS, 0.0, NEG_INF))
    return jnp.asarray(np.stack(tabs), dtype=jnp.float32)


def _dil_attention(arr, mask, chunk0):
    b, _, dil, sub_len, _ = arr.shape
    o, lse = _windowed_attention(
        (arr, arr, arr), (chunk0, chunk0 + 1, chunk0 + 2), mask,
        pl.BlockSpec(mask.shape, lambda bi: (0, 0, 0)),
        grid=(b,), batch_of=lambda bi: bi, chunk_of=lambda bi: 0, out_chunks=1,
        name=f"dil_attention_d{dil}", mq=DIL_Q, kw=DIL_K,
        window_start=lambda n: jnp.clip(n * DIL_Q - DIL_RADIUS, 0, sub_len - DIL_K), row_chunk=DIL_Q, with_lse=True,
        split_heads=True)
    return o.reshape(b, dil, sub_len, LANES), lse.reshape(b, dil, sub_len, LANES)


def _post_kernel(x_ref, yna_ref, o1_ref, o2_ref, o3_ref, l1_ref, l2_ref, l3_ref, p_ref,
                 gmix_ref, gmlp_ref, gple_ref, gfin_ref,
                 win_hbm, wbna_hbm, wbdil_hbm, wout_hbm, wup_hbm, wdown_hbm, wpg_hbm, wpp_hbm,
                 out_ref, *scratch, layer, final_norm, ff_chunk, sub_rows):
    f32, bf16 = jnp.float32, jnp.bfloat16
    bm, d = x_ref.shape
    stage_refs = scratch[:-8]
    wgate_ref, wbna_ref, wbdil_ref, wout_ref, wup_ref, wdown_ref, wpg_ref, wpp_ref = scratch[-8:]
    d_ff = wup_ref.shape[1]

    @pl.when(_is_first_step())
    def _():
        _load_weight_bf16(win_hbm, layer, QKV_WIDTH, wgate_ref)
        for w_hbm, w_vmem in ((wbna_hbm, wbna_ref), (wbdil_hbm, wbdil_ref), (wout_hbm, wout_ref),
                              (wup_hbm, wup_ref), (wdown_hbm, wdown_ref), (wpg_hbm, wpg_ref), (wpp_hbm, wpp_ref)):
            _load_weight_bf16(w_hbm, layer, 0, w_vmem)

    stages = iter(stage_refs)
    token_order = []
    for ref in (o1_ref, o2_ref, o3_ref, l1_ref, l2_ref, l3_ref):
        dil = ref.shape[0]
        if dil == 1:
            token_order.append(lambda r0, ref=ref: ref[0, r0:r0 + sub_rows, :].astype(f32))
            continue
        stage_ref = next(stages)
        for rho in range(dil):
            blk = ref[rho].astype(f32)
            for half in range(LANES // 128):
                stage_ref[half, pl.ds(rho, bm // dil, stride=dil), :] = blk[:, half * 128:(half + 1) * 128]
        token_order.append(lambda r0, stage_ref=stage_ref: jnp.concatenate(
            [stage_ref[half, r0:r0 + sub_rows, :] for half in range(LANES // 128)], axis=1))

    def dot(lhs, w):
        return jnp.dot(lhs, w, preferred_element_type=f32)

    chains = [dict(r0=r0) for r0 in range(0, bm, sub_rows)]
    for c in chains:
        r0 = c["r0"]
        c["x"] = x_ref[r0:r0 + sub_rows, :]
        a = _rms(c["x"], gmix_ref[...]).astype(bf16)
        c["gate_na"] = _sigmoid(dot(a, wgate_ref[:, :d]))
        c["gate_dil"] = _sigmoid(dot(a, wgate_ref[:, d:]))
    for c in chains:
        r0 = c["r0"]
        o1, o2, o3, l1, l2, l3 = [get(r0) for get in token_order]
        mx = jnp.maximum(jnp.maximum(l1, l2), l3)
        e1, e2, e3 = jnp.exp2(l1 - mx), jnp.exp2(l2 - mx), jnp.exp2(l3 - mx)
        ydil = (e1 * o1 + e2 * o2 + e3 * o3) * (1.0 / (e1 + e2 + e3))
        yna = jnp.concatenate([yna_ref[g, r0:r0 + sub_rows, :] for g in range(yna_ref.shape[0])], axis=1)
        c["mixed"] = (c.pop("gate_na") * dot(yna, wbna_ref[...])
                      + c.pop("gate_dil") * dot(ydil.astype(bf16), wbdil_ref[...]))
    for c in chains:
        c["h"] = c.pop("x") + dot(c.pop("mixed").astype(bf16), wout_ref[...])
        c["c"] = _rms(c["h"], gmlp_ref[...]).astype(bf16)
        c["acc"] = jnp.zeros_like(c["h"])
    for f in range(d_ff // ff_chunk):
        for c in chains:
            u = dot(c["c"], wup_ref[:, f * ff_chunk:(f + 1) * ff_chunk])
            c["u"] = jnp.square(jnp.maximum(u, 0.0)).astype(bf16)
        for c in chains:
            c["acc"] = c["acc"] + dot(c.pop("u"), wdown_ref[f * ff_chunk:(f + 1) * ff_chunk, :])
    for c in chains:
        r0 = c["r0"]
        c["h"] = c["h"] + c.pop("acc")
        e = _rms(c["h"], gple_ref[...]).astype(bf16)
        c["pg"] = _sigmoid(dot(e, wpg_ref[...]))
        c["pp"] = dot(p_ref[r0:r0 + sub_rows, :].astype(bf16), wpp_ref[...])
    for c in chains:
        r0 = c["r0"]
        h = c["h"] + c["pg"] * c["pp"]
        if final_norm:
            h = _rms(h, gfin_ref[...])
        out_ref[r0:r0 + sub_rows, :] = h


def _post_block(x3, yna, outs, lses, p3, gains, weights, layer, final_norm, bm=512, sub_rows=256, ff_chunk=1024):
    b, s, d = x3.shape
    w_in = weights[0]
    resident = [(d, w_in.shape[2] - QKV_WIDTH)] + [w.shape[1:] for w in weights[1:]]

    def rows(arr):
        return pl.BlockSpec((None, bm, arr.shape[-1]), lambda bi, t: (bi, t, 0))

    def residues(arr):
        lead = arr.shape[1]
        rows_per = bm if arr is yna else bm // lead
        return pl.BlockSpec((None, lead, rows_per, arr.shape[-1]), lambda bi, t: (bi, 0, t, 0))

    def whole(arr):
        return pl.BlockSpec(arr.shape, lambda bi, t: (0, 0))

    n_stages = sum(a.shape[1] > 1 for a in (*outs, *lses))
    return pl.pallas_call(
        functools.partial(_post_kernel, layer=layer, final_norm=final_norm, ff_chunk=ff_chunk, sub_rows=sub_rows),
        out_shape=jax.ShapeDtypeStruct((b, s, d), jnp.float32),
        grid=(b, s // bm),
        in_specs=([rows(x3), residues(yna)] + [residues(a) for a in (*outs, *lses)] + [rows(p3)]
                  + [whole(g) for g in gains] + [pl.BlockSpec(memory_space=pl.ANY)] * len(weights)),
        out_specs=rows(x3),
        scratch_shapes=([pltpu.VMEM((LANES // 128, bm, 128), jnp.float32)] * n_stages
                        + [pltpu.VMEM(shape, jnp.bfloat16) for shape in resident]),
        compiler_params=pltpu.CompilerParams(
            dimension_semantics=("arbitrary",) * 2, vmem_limit_bytes=VMEM_LIMIT),
        name="post_block",
    )(x3, yna, *outs, *lses, p3, *gains, *weights)


def kernel(x, p, positions, g_mix, w_in, rpb, w_branch_na, w_branch_dil, w_out, g_mlp, w_up, w_down,
           g_ple, w_ple_gate, w_ple_proj, g_final):
    b, s, d = x.shape
    depth = w_in.shape[0]
    bf16 = jnp.bfloat16
    dil_mask = _dil_mask_tables()
    h = x
    for i in range(depth):
        nat, *dil_arrays = _qkv_project(h, g_mix[i].reshape(1, d), w_in, i, positions)
        yna = _na_attention(nat, _na_bias_tables(rpb[i], s // GRID_W))
        outs, lses = [], []
        for g, (window, dil) in enumerate(DIL_GROUPS):
            assert window // (2 * dil) == DIL_RADIUS
            if dil == 1:
                o, lse = _dil_attention(nat, dil_mask, DIL_Q0)
            else:
                o, lse = _dil_attention(dil_arrays[g - 1], dil_mask, 0)
            outs.append(o)
            lses.append(lse)
        gains = [g_mix[i].reshape(1, d), g_mlp[i].reshape(1, d), g_ple[i].reshape(1, d), g_final.reshape(1, d)]
        weights = [w_in, w_branch_na, w_branch_dil, w_out, w_up, w_down, w_ple_gate, w_ple_proj]
        h = _post_block(h, yna, outs, lses, p[i], gains, weights, layer=i, final_norm=(i == depth - 1))
    return h
```

```python
import functools

import jax
import jax.numpy as jnp
import numpy as np
from jax import lax
from jax.experimental import pallas as pl
from jax.experimental.pallas import tpu as pltpu

HEAD_DIM = 64
GRID_W = 64
NA_HEADS = 8
NA_WIN_ROWS = 8
NA_WIN_COLS = 16
DIL_GROUPS = ((128, 1), (512, 4), (2048, 16))
DIL_HEADS_PER_GROUP = 4
ROPE_THETA = 10000.0
RMS_EPS = 1e-6
NEG_INF = -1e30
LOG2E = 1.4426950408889634

LANES = 256
HEADS_PER_CALL = LANES // HEAD_DIM
NA_WIDTH = NA_HEADS * HEAD_DIM
DIL_WIDTH = DIL_HEADS_PER_GROUP * len(DIL_GROUPS) * HEAD_DIM
QKV_WIDTH = 3 * NA_WIDTH + 3 * DIL_WIDTH
N_QKV_CHUNKS = QKV_WIDTH // LANES
NA_Q0, NA_K0, NA_V0 = 0, 2, 4
DIL_Q0, DIL_K0, DIL_V0 = 6, 9, 12

NA_Q_ROWS = 4
NA_K_ROWS = 12
DIL_Q = 128
DIL_K = 256
DIL_RADIUS = 64

VMEM_LIMIT = 56 * 1024 * 1024


def _rms(x, g):
    ms = jnp.mean(x * x, axis=-1, keepdims=True)
    return x * lax.rsqrt(ms + RMS_EPS) * g


def _sigmoid(x):
    return 1.0 / (1.0 + jnp.exp(-x))


def _first_head_of_tile(rows):
    assert HEAD_DIM * 2 == 128
    return lax.broadcasted_iota(jnp.int32, (rows, 128), 1) < HEAD_DIM


def _stack_head_pair(q_tile):
    first = _first_head_of_tile(q_tile.shape[0])
    zero = jnp.zeros_like(q_tile)
    return jnp.concatenate([jnp.where(first, q_tile, zero), jnp.where(first, zero, q_tile)], axis=0)


def _stack_heads(q):
    zero = jnp.zeros((2 * q.shape[0], 128), q.dtype)
    pairs = [_stack_head_pair(q[:, t * 128:(t + 1) * 128]) for t in range(LANES // 128)]
    return jnp.concatenate([jnp.concatenate([pairs[0], zero], axis=1),
                            jnp.concatenate([zero, pairs[1]], axis=1)], axis=0)


def _unstack_heads(x, m):
    first = _first_head_of_tile(m)
    tiles = []
    for t in range(LANES // 128):
        lanes = slice(t * 128, (t + 1) * 128) if x.shape[1] == LANES else slice(0, 128)
        tiles.append(jnp.where(first, x[2 * t * m:(2 * t + 1) * m, lanes], x[(2 * t + 1) * m:(2 * t + 2) * m, lanes]))
    return jnp.concatenate(tiles, axis=1)


WEIGHT_STAGE_ELEMS = 1 << 18


WEIGHT_STAGE_SLOTS = 3


def _load_weights_bf16(layer, weights):
    n_cols = weights[0][2].shape[1]
    rc = 1 << ((WEIGHT_STAGE_ELEMS // n_cols).bit_length() - 1)
    rc = min([rc] + [w_vmem.shape[0] for _, _, w_vmem in weights])
    assert rc % 16 == 0
    chunks = []
    for w_hbm, col0, w_vmem in weights:
        assert w_vmem.shape[1] == n_cols and w_vmem.shape[0] % rc == 0
        chunks += [(w_hbm, col0, w_vmem, r0) for r0 in range(0, w_vmem.shape[0], rc)]
    slots = WEIGHT_STAGE_SLOTS

    def body(stage, sems):
        def copy(i):
            w_hbm, col0, _, r0 = chunks[i]
            return pltpu.make_async_copy(w_hbm.at[layer, pl.ds(r0, rc), pl.ds(col0, n_cols)],
                                         stage.at[i % slots], sems.at[i % slots])

        for i in range(min(slots - 1, len(chunks))):
            copy(i).start()
        for i, (_, _, w_vmem, r0) in enumerate(chunks):
            if i + slots - 1 < len(chunks):
                copy(i + slots - 1).start()
            copy(i).wait()
            w_vmem[r0:r0 + rc, :] = stage[i % slots].astype(w_vmem.dtype)

    pl.run_scoped(body, pltpu.VMEM((slots, rc, n_cols), jnp.float32), pltpu.SemaphoreType.DMA((slots,)))


def _is_first_step():
    return (pl.program_id(0) == 0) & (pl.program_id(1) == 0)


def _qkv_kernel(x_ref, g_ref, win_hbm, pos_ref, freq_ref, nat_ref, *rest, layer):
    dil_refs, stage_ref, w_ref = rest[:-2], rest[-2], rest[-1]
    bm = x_ref.shape[0]

    @pl.when(_is_first_step())
    def _():
        _load_weights_bf16(layer, [(win_hbm, 0, w_ref)])

    a = _rms(x_ref[...], g_ref[...]).astype(jnp.bfloat16)
    ang = pos_ref[...] * freq_ref[...]
    cos_t, sin_t = jnp.cos(ang), jnp.sin(ang)
    cos = jnp.concatenate([cos_t] * 4, axis=0).T
    sin = jnp.concatenate([-sin_t, sin_t] * 2, axis=0).T
    cos = jnp.concatenate([cos] * (LANES // 128), axis=1)
    sin = jnp.concatenate([sin] * (LANES // 128), axis=1)
    first_half = (lax.broadcasted_iota(jnp.int32, cos.shape, 1) % HEAD_DIM) < (HEAD_DIM // 2)
    scale = HEAD_DIM ** -0.5 * LOG2E
    n_groups = len(DIL_GROUPS)
    for c in range(N_QKV_CHUNKS):
        acc = jnp.dot(a, w_ref[:, c * LANES:(c + 1) * LANES], preferred_element_type=jnp.float32)
        if DIL_Q0 <= c < DIL_V0:
            swapped = jnp.where(first_half,
                                pltpu.roll(acc, LANES - HEAD_DIM // 2, axis=1),
                                pltpu.roll(acc, HEAD_DIM // 2, axis=1))
            acc = acc * cos + swapped * sin
        if NA_Q0 <= c < NA_K0 or DIL_Q0 <= c < DIL_K0:
            acc = acc * scale
        group, kind = (c - DIL_Q0) % n_groups, (c - DIL_Q0) // n_groups
        if c < DIL_Q0:
            nat_ref[c, 0] = acc.astype(nat_ref.dtype)
        elif DIL_GROUPS[group][1] == 1:
            nat_ref[DIL_Q0 + kind, 0] = acc.astype(nat_ref.dtype)
        else:
            dil = DIL_GROUPS[group][1]
            o_ref = dil_refs[group - 1]
            for half in range(LANES // 128):
                stage_ref[half] = acc[:, half * 128:(half + 1) * 128]
            for rho in range(dil):
                for half in range(LANES // 128):
                    o_ref[kind, rho, :, half * 128:(half + 1) * 128] = (
                        stage_ref[half, pl.ds(rho, bm // dil, stride=dil), :].astype(o_ref.dtype))


def _qkv_project(x3, g, w_in, layer, positions, bm=1024):
    b, s, d = x3.shape
    assert DIL_GROUPS[0][1] == 1
    half = HEAD_DIM // 2
    tiles = s // bm
    inv_freq = ROPE_THETA ** (-jnp.arange(half, dtype=jnp.float32) / half)
    pos = positions.reshape(1, b * s).astype(jnp.float32)
    out_shape = [jax.ShapeDtypeStruct((b, DIL_Q0 + 3, 1, s, LANES), jnp.bfloat16)]
    out_specs = [pl.BlockSpec((None, DIL_Q0 + 3, 1, bm, LANES), lambda bi, t: (bi, 0, 0, t, 0))]
    for _, dil in DIL_GROUPS[1:]:
        out_shape.append(jax.ShapeDtypeStruct((b, 3, dil, s // dil, LANES), jnp.bfloat16))
        out_specs.append(pl.BlockSpec((None, 3, dil, bm // dil, LANES), lambda bi, t: (bi, 0, 0, t, 0)))
    return pl.pallas_call(
        functools.partial(_qkv_kernel, layer=layer),
        out_shape=out_shape,
        grid=(b, s // bm),
        in_specs=[
            pl.BlockSpec((None, bm, d), lambda bi, t: (bi, t, 0)),
            pl.BlockSpec((1, d), lambda bi, t: (0, 0)),
            pl.BlockSpec(memory_space=pl.ANY),
            pl.BlockSpec((1, bm), lambda bi, t: (0, bi * tiles + t)),
            pl.BlockSpec((half, 1), lambda bi, t: (0, 0)),
        ],
        out_specs=out_specs,
        scratch_shapes=[pltpu.VMEM((LANES // 128, bm, 128), jnp.float32), pltpu.VMEM((d, QKV_WIDTH), jnp.bfloat16)],
        compiler_params=pltpu.CompilerParams(
            dimension_semantics=("arbitrary",) * 2, vmem_limit_bytes=VMEM_LIMIT),
        name="qkv_project",
    )(x3, g, w_in, pos, inv_freq.reshape(half, 1))


def _na_window_start(row0, rows):
    return jnp.clip(row0 - NA_WIN_ROWS // 2, 0, rows - NA_K_ROWS)


def _na_bias_kernel(rpb_ref, out_ref, *, rows):
    n_off_c = 2 * NA_WIN_COLS - 1
    qc = lax.broadcasted_iota(jnp.int32, (GRID_W, 128), 0)
    kc = lax.broadcasted_iota(jnp.int32, (GRID_W, 128), 1)
    cs = jnp.clip(qc - NA_WIN_COLS // 2, 0, GRID_W - NA_WIN_COLS)
    col_valid = (kc >= cs) & (kc < cs + NA_WIN_COLS)
    neg = jnp.full((GRID_W, GRID_W), NEG_INF, jnp.float32)
    for h in range(HEADS_PER_CALL):
        toeplitz = []
        for ro in range(2 * NA_WIN_ROWS - 1):
            row = jnp.broadcast_to(rpb_ref[h, ro:ro + 1, :], (GRID_W, 128))
            t = pltpu.roll(row, 128 - (n_off_c // 2), axis=1, stride=1, stride_axis=0)
            toeplitz.append(jnp.where(col_valid, t * LOG2E, NEG_INF)[:, :GRID_W])
        for variant, r0 in enumerate((0, NA_Q_ROWS, rows - NA_Q_ROWS)):
            w0 = int(np.clip(r0 - NA_WIN_ROWS // 2, 0, rows - NA_K_ROWS))
            for qr in range(NA_Q_ROWS):
                r = r0 + qr
                rs = int(np.clip(r - NA_WIN_ROWS // 2, 0, rows - NA_WIN_ROWS))
                blocks = [toeplitz[w0 + kr - r + NA_WIN_ROWS - 1] if rs <= w0 + kr < rs + NA_WIN_ROWS else neg
                          for kr in range(NA_K_ROWS)]
                row0 = (h * NA_Q_ROWS + qr) * GRID_W
                out_ref[variant, row0:row0 + GRID_W, :] = jnp.concatenate(blocks, axis=1)


def _na_bias_tables(rpb, rows):
    h, n_ro, n_co = rpb.shape
    rpb_pad = jnp.pad(rpb.astype(jnp.float32), ((0, 0), (0, 16 - n_ro), (0, 128 - n_co)))
    n_groups = h // HEADS_PER_CALL
    q, k = HEADS_PER_CALL * NA_Q_ROWS * GRID_W, NA_K_ROWS * GRID_W
    return pl.pallas_call(
        functools.partial(_na_bias_kernel, rows=rows),
        out_shape=jax.ShapeDtypeStruct((3, n_groups, q, k), jnp.float32),
        grid=(n_groups,),
        in_specs=[pl.BlockSpec((HEADS_PER_CALL, 16, 128), lambda g: (g, 0, 0))],
        out_specs=pl.BlockSpec((3, None, q, k), lambda g: (0, g, 0, 0)),
        compiler_params=pltpu.CompilerParams(dimension_semantics=("arbitrary",), vmem_limit_bytes=VMEM_LIMIT),
        name="na_bias_tables",
    )(rpb_pad)


def _attn_kernel(q_ref, k_ref, v_ref, bias_ref, o_ref, *rest, mq, kw, blocks_per_seq, window_start, row_chunk,
                 with_lse, interior_dead_tiles, split_heads, mxu_row_sums):
    f32 = jnp.float32
    if with_lse:
        lse_ref, *scratch = rest
    else:
        scratch = rest
    s_refs, p_refs, st_refs = scratch[0:2], scratch[2:4], scratch[4:6]
    ST_MAX, ST_SUM, ST_MAX_KEPT = 0, 1, 2
    n_blocks = q_ref.shape[0] * blocks_per_seq
    rows_s = HEADS_PER_CALL * mq
    rows_b = bias_ref.shape[1]
    log2_bps = blocks_per_seq.bit_length() - 1
    assert blocks_per_seq == 1 << log2_bps and n_blocks % 2 == 0 and n_blocks >= 2
    assert rows_s % rows_b == 0 and kw % 128 == 0

    def locate(j):
        j = jnp.int32(j)
        seq = lax.shift_right_logical(j, log2_bps)
        n = j & (blocks_per_seq - 1)
        q0 = pl.multiple_of(n * mq, mq)
        w0 = pl.multiple_of(window_start(n), 64)
        variant = jnp.where(n == 0, 0, jnp.where(n == blocks_per_seq - 1, 2, 1))
        return seq, q0, w0, variant

    def lanes_of(x, width):
        return jnp.concatenate([x] * (width // 128), axis=1)

    def stage_a(j, par):
        seq, q0, w0, variant = locate(j)
        bias = jnp.concatenate([bias_ref[variant]] * (rows_s // rows_b), axis=0)
        nt = (((1,), (1,)), ((), ()))
        if split_heads:
            for t in range(LANES // 128):
                lanes = slice(t * 128, (t + 1) * 128)
                qs = _stack_head_pair(q_ref[seq, pl.ds(q0, mq), lanes])
                s = lax.dot_general(qs, k_ref[seq, pl.ds(w0, kw), lanes], nt, preferred_element_type=f32)
                rows = slice(2 * t * mq, (2 * t + 2) * mq)
                s_refs[par][rows, :] = s + bias[rows]
        else:
            qs = _stack_heads(q_ref[seq, pl.ds(q0, mq), :])
            s = lax.dot_general(qs, k_ref[seq, pl.ds(w0, kw), :], nt, preferred_element_type=f32)
            s_refs[par][...] = s + bias
        for r0 in range(0, rows_s, row_chunk):
            mx = jnp.max(s_refs[par][r0:r0 + row_chunk, :], axis=-1, keepdims=True)
            st_refs[par][ST_MAX, r0:r0 + row_chunk, :] = jnp.broadcast_to(mx, (row_chunk, 128))

    def stage_b(par, p_ref, dead_tiles):
        for r0 in range(0, rows_s, row_chunk):
            mx = st_refs[par][ST_MAX, r0:r0 + row_chunk, :]
            live = [t for t in range(kw // 128) if t not in dead_tiles(r0)]
            s = jnp.concatenate([s_refs[par][r0:r0 + row_chunk, t * 128:(t + 1) * 128] for t in live], axis=1)
            p = jnp.exp2(s - lanes_of(mx, s.shape[1]))
            if not mxu_row_sums:
                l = jnp.sum(p, axis=-1, keepdims=True)
                st_refs[par][ST_SUM, r0:r0 + row_chunk, :] = jnp.broadcast_to(l, (row_chunk, 128))
                if with_lse:
                    st_refs[par][ST_MAX_KEPT, r0:r0 + row_chunk, :] = mx
            p = p.astype(p_ref.dtype)
            for i, t in enumerate(live):
                p_ref[r0:r0 + row_chunk, t * 128:(t + 1) * 128] = p[:, i * 128:(i + 1) * 128]

    def stage_c(j, par, p_ref):
        seq, q0, w0, _ = locate(j)
        if mxu_row_sums:
            first = _first_head_of_tile(mq)
            ones = jnp.ones((kw, 128), p_ref.dtype)
            tiles = []
            for t in range(LANES // 128):
                rhs = jnp.concatenate([v_ref[seq, pl.ds(w0, kw), t * 128:(t + 1) * 128], ones], axis=1)
                pv = jnp.dot(p_ref[2 * t * mq:(2 * t + 2) * mq, :], rhs, preferred_element_type=f32)
                den = jnp.where(first, pv[:mq, 128:], pv[mq:, 128:])
                tiles.append(jnp.where(first, pv[:mq, :128], pv[mq:, :128]) * (1.0 / den))
            o_ref[seq, pl.ds(q0, mq), :] = jnp.concatenate(tiles, axis=1).astype(o_ref.dtype)
            return
        if split_heads:
            first = _first_head_of_tile(mq)
            tiles = []
            for t in range(LANES // 128):
                vwin = v_ref[seq, pl.ds(w0, kw), t * 128:(t + 1) * 128]
                pv = jnp.dot(p_ref[2 * t * mq:(2 * t + 2) * mq, :], vwin, preferred_element_type=f32)
                tiles.append(jnp.where(first, pv[:mq], pv[mq:]))
            out = jnp.concatenate(tiles, axis=1)
        else:
            pv = jnp.dot(p_ref[...], v_ref[seq, pl.ds(w0, kw), :], preferred_element_type=f32)
            out = _unstack_heads(pv, mq)
        l = _unstack_heads(st_refs[par][ST_SUM], mq)
        o_ref[seq, pl.ds(q0, mq), :] = (out * (1.0 / l)).astype(o_ref.dtype)
        if with_lse:
            lse_ref[seq, pl.ds(q0, mq), :] = _unstack_heads(st_refs[par][ST_MAX_KEPT], mq) + jnp.log2(l)

    def no_dead_tiles(r0):
        return ()

    def steady_state(first_pair, dead_tiles):
        def body(i, carry):
            for par in (0, 1):
                j = 2 * i + par
                if mxu_row_sums:
                    stage_c(j - 2, par, p_refs[par])
                stage_a(j, par)
                stage_b(1 - par, p_refs[1 - par], dead_tiles)
                if not mxu_row_sums:
                    stage_c(j - 2, par, p_refs[par])
            return carry

        lax.fori_loop(first_pair, n_blocks // 2, body, 0)

    if interior_dead_tiles is None:
        stage_a(0, 0)
        stage_a(1, 1)
        stage_b(0, p_refs[0], no_dead_tiles)
        steady_state(1, no_dead_tiles)
        stage_b(1, p_refs[1], no_dead_tiles)
        stage_c(n_blocks - 2, 0, p_refs[0])
        stage_c(n_blocks - 1, 1, p_refs[1])
    else:
        p_edge = scratch[6]
        assert q_ref.shape[0] == 1 and n_blocks >= 6
        for p_ref in p_refs:
            for r0 in range(0, rows_s, row_chunk):
                for t in interior_dead_tiles(r0):
                    p_ref[r0:r0 + row_chunk, t * 128:(t + 1) * 128] = jnp.zeros((row_chunk, 128), p_ref.dtype)
        stage_a(0, 0)
        stage_a(1, 1)
        stage_b(0, p_edge, no_dead_tiles)
        stage_a(2, 0)
        stage_b(1, p_refs[1], interior_dead_tiles)
        stage_c(0, 0, p_edge)
        stage_a(3, 1)
        stage_b(0, p_refs[0], interior_dead_tiles)
        stage_c(1, 1, p_refs[1])
        steady_state(2, interior_dead_tiles)
        stage_b(1, p_edge, no_dead_tiles)
        stage_c(n_blocks - 2, 0, p_refs[0])
        stage_c(n_blocks - 1, 1, p_edge)


def _windowed_attention(arrays, chunks, bias, bias_spec, grid, batch_of, chunk_of, out_chunks, *, name,
                        mq, kw, window_start, row_chunk, with_lse, split_heads, mxu_row_sums=False,
                        interior_dead_tiles=None):
    b, _, r, sub_len, _ = arrays[0].shape
    blocks_per_seq = sub_len // mq

    def spec(chunk):
        return pl.BlockSpec((None, None, r, sub_len, LANES),
                            lambda *g: (batch_of(*g), chunk + chunk_of(*g), 0, 0, 0))

    out_shape = [jax.ShapeDtypeStruct((b, out_chunks, r, sub_len, LANES), jnp.bfloat16)]
    out_specs = [spec(0)]
    if with_lse:
        out_shape.append(jax.ShapeDtypeStruct((b, out_chunks, r, sub_len, LANES), jnp.float32))
        out_specs.append(spec(0))
    rows_s = HEADS_PER_CALL * mq
    scratch = ([pltpu.VMEM((rows_s, kw), jnp.float32)] * 2 + [pltpu.VMEM((rows_s, kw), jnp.bfloat16)] * 2
               + [pltpu.VMEM((3 if with_lse else 2, rows_s, 128), jnp.float32)] * 2)
    if interior_dead_tiles is not None:
        scratch.append(pltpu.VMEM((rows_s, kw), jnp.bfloat16))
    return pl.pallas_call(
        functools.partial(_attn_kernel, mq=mq, kw=kw, blocks_per_seq=blocks_per_seq, window_start=window_start,
                          row_chunk=row_chunk, with_lse=with_lse, interior_dead_tiles=interior_dead_tiles,
                          split_heads=split_heads, mxu_row_sums=mxu_row_sums),
        out_shape=out_shape,
        grid=grid,
        in_specs=[spec(c) for c in chunks] + [bias_spec],
        out_specs=out_specs,
        scratch_shapes=scratch,
        compiler_params=pltpu.CompilerParams(
            dimension_semantics=("arbitrary",) * len(grid), vmem_limit_bytes=VMEM_LIMIT),
        name=name,
    )(*arrays, bias)


def _na_attention(nat, bias):
    b, _, _, seq, _ = nat.shape
    rows = seq // GRID_W
    n_groups = NA_HEADS // HEADS_PER_CALL
    m_q = NA_Q_ROWS * GRID_W
    for blk in range(1, rows // NA_Q_ROWS - 1):
        r0 = blk * NA_Q_ROWS
        assert 0 <= r0 - NA_WIN_ROWS // 2 <= rows - NA_K_ROWS
        assert r0 + NA_Q_ROWS - 1 - NA_WIN_ROWS // 2 <= rows - NA_WIN_ROWS

    def interior_dead_tiles(row):
        qr = (row % m_q) // GRID_W
        per_tile = 128 // GRID_W
        return tuple(t for t in range(NA_K_ROWS // per_tile)
                     if not any(0 <= kr - qr < NA_WIN_ROWS for kr in range(t * per_tile, (t + 1) * per_tile)))

    (out,) = _windowed_attention(
        (nat, nat, nat), (NA_Q0, NA_K0, NA_V0), bias,
        pl.BlockSpec((3, None) + bias.shape[2:], lambda g, bi: (0, g, 0, 0)),
        grid=(n_groups, b), batch_of=lambda g, bi: bi, chunk_of=lambda g, bi: g, out_chunks=n_groups,
        name="na_attention", mq=m_q, kw=NA_K_ROWS * GRID_W,
        window_start=lambda n: _na_window_start(n * NA_Q_ROWS, rows) * GRID_W, row_chunk=32, with_lse=False,
        split_heads=False, mxu_row_sums=True, interior_dead_tiles=interior_dead_tiles)
    return out.reshape(b, n_groups, seq, LANES)


def _dil_mask_tables():
    i = np.arange(DIL_Q)[:, None]
    j = np.arange(DIL_K)[None, :]
    tabs = []
    for off in (0, DIL_RADIUS, DIL_K - DIL_Q):
        tabs.append(np.where(np.abs(off + i - j) <= DIL_RADIUS, 0.0, NEG_INF))
    return jnp.asarray(np.stack(tabs), dtype=jnp.float32)


def _dil_attention(arr, mask, chunk0):
    b, _, dil, sub_len, _ = arr.shape
    o, lse = _windowed_attention(
        (arr, arr, arr), (chunk0, chunk0 + 1, chunk0 + 2), mask,
        pl.BlockSpec(mask.shape, lambda bi: (0, 0, 0)),
        grid=(b,), batch_of=lambda bi: bi, chunk_of=lambda bi: 0, out_chunks=1,
        name=f"dil_attention_d{dil}", mq=DIL_Q, kw=DIL_K,
        window_start=lambda n: jnp.clip(n * DIL_Q - DIL_RADIUS, 0, sub_len - DIL_K), row_chunk=DIL_Q, with_lse=True,
        split_heads=True)
    return o.reshape(b, dil, sub_len, LANES), lse.reshape(b, dil, sub_len, LANES)


def _post_kernel(x_ref, yna_ref, o1_ref, o2_ref, o3_ref, l1_ref, l2_ref, l3_ref, p_ref,
                 gmix_ref, gmlp_ref, gple_ref, gfin_ref,
                 win_hbm, wbna_hbm, wbdil_hbm, wout_hbm, wup_hbm, wdown_hbm, wpg_hbm, wpp_hbm,
                 out_ref, *scratch, layer, final_norm, ff_chunk, sub_rows):
    f32, bf16 = jnp.float32, jnp.bfloat16
    bm, d = x_ref.shape
    stage_refs = scratch[:-8]
    wgate_ref, wbna_ref, wbdil_ref, wout_ref, wup_ref, wdown_ref, wpg_ref, wpp_ref = scratch[-8:]
    d_ff = wup_ref.shape[1]

    @pl.when(_is_first_step())
    def _():
        _load_weights_bf16(layer, [(win_hbm, QKV_WIDTH, wgate_ref)])
        _load_weights_bf16(layer, [(wbna_hbm, 0, wbna_ref), (wbdil_hbm, 0, wbdil_ref), (wout_hbm, 0, wout_ref),
                                   (wdown_hbm, 0, wdown_ref), (wpg_hbm, 0, wpg_ref), (wpp_hbm, 0, wpp_ref)])
        _load_weights_bf16(layer, [(wup_hbm, 0, wup_ref)])

    stages = iter(stage_refs)
    token_order = []
    for ref in (o1_ref, o2_ref, o3_ref, l1_ref, l2_ref, l3_ref):
        dil = ref.shape[0]
        if dil == 1:
            token_order.append(lambda r0, ref=ref: ref[0, r0:r0 + sub_rows, :].astype(f32))
            continue
        stage_ref = next(stages)
        for rho in range(dil):
            blk = ref[rho].astype(f32)
            for half in range(LANES // 128):
                stage_ref[half, pl.ds(rho, bm // dil, stride=dil), :] = blk[:, half * 128:(half + 1) * 128]
        token_order.append(lambda r0, stage_ref=stage_ref: jnp.concatenate(
            [stage_ref[half, r0:r0 + sub_rows, :] for half in range(LANES // 128)], axis=1))

    def dot(lhs, w):
        return jnp.dot(lhs, w, preferred_element_type=f32)

    chains = [dict(r0=r0) for r0 in range(0, bm, sub_rows)]
    for c in chains:
        r0 = c["r0"]
        c["x"] = x_ref[r0:r0 + sub_rows, :]
        a = _rms(c["x"], gmix_ref[...]).astype(bf16)
        c["gate_na"] = _sigmoid(dot(a, wgate_ref[:, :d]))
        c["gate_dil"] = _sigmoid(dot(a, wgate_ref[:, d:]))
    for c in chains:
        r0 = c["r0"]
        o1, o2, o3, l1, l2, l3 = [get(r0) for get in token_order]
        mx = jnp.maximum(jnp.maximum(l1, l2), l3)
        e1, e2, e3 = jnp.exp2(l1 - mx), jnp.exp2(l2 - mx), jnp.exp2(l3 - mx)
        ydil = (e1 * o1 + e2 * o2 + e3 * o3) * (1.0 / (e1 + e2 + e3))
        yna = jnp.concatenate([yna_ref[g, r0:r0 + sub_rows, :] for g in range(yna_ref.shape[0])], axis=1)
        c["mixed"] = (c.pop("gate_na") * dot(yna, wbna_ref[...])
                      + c.pop("gate_dil") * dot(ydil.astype(bf16), wbdil_ref[...]))
    for c in chains:
        c["h"] = c.pop("x") + dot(c.pop("mixed").astype(bf16), wout_ref[...])
        c["c"] = _rms(c["h"], gmlp_ref[...]).astype(bf16)
        c["acc"] = jnp.zeros_like(c["h"])
    for f in range(d_ff // ff_chunk):
        for c in chains:
            u = dot(c["c"], wup_ref[:, f * ff_chunk:(f + 1) * ff_chunk])
            c["u"] = jnp.square(jnp.maximum(u, 0.0)).astype(bf16)
        for c in chains:
            c["acc"] = c["acc"] + dot(c.pop("u"), wdown_ref[f * ff_chunk:(f + 1) * ff_chunk, :])
    for c in chains:
        r0 = c["r0"]
        c["h"] = c["h"] + c.pop("acc")
        e = _rms(c["h"], gple_ref[...]).astype(bf16)
        c["pg"] = _sigmoid(dot(e, wpg_ref[...]))
        c["pp"] = dot(p_ref[r0:r0 + sub_rows, :].astype(bf16), wpp_ref[...])
    for c in chains:
        r0 = c["r0"]
        h = c["h"] + c["pg"] * c["pp"]
        if final_norm:
            h = _rms(h, gfin_ref[...])
        out_ref[r0:r0 + sub_rows, :] = h


def _post_block(x3, yna, outs, lses, p3, gains, weights, layer, final_norm, bm=512, sub_rows=256, ff_chunk=1024):
    b, s, d = x3.shape
    w_in = weights[0]
    resident = [(d, w_in.shape[2] - QKV_WIDTH)] + [w.shape[1:] for w in weights[1:]]

    def rows(arr):
        return pl.BlockSpec((None, bm, arr.shape[-1]), lambda bi, t: (bi, t, 0))

    def residues(arr):
        lead = arr.shape[1]
        rows_per = bm if arr is yna else bm // lead
        return pl.BlockSpec((None, lead, rows_per, arr.shape[-1]), lambda bi, t: (bi, 0, t, 0))

    def whole(arr):
        return pl.BlockSpec(arr.shape, lambda bi, t: (0, 0))

    n_stages = sum(a.shape[1] > 1 for a in (*outs, *lses))
    return pl.pallas_call(
        functools.partial(_post_kernel, layer=layer, final_norm=final_norm, ff_chunk=ff_chunk, sub_rows=sub_rows),
        out_shape=jax.ShapeDtypeStruct((b, s, d), jnp.float32),
        grid=(b, s // bm),
        in_specs=([rows(x3), residues(yna)] + [residues(a) for a in (*outs, *lses)] + [rows(p3)]
                  + [whole(g) for g in gains] + [pl.BlockSpec(memory_space=pl.ANY)] * len(weights)),
        out_specs=rows(x3),
        scratch_shapes=([pltpu.VMEM((LANES // 128, bm, 128), jnp.float32)] * n_stages
                        + [pltpu.VMEM(shape, jnp.bfloat16) for shape in resident]),
        compiler_params=pltpu.CompilerParams(
            dimension_semantics=("arbitrary",) * 2, vmem_limit_bytes=VMEM_LIMIT),
        name="post_block",
    )(x3, yna, *outs, *lses, p3, *gains, *weights)


def kernel(x, p, positions, g_mix, w_in, rpb, w_branch_na, w_branch_dil, w_out, g_mlp, w_up, w_down,
           g_ple, w_ple_gate, w_ple_proj, g_final):
    b, s, d = x.shape
    depth = w_in.shape[0]
    bf16 = jnp.bfloat16
    dil_mask = _dil_mask_tables()
    h = x
    for i in range(depth):
        nat, *dil_arrays = _qkv_project(h, g_mix[i].reshape(1, d), w_in, i, positions)
        yna = _na_attention(nat, _na_bias_tables(rpb[i], s // GRID_W))
        outs, lses = [], []
        for g, (window, dil) in enumerate(DIL_GROUPS):
            assert window // (2 * dil) == DIL_RADIUS
            if dil == 1:
                o, lse = _dil_attention(nat, dil_mask, DIL_Q0)
            else:
                o, lse = _dil_attention(dil_arrays[g - 1], dil_mask, 0)
            outs.append(o)
            lses.append(lse)
        gains = [g_mix[i].reshape(1, d), g_mlp[i].reshape(1, d), g_ple[i].reshape(1, d), g_final.reshape(1, d)]
        weights = [w_in, w_branch_na, w_branch_dil, w_out, w_up, w_down, w_ple_gate, w_ple_proj]
        h = _post_block(h, yna, outs, lses, p[i], gains, weights, layer=i, final_norm=(i == depth - 1))
    return h
```

```python
import functools

import jax
import jax.numpy as jnp
import numpy as np
from jax import lax
from jax.experimental import pallas as pl
from jax.experimental.pallas import tpu as pltpu

HEAD_DIM = 64
GRID_W = 64
NA_HEADS = 8
NA_WIN_ROWS = 8
NA_WIN_COLS = 16
DIL_GROUPS = ((128, 1), (512, 4), (2048, 16))
DIL_HEADS_PER_GROUP = 4
ROPE_THETA = 10000.0
RMS_EPS = 1e-6
NEG_INF = -1e30
LOG2E = 1.4426950408889634

TILE = 128
LANES = 256
HEADS_PER_CALL = LANES // HEAD_DIM
NA_WIDTH = NA_HEADS * HEAD_DIM
DIL_WIDTH = DIL_HEADS_PER_GROUP * len(DIL_GROUPS) * HEAD_DIM
QKV_WIDTH = 3 * NA_WIDTH + 3 * DIL_WIDTH
N_QKV_CHUNKS = QKV_WIDTH // LANES
NA_Q0, NA_K0, NA_V0 = 0, 2, 4
DIL_Q0, DIL_K0, DIL_V0 = 6, 9, 12

NA_Q_ROWS = 4
NA_K_ROWS = 12
DIL_Q = 128
DIL_K = 256
DIL_RADIUS = 64

VMEM_LIMIT = 56 * 1024 * 1024


def _rms(x, g):
    ms = jnp.mean(x * x, axis=-1, keepdims=True)
    return x * lax.rsqrt(ms + RMS_EPS) * g


def _sigmoid(x):
    return 1.0 / (1.0 + jnp.exp(-x))


def _first_head_of_tile(rows):
    assert HEAD_DIM * 2 == TILE and LANES == 2 * TILE
    return lax.broadcasted_iota(jnp.int32, (rows, TILE), 1) < HEAD_DIM


def _tile(t):
    return slice(t * TILE, (t + 1) * TILE)


def _stack_head_pair(q_tile):
    first = _first_head_of_tile(q_tile.shape[0])
    zero = jnp.zeros_like(q_tile)
    return jnp.concatenate([jnp.where(first, q_tile, zero), jnp.where(first, zero, q_tile)], axis=0)


def _stack_heads(q):
    zero = jnp.zeros((2 * q.shape[0], TILE), q.dtype)
    pairs = [_stack_head_pair(q[:, _tile(t)]) for t in range(LANES // TILE)]
    return jnp.concatenate([jnp.concatenate([pairs[0], zero], axis=1),
                            jnp.concatenate([zero, pairs[1]], axis=1)], axis=0)


def _unstack_heads(x, m):
    first = _first_head_of_tile(m)
    tiles = []
    for t in range(LANES // TILE):
        lanes = _tile(t) if x.shape[1] == LANES else _tile(0)
        tiles.append(jnp.where(first, x[2 * t * m:(2 * t + 1) * m, lanes], x[(2 * t + 1) * m:(2 * t + 2) * m, lanes]))
    return jnp.concatenate(tiles, axis=1)


WEIGHT_STAGE_ELEMS = 1 << 18
WEIGHT_STAGE_SLOTS = 3


def _load_weights_bf16(layer, weights):
    n_cols = weights[0][2].shape[1]
    rc = 1 << ((WEIGHT_STAGE_ELEMS // n_cols).bit_length() - 1)
    rc = min([rc] + [w_vmem.shape[0] for _, _, w_vmem in weights])
    assert rc % 16 == 0
    chunks = []
    for w_hbm, col0, w_vmem in weights:
        assert w_vmem.shape[1] == n_cols and w_vmem.shape[0] % rc == 0
        chunks += [(w_hbm, col0, w_vmem, r0) for r0 in range(0, w_vmem.shape[0], rc)]
    slots = WEIGHT_STAGE_SLOTS

    def body(stage, sems):
        def copy(i):
            w_hbm, col0, _, r0 = chunks[i]
            return pltpu.make_async_copy(w_hbm.at[layer, pl.ds(r0, rc), pl.ds(col0, n_cols)],
                                         stage.at[i % slots], sems.at[i % slots])

        for i in range(min(slots - 1, len(chunks))):
            copy(i).start()
        for i, (_, _, w_vmem, r0) in enumerate(chunks):
            if i + slots - 1 < len(chunks):
                copy(i + slots - 1).start()
            copy(i).wait()
            w_vmem[r0:r0 + rc, :] = stage[i % slots].astype(w_vmem.dtype)

    pl.run_scoped(body, pltpu.VMEM((slots, rc, n_cols), jnp.float32), pltpu.SemaphoreType.DMA((slots,)))


def _is_first_step():
    return (pl.program_id(0) == 0) & (pl.program_id(1) == 0)


def _qkv_kernel(x_ref, g_ref, win_hbm, pos_ref, freq_ref, nat_ref, *rest, layer):
    dil_refs, stage_ref, w_ref = rest[:-2], rest[-2], rest[-1]
    bm = x_ref.shape[0]

    @pl.when(_is_first_step())
    def _():
        _load_weights_bf16(layer, [(win_hbm, 0, w_ref)])

    a = _rms(x_ref[...], g_ref[...]).astype(jnp.bfloat16)
    ang = pos_ref[...] * freq_ref[...]
    cos_t, sin_t = jnp.cos(ang), jnp.sin(ang)
    reps = TILE // cos_t.shape[0]
    cos = jnp.concatenate([cos_t] * reps, axis=0).T
    sin = jnp.concatenate([-sin_t, sin_t] * (reps // 2), axis=0).T
    cos = jnp.concatenate([cos] * (LANES // TILE), axis=1)
    sin = jnp.concatenate([sin] * (LANES // TILE), axis=1)
    first_half = (lax.broadcasted_iota(jnp.int32, cos.shape, 1) % HEAD_DIM) < (HEAD_DIM // 2)
    scale = HEAD_DIM ** -0.5 * LOG2E
    n_groups = len(DIL_GROUPS)
    for c in range(N_QKV_CHUNKS):
        acc = jnp.dot(a, w_ref[:, c * LANES:(c + 1) * LANES], preferred_element_type=jnp.float32)
        if DIL_Q0 <= c < DIL_V0:
            swapped = jnp.where(first_half,
                                pltpu.roll(acc, LANES - HEAD_DIM // 2, axis=1),
                                pltpu.roll(acc, HEAD_DIM // 2, axis=1))
            acc = acc * cos + swapped * sin
        if NA_Q0 <= c < NA_K0 or DIL_Q0 <= c < DIL_K0:
            acc = acc * scale
        group, kind = (c - DIL_Q0) % n_groups, (c - DIL_Q0) // n_groups
        if c < DIL_Q0:
            nat_ref[c, 0] = acc.astype(nat_ref.dtype)
        elif DIL_GROUPS[group][1] == 1:
            nat_ref[DIL_Q0 + kind, 0] = acc.astype(nat_ref.dtype)
        else:
            dil = DIL_GROUPS[group][1]
            o_ref = dil_refs[group - 1]
            for half in range(LANES // TILE):
                stage_ref[half] = acc[:, _tile(half)]
            for rho in range(dil):
                for half in range(LANES // TILE):
                    o_ref[kind, rho, :, _tile(half)] = (
                        stage_ref[half, pl.ds(rho, bm // dil, stride=dil), :].astype(o_ref.dtype))


def _qkv_project(x3, g, w_in, layer, positions, bm=1024):
    b, s, d = x3.shape
    assert DIL_GROUPS[0][1] == 1
    half = HEAD_DIM // 2
    tiles = s // bm
    inv_freq = ROPE_THETA ** (-jnp.arange(half, dtype=jnp.float32) / half)
    pos = positions.reshape(1, b * s).astype(jnp.float32)
    out_shape = [jax.ShapeDtypeStruct((b, DIL_Q0 + 3, 1, s, LANES), jnp.bfloat16)]
    out_specs = [pl.BlockSpec((None, DIL_Q0 + 3, 1, bm, LANES), lambda bi, t: (bi, 0, 0, t, 0))]
    for _, dil in DIL_GROUPS[1:]:
        out_shape.append(jax.ShapeDtypeStruct((b, 3, dil, s // dil, LANES), jnp.bfloat16))
        out_specs.append(pl.BlockSpec((None, 3, dil, bm // dil, LANES), lambda bi, t: (bi, 0, 0, t, 0)))
    return pl.pallas_call(
        functools.partial(_qkv_kernel, layer=layer),
        out_shape=out_shape,
        grid=(b, s // bm),
        in_specs=[
            pl.BlockSpec((None, bm, d), lambda bi, t: (bi, t, 0)),
            pl.BlockSpec((1, d), lambda bi, t: (0, 0)),
            pl.BlockSpec(memory_space=pl.ANY),
            pl.BlockSpec((1, bm), lambda bi, t: (0, bi * tiles + t)),
            pl.BlockSpec((half, 1), lambda bi, t: (0, 0)),
        ],
        out_specs=out_specs,
        scratch_shapes=[pltpu.VMEM((LANES // TILE, bm, TILE), jnp.float32), pltpu.VMEM((d, QKV_WIDTH), jnp.bfloat16)],
        compiler_params=pltpu.CompilerParams(
            dimension_semantics=("arbitrary",) * 2, vmem_limit_bytes=VMEM_LIMIT),
        name="qkv_project",
    )(x3, g, w_in, pos, inv_freq.reshape(half, 1))


def _na_window_start(row0, rows):
    return jnp.clip(row0 - NA_WIN_ROWS // 2, 0, rows - NA_K_ROWS)


def _na_bias_kernel(rpb_ref, out_ref, *, rows):
    n_off_c = 2 * NA_WIN_COLS - 1
    qc = lax.broadcasted_iota(jnp.int32, (GRID_W, TILE), 0)
    kc = lax.broadcasted_iota(jnp.int32, (GRID_W, TILE), 1)
    cs = jnp.clip(qc - NA_WIN_COLS // 2, 0, GRID_W - NA_WIN_COLS)
    col_valid = (kc >= cs) & (kc < cs + NA_WIN_COLS)
    neg = jnp.full((GRID_W, GRID_W), NEG_INF, jnp.float32)
    for h in range(HEADS_PER_CALL):
        toeplitz = []
        for ro in range(2 * NA_WIN_ROWS - 1):
            row = jnp.broadcast_to(rpb_ref[h, ro:ro + 1, :], (GRID_W, TILE))
            t = pltpu.roll(row, TILE - (n_off_c // 2), axis=1, stride=1, stride_axis=0)
            toeplitz.append(jnp.where(col_valid, t * LOG2E, NEG_INF)[:, :GRID_W])
        for variant, r0 in enumerate((0, NA_Q_ROWS, rows - NA_Q_ROWS)):
            w0 = int(np.clip(r0 - NA_WIN_ROWS // 2, 0, rows - NA_K_ROWS))
            for qr in range(NA_Q_ROWS):
                r = r0 + qr
                rs = int(np.clip(r - NA_WIN_ROWS // 2, 0, rows - NA_WIN_ROWS))
                blocks = [toeplitz[w0 + kr - r + NA_WIN_ROWS - 1] if rs <= w0 + kr < rs + NA_WIN_ROWS else neg
                          for kr in range(NA_K_ROWS)]
                row0 = (h * NA_Q_ROWS + qr) * GRID_W
                out_ref[variant, row0:row0 + GRID_W, :] = jnp.concatenate(blocks, axis=1)


def _na_bias_tables(rpb, rows):
    h, n_ro, n_co = rpb.shape
    rpb_pad = jnp.pad(rpb.astype(jnp.float32), ((0, 0), (0, 16 - n_ro), (0, TILE - n_co)))
    n_groups = h // HEADS_PER_CALL
    q, k = HEADS_PER_CALL * NA_Q_ROWS * GRID_W, NA_K_ROWS * GRID_W
    return pl.pallas_call(
        functools.partial(_na_bias_kernel, rows=rows),
        out_shape=jax.ShapeDtypeStruct((3, n_groups, q, k), jnp.float32),
        grid=(n_groups,),
        in_specs=[pl.BlockSpec((HEADS_PER_CALL, 16, TILE), lambda g: (g, 0, 0))],
        out_specs=pl.BlockSpec((3, None, q, k), lambda g: (0, g, 0, 0)),
        compiler_params=pltpu.CompilerParams(dimension_semantics=("arbitrary",), vmem_limit_bytes=VMEM_LIMIT),
        name="na_bias_tables",
    )(rpb_pad)


def _attn_kernel(q_ref, k_ref, v_ref, bias_ref, o_ref, *rest, mq, kw, blocks_per_seq, window_start, row_chunk,
                 with_lse, interior_dead_tiles, split_heads, mxu_row_sums):
    f32 = jnp.float32
    if with_lse:
        lse_ref, *scratch = rest
    else:
        scratch = rest
    s_refs, p_refs, st_refs = scratch[0:2], scratch[2:4], scratch[4:6]
    ST_MAX, ST_SUM, ST_MAX_KEPT = 0, 1, 2
    n_blocks = q_ref.shape[0] * blocks_per_seq
    rows_s = HEADS_PER_CALL * mq
    rows_b = bias_ref.shape[1]
    log2_bps = blocks_per_seq.bit_length() - 1
    assert blocks_per_seq == 1 << log2_bps and n_blocks % 2 == 0 and n_blocks >= 2
    assert rows_s % rows_b == 0 and kw % TILE == 0

    def locate(j):
        j = jnp.int32(j)
        seq = lax.shift_right_logical(j, log2_bps)
        n = j & (blocks_per_seq - 1)
        q0 = pl.multiple_of(n * mq, mq)
        w0 = pl.multiple_of(window_start(n), 64)
        variant = jnp.where(n == 0, 0, jnp.where(n == blocks_per_seq - 1, 2, 1))
        return seq, q0, w0, variant

    def lanes_of(x, width):
        return jnp.concatenate([x] * (width // TILE), axis=1)

    def stage_a(j, par):
        seq, q0, w0, variant = locate(j)
        bias = jnp.concatenate([bias_ref[variant]] * (rows_s // rows_b), axis=0)
        nt = (((1,), (1,)), ((), ()))
        if split_heads:
            for t in range(LANES // TILE):
                qs = _stack_head_pair(q_ref[seq, pl.ds(q0, mq), _tile(t)])
                s = lax.dot_general(qs, k_ref[seq, pl.ds(w0, kw), _tile(t)], nt, preferred_element_type=f32)
                rows = slice(2 * t * mq, (2 * t + 2) * mq)
                s_refs[par][rows, :] = s + bias[rows]
        else:
            qs = _stack_heads(q_ref[seq, pl.ds(q0, mq), :])
            s = lax.dot_general(qs, k_ref[seq, pl.ds(w0, kw), :], nt, preferred_element_type=f32)
            s_refs[par][...] = s + bias
        for r0 in range(0, rows_s, row_chunk):
            mx = jnp.max(s_refs[par][r0:r0 + row_chunk, :], axis=-1, keepdims=True)
            st_refs[par][ST_MAX, r0:r0 + row_chunk, :] = jnp.broadcast_to(mx, (row_chunk, TILE))

    def stage_b(par, p_ref, dead_tiles):
        for r0 in range(0, rows_s, row_chunk):
            mx = st_refs[par][ST_MAX, r0:r0 + row_chunk, :]
            live = [t for t in range(kw // TILE) if t not in dead_tiles(r0)]
            s = jnp.concatenate([s_refs[par][r0:r0 + row_chunk, _tile(t)] for t in live], axis=1)
            p = jnp.exp2(s - lanes_of(mx, s.shape[1]))
            if not mxu_row_sums:
                l = jnp.sum(p, axis=-1, keepdims=True)
                st_refs[par][ST_SUM, r0:r0 + row_chunk, :] = jnp.broadcast_to(l, (row_chunk, TILE))
                if with_lse:
                    st_refs[par][ST_MAX_KEPT, r0:r0 + row_chunk, :] = mx
            p = p.astype(p_ref.dtype)
            for i, t in enumerate(live):
                p_ref[r0:r0 + row_chunk, _tile(t)] = p[:, _tile(i)]

    def stage_c(j, par, p_ref):
        seq, q0, w0, _ = locate(j)
        if mxu_row_sums:
            first = _first_head_of_tile(mq)
            ones = jnp.ones((kw, TILE), p_ref.dtype)
            tiles = []
            for t in range(LANES // TILE):
                rhs = jnp.concatenate([v_ref[seq, pl.ds(w0, kw), _tile(t)], ones], axis=1)
                pv = jnp.dot(p_ref[2 * t * mq:(2 * t + 2) * mq, :], rhs, preferred_element_type=f32)
                den = jnp.where(first, pv[:mq, TILE:], pv[mq:, TILE:])
                tiles.append(jnp.where(first, pv[:mq, :TILE], pv[mq:, :TILE]) * (1.0 / den))
            o_ref[seq, pl.ds(q0, mq), :] = jnp.concatenate(tiles, axis=1).astype(o_ref.dtype)
            return
        if split_heads:
            first = _first_head_of_tile(mq)
            tiles = []
            for t in range(LANES // TILE):
                vwin = v_ref[seq, pl.ds(w0, kw), _tile(t)]
                pv = jnp.dot(p_ref[2 * t * mq:(2 * t + 2) * mq, :], vwin, preferred_element_type=f32)
                tiles.append(jnp.where(first, pv[:mq], pv[mq:]))
            out = jnp.concatenate(tiles, axis=1)
        else:
            pv = jnp.dot(p_ref[...], v_ref[seq, pl.ds(w0, kw), :], preferred_element_type=f32)
            out = _unstack_heads(pv, mq)
        l = _unstack_heads(st_refs[par][ST_SUM], mq)
        o_ref[seq, pl.ds(q0, mq), :] = (out * (1.0 / l)).astype(o_ref.dtype)
        if with_lse:
            lse_ref[seq, pl.ds(q0, mq), :] = _unstack_heads(st_refs[par][ST_MAX_KEPT], mq) + jnp.log2(l)

    def no_dead_tiles(r0):
        return ()

    def steady_state(first_pair, dead_tiles):
        def body(i, carry):
            for par in (0, 1):
                j = 2 * i + par
                if mxu_row_sums:
                    stage_c(j - 2, par, p_refs[par])
                stage_a(j, par)
                stage_b(1 - par, p_refs[1 - par], dead_tiles)
                if not mxu_row_sums:
                    stage_c(j - 2, par, p_refs[par])
            return carry

        lax.fori_loop(first_pair, n_blocks // 2, body, 0)

    if interior_dead_tiles is None:
        stage_a(0, 0)
        stage_a(1, 1)
        stage_b(0, p_refs[0], no_dead_tiles)
        steady_state(1, no_dead_tiles)
        stage_b(1, p_refs[1], no_dead_tiles)
        stage_c(n_blocks - 2, 0, p_refs[0])
        stage_c(n_blocks - 1, 1, p_refs[1])
    else:
        p_edge = scratch[6]
        assert q_ref.shape[0] == 1 and n_blocks >= 6
        for p_ref in p_refs:
            for r0 in range(0, rows_s, row_chunk):
                for t in interior_dead_tiles(r0):
                    p_ref[r0:r0 + row_chunk, _tile(t)] = jnp.zeros((row_chunk, TILE), p_ref.dtype)
        stage_a(0, 0)
        stage_a(1, 1)
        stage_b(0, p_edge, no_dead_tiles)
        stage_a(2, 0)
        stage_b(1, p_refs[1], interior_dead_tiles)
        stage_c(0, 0, p_edge)
        stage_a(3, 1)
        stage_b(0, p_refs[0], interior_dead_tiles)
        stage_c(1, 1, p_refs[1])
        steady_state(2, interior_dead_tiles)
        stage_b(1, p_edge, no_dead_tiles)
        stage_c(n_blocks - 2, 0, p_refs[0])
        stage_c(n_blocks - 1, 1, p_edge)


def _windowed_attention(arrays, chunks, bias, bias_spec, grid, batch_of, chunk_of, out_chunks, *, name,
                        mq, kw, window_start, row_chunk, with_lse, split_heads, mxu_row_sums=False,
                        interior_dead_tiles=None):
    b, _, r, sub_len, _ = arrays[0].shape
    blocks_per_seq = sub_len // mq

    def spec(chunk):
        return pl.BlockSpec((None, None, r, sub_len, LANES),
                            lambda *g: (batch_of(*g), chunk + chunk_of(*g), 0, 0, 0))

    out_shape = [jax.ShapeDtypeStruct((b, out_chunks, r, sub_len, LANES), jnp.bfloat16)]
    out_specs = [spec(0)]
    if with_lse:
        out_shape.append(jax.ShapeDtypeStruct((b, out_chunks, r, sub_len, LANES), jnp.float32))
        out_specs.append(spec(0))
    rows_s = HEADS_PER_CALL * mq
    scratch = ([pltpu.VMEM((rows_s, kw), jnp.float32)] * 2 + [pltpu.VMEM((rows_s, kw), jnp.bfloat16)] * 2
               + [pltpu.VMEM((3 if with_lse else 2, rows_s, TILE), jnp.float32)] * 2)
    if interior_dead_tiles is not None:
        scratch.append(pltpu.VMEM((rows_s, kw), jnp.bfloat16))
    return pl.pallas_call(
        functools.partial(_attn_kernel, mq=mq, kw=kw, blocks_per_seq=blocks_per_seq, window_start=window_start,
                          row_chunk=row_chunk, with_lse=with_lse, interior_dead_tiles=interior_dead_tiles,
                          split_heads=split_heads, mxu_row_sums=mxu_row_sums),
        out_shape=out_shape,
        grid=grid,
        in_specs=[spec(c) for c in chunks] + [bias_spec],
        out_specs=out_specs,
        scratch_shapes=scratch,
        compiler_params=pltpu.CompilerParams(
            dimension_semantics=("arbitrary",) * len(grid), vmem_limit_bytes=VMEM_LIMIT),
        name=name,
    )(*arrays, bias)


def _na_attention(nat, bias):
    b, _, _, seq, _ = nat.shape
    rows = seq // GRID_W
    n_groups = NA_HEADS // HEADS_PER_CALL
    m_q = NA_Q_ROWS * GRID_W
    for blk in range(1, rows // NA_Q_ROWS - 1):
        r0 = blk * NA_Q_ROWS
        assert 0 <= r0 - NA_WIN_ROWS // 2 <= rows - NA_K_ROWS
        assert r0 + NA_Q_ROWS - 1 - NA_WIN_ROWS // 2 <= rows - NA_WIN_ROWS

    def interior_dead_tiles(row):
        qr = (row % m_q) // GRID_W
        per_tile = TILE // GRID_W
        return tuple(t for t in range(NA_K_ROWS // per_tile)
                     if not any(0 <= kr - qr < NA_WIN_ROWS for kr in range(t * per_tile, (t + 1) * per_tile)))

    (out,) = _windowed_attention(
        (nat, nat, nat), (NA_Q0, NA_K0, NA_V0), bias,
        pl.BlockSpec((3, None) + bias.shape[2:], lambda g, bi: (0, g, 0, 0)),
        grid=(n_groups, b), batch_of=lambda g, bi: bi, chunk_of=lambda g, bi: g, out_chunks=n_groups,
        name="na_attention", mq=m_q, kw=NA_K_ROWS * GRID_W,
        window_start=lambda n: _na_window_start(n * NA_Q_ROWS, rows) * GRID_W, row_chunk=32, with_lse=False,
        split_heads=False, mxu_row_sums=True, interior_dead_tiles=interior_dead_tiles)
    return out.reshape(b, n_groups, seq, LANES)


def _dil_mask_tables():
    i = np.arange(DIL_Q)[:, None]
    j = np.arange(DIL_K)[None, :]
    tabs = []
    for off in (0, DIL_RADIUS, DIL_K - DIL_Q):
        tabs.append(np.where(np.abs(off + i - j) <= DIL_RADIUS, 0.0, NEG_INF))
    return jnp.asarray(np.stack(tabs), dtype=jnp.float32)


def _dil_attention(arr, mask, chunk0):
    b, _, dil, sub_len, _ = arr.shape
    o, lse = _windowed_attention(
        (arr, arr, arr), (chunk0, chunk0 + 1, chunk0 + 2), mask,
        pl.BlockSpec(mask.shape, lambda bi: (0, 0, 0)),
        grid=(b,), batch_of=lambda bi: bi, chunk_of=lambda bi: 0, out_chunks=1,
        name=f"dil_attention_d{dil}", mq=DIL_Q, kw=DIL_K,
        window_start=lambda n: jnp.clip(n * DIL_Q - DIL_RADIUS, 0, sub_len - DIL_K), row_chunk=DIL_Q, with_lse=True,
        split_heads=True)
    return o.reshape(b, dil, sub_len, LANES), lse.reshape(b, dil, sub_len, LANES)


def _post_kernel(x_ref, yna_ref, o1_ref, o2_ref, o3_ref, l1_ref, l2_ref, l3_ref, p_ref,
                 gmix_ref, gmlp_ref, gple_ref, gfin_ref,
                 win_hbm, wbna_hbm, wbdil_hbm, wout_hbm, wup_hbm, wdown_hbm, wpg_hbm, wpp_hbm,
                 out_ref, *scratch, layer, final_norm, ff_chunk, sub_rows):
    f32, bf16 = jnp.float32, jnp.bfloat16
    bm, d = x_ref.shape
    stage_refs = scratch[:-8]
    wgate_ref, wbna_ref, wbdil_ref, wout_ref, wup_ref, wdown_ref, wpg_ref, wpp_ref = scratch[-8:]
    d_ff = wup_ref.shape[1]

    @pl.when(_is_first_step())
    def _():
        _load_weights_bf16(layer, [(win_hbm, QKV_WIDTH, wgate_ref)])
        _load_weights_bf16(layer, [(wbna_hbm, 0, wbna_ref), (wbdil_hbm, 0, wbdil_ref), (wout_hbm, 0, wout_ref),
                                   (wdown_hbm, 0, wdown_ref), (wpg_hbm, 0, wpg_ref), (wpp_hbm, 0, wpp_ref)])
        _load_weights_bf16(layer, [(wup_hbm, 0, wup_ref)])

    stages = iter(stage_refs)
    token_order = []
    for ref in (o1_ref, o2_ref, o3_ref, l1_ref, l2_ref, l3_ref):
        dil = ref.shape[0]
        if dil == 1:
            token_order.append(lambda r0, ref=ref: ref[0, r0:r0 + sub_rows, :].astype(f32))
            continue
        stage_ref = next(stages)
        for rho in range(dil):
            blk = ref[rho].astype(f32)
            for half in range(LANES // TILE):
                stage_ref[half, pl.ds(rho, bm // dil, stride=dil), :] = blk[:, _tile(half)]
        token_order.append(lambda r0, stage_ref=stage_ref: jnp.concatenate(
            [stage_ref[half, r0:r0 + sub_rows, :] for half in range(LANES // TILE)], axis=1))

    def dot(lhs, w):
        return jnp.dot(lhs, w, preferred_element_type=f32)

    chains = [dict(r0=r0) for r0 in range(0, bm, sub_rows)]
    for c in chains:
        r0 = c["r0"]
        c["x"] = x_ref[r0:r0 + sub_rows, :]
        a = _rms(c["x"], gmix_ref[...]).astype(bf16)
        c["gate_na"] = _sigmoid(dot(a, wgate_ref[:, :d]))
        c["gate_dil"] = _sigmoid(dot(a, wgate_ref[:, d:]))
    for c in chains:
        r0 = c["r0"]
        o1, o2, o3, l1, l2, l3 = [get(r0) for get in token_order]
        mx = jnp.maximum(jnp.maximum(l1, l2), l3)
        e1, e2, e3 = jnp.exp2(l1 - mx), jnp.exp2(l2 - mx), jnp.exp2(l3 - mx)
        ydil = (e1 * o1 + e2 * o2 + e3 * o3) * (1.0 / (e1 + e2 + e3))
        yna = jnp.concatenate([yna_ref[g, r0:r0 + sub_rows, :] for g in range(yna_ref.shape[0])], axis=1)
        c["mixed"] = (c.pop("gate_na") * dot(yna, wbna_ref[...])
                      + c.pop("gate_dil") * dot(ydil.astype(bf16), wbdil_ref[...]))
    for c in chains:
        c["h"] = c.pop("x") + dot(c.pop("mixed").astype(bf16), wout_ref[...])
        c["c"] = _rms(c["h"], gmlp_ref[...]).astype(bf16)
        c["acc"] = jnp.zeros_like(c["h"])
    for f in range(d_ff // ff_chunk):
        for c in chains:
            u = dot(c["c"], wup_ref[:, f * ff_chunk:(f + 1) * ff_chunk])
            c["u"] = jnp.square(jnp.maximum(u, 0.0)).astype(bf16)
        for c in chains:
            c["acc"] = c["acc"] + dot(c.pop("u"), wdown_ref[f * ff_chunk:(f + 1) * ff_chunk, :])
    for c in chains:
        r0 = c["r0"]
        c["h"] = c["h"] + c.pop("acc")
        e = _rms(c["h"], gple_ref[...]).astype(bf16)
        c["pg"] = _sigmoid(dot(e, wpg_ref[...]))
        c["pp"] = dot(p_ref[r0:r0 + sub_rows, :].astype(bf16), wpp_ref[...])
    for c in chains:
        r0 = c["r0"]
        h = c["h"] + c["pg"] * c["pp"]
        if final_norm:
            h = _rms(h, gfin_ref[...])
        out_ref[r0:r0 + sub_rows, :] = h


def _post_block(x3, yna, outs, lses, p3, gains, weights, layer, final_norm, bm=512, sub_rows=256, ff_chunk=1024):
    b, s, d = x3.shape
    w_in = weights[0]
    resident = [(d, w_in.shape[2] - QKV_WIDTH)] + [w.shape[1:] for w in weights[1:]]

    def rows(arr):
        return pl.BlockSpec((None, bm, arr.shape[-1]), lambda bi, t: (bi, t, 0))

    def residues(arr):
        lead = arr.shape[1]
        rows_per = bm if arr is yna else bm // lead
        return pl.BlockSpec((None, lead, rows_per, arr.shape[-1]), lambda bi, t: (bi, 0, t, 0))

    def whole(arr):
        return pl.BlockSpec(arr.shape, lambda bi, t: (0, 0))

    n_stages = sum(a.shape[1] > 1 for a in (*outs, *lses))
    return pl.pallas_call(
        functools.partial(_post_kernel, layer=layer, final_norm=final_norm, ff_chunk=ff_chunk, sub_rows=sub_rows),
        out_shape=jax.ShapeDtypeStruct((b, s, d), jnp.float32),
        grid=(b, s // bm),
        in_specs=([rows(x3), residues(yna)] + [residues(a) for a in (*outs, *lses)] + [rows(p3)]
                  + [whole(g) for g in gains] + [pl.BlockSpec(memory_space=pl.ANY)] * len(weights)),
        out_specs=rows(x3),
        scratch_shapes=([pltpu.VMEM((LANES // TILE, bm, TILE), jnp.float32)] * n_stages
                        + [pltpu.VMEM(shape, jnp.bfloat16) for shape in resident]),
        compiler_params=pltpu.CompilerParams(
            dimension_semantics=("arbitrary",) * 2, vmem_limit_bytes=VMEM_LIMIT),
        name="post_block",
    )(x3, yna, *outs, *lses, p3, *gains, *weights)


def kernel(x, p, positions, g_mix, w_in, rpb, w_branch_na, w_branch_dil, w_out, g_mlp, w_up, w_down,
           g_ple, w_ple_gate, w_ple_proj, g_final):
    b, s, d = x.shape
    depth = w_in.shape[0]
    dil_mask = _dil_mask_tables()
    h = x
    for i in range(depth):
        nat, *dil_arrays = _qkv_project(h, g_mix[i].reshape(1, d), w_in, i, positions)
        yna = _na_attention(nat, _na_bias_tables(rpb[i], s // GRID_W))
        outs, lses = [], []
        for g, (window, dil) in enumerate(DIL_GROUPS):
            assert window // (2 * dil) == DIL_RADIUS
            if dil == 1:
                o, lse = _dil_attention(nat, dil_mask, DIL_Q0)
            else:
                o, lse = _dil_attention(dil_arrays[g - 1], dil_mask, 0)
            outs.append(o)
            lses.append(lse)
        gains = [g_mix[i].reshape(1, d), g_mlp[i].reshape(1, d), g_ple[i].reshape(1, d), g_final.reshape(1, d)]
        weights = [w_in, w_branch_na, w_branch_dil, w_out, w_up, w_down, w_ple_gate, w_ple_proj]
        h = _post_block(h, yna, outs, lses, p[i], gains, weights, layer=i, final_norm=(i == depth - 1))
    return h
```

```python
import functools

import jax
import jax.numpy as jnp
import numpy as np
from jax import lax
from jax.experimental import pallas as pl
from jax.experimental.pallas import tpu as pltpu

HEAD_DIM = 64
GRID_W = 64
NA_HEADS = 8
NA_WIN_ROWS = 8
NA_WIN_COLS = 16
DIL_GROUPS = ((128, 1), (512, 4), (2048, 16))
DIL_HEADS_PER_GROUP = 4
ROPE_THETA = 10000.0
RMS_EPS = 1e-6
NEG_INF = -1e30
LOG2E = 1.4426950408889634

TILE = 128
LANES = 256
HEADS_PER_CALL = LANES // HEAD_DIM
NA_WIDTH = NA_HEADS * HEAD_DIM
DIL_WIDTH = DIL_HEADS_PER_GROUP * len(DIL_GROUPS) * HEAD_DIM
QKV_WIDTH = 3 * NA_WIDTH + 3 * DIL_WIDTH
N_QKV_CHUNKS = QKV_WIDTH // LANES
NA_Q0, NA_K0, NA_V0 = 0, 2, 4
DIL_Q0, DIL_K0, DIL_V0 = 6, 9, 12

NA_Q_ROWS = 4
NA_K_ROWS = 12
DIL_Q = 128
DIL_K = 256
DIL_RADIUS = 64

VMEM_LIMIT = 56 * 1024 * 1024


def _rms(x, g):
    ms = jnp.mean(x * x, axis=-1, keepdims=True)
    return x * lax.rsqrt(ms + RMS_EPS) * g


def _sigmoid(x):
    return 1.0 / (1.0 + jnp.exp(-x))


def _first_head_of_tile(rows):
    assert HEAD_DIM * 2 == TILE and LANES == 2 * TILE
    return lax.broadcasted_iota(jnp.int32, (rows, TILE), 1) < HEAD_DIM


def _tile(t):
    return slice(t * TILE, (t + 1) * TILE)


def _stack_head_pair(q_tile):
    first = _first_head_of_tile(q_tile.shape[0])
    zero = jnp.zeros_like(q_tile)
    return jnp.concatenate([jnp.where(first, q_tile, zero), jnp.where(first, zero, q_tile)], axis=0)


def _stack_heads(q):
    zero = jnp.zeros((2 * q.shape[0], TILE), q.dtype)
    pairs = [_stack_head_pair(q[:, _tile(t)]) for t in range(LANES // TILE)]
    return jnp.concatenate([jnp.concatenate([pairs[0], zero], axis=1),
                            jnp.concatenate([zero, pairs[1]], axis=1)], axis=0)


def _unstack_heads(x, m):
    first = _first_head_of_tile(m)
    tiles = []
    for t in range(LANES // TILE):
        lanes = _tile(t) if x.shape[1] == LANES else _tile(0)
        tiles.append(jnp.where(first, x[2 * t * m:(2 * t + 1) * m, lanes], x[(2 * t + 1) * m:(2 * t + 2) * m, lanes]))
    return jnp.concatenate(tiles, axis=1)


WEIGHT_STAGE_ELEMS = 1 << 18
WEIGHT_STAGE_SLOTS = 3


def _load_weights_bf16(layer, weights):
    n_cols = weights[0][2].shape[1]
    rc = 1 << ((WEIGHT_STAGE_ELEMS // n_cols).bit_length() - 1)
    rc = min([rc] + [w_vmem.shape[0] for _, _, w_vmem in weights])
    assert rc % 16 == 0
    chunks = []
    for w_hbm, col0, w_vmem in weights:
        assert w_vmem.shape[1] == n_cols and w_vmem.shape[0] % rc == 0
        chunks += [(w_hbm, col0, w_vmem, r0) for r0 in range(0, w_vmem.shape[0], rc)]
    slots = WEIGHT_STAGE_SLOTS

    def body(stage, sems):
        def copy(i):
            w_hbm, col0, _, r0 = chunks[i]
            return pltpu.make_async_copy(w_hbm.at[layer, pl.ds(r0, rc), pl.ds(col0, n_cols)],
                                         stage.at[i % slots], sems.at[i % slots])

        for i in range(min(slots - 1, len(chunks))):
            copy(i).start()
        for i, (_, _, w_vmem, r0) in enumerate(chunks):
            if i + slots - 1 < len(chunks):
                copy(i + slots - 1).start()
            copy(i).wait()
            w_vmem[r0:r0 + rc, :] = stage[i % slots].astype(w_vmem.dtype)

    pl.run_scoped(body, pltpu.VMEM((slots, rc, n_cols), jnp.float32), pltpu.SemaphoreType.DMA((slots,)))


def _is_first_step():
    return (pl.program_id(0) == 0) & (pl.program_id(1) == 0)


def _qkv_kernel(x_ref, g_ref, win_hbm, pos_ref, freq_ref, nat_ref, *rest, layer):
    dil_refs, stage_ref, w_ref = rest[:-2], rest[-2], rest[-1]
    bm = x_ref.shape[0]

    @pl.when(_is_first_step())
    def _():
        _load_weights_bf16(layer, [(win_hbm, 0, w_ref)])

    a = _rms(x_ref[...], g_ref[...]).astype(jnp.bfloat16)
    ang = pos_ref[...] * freq_ref[...]
    cos_t, sin_t = jnp.cos(ang), jnp.sin(ang)
    reps = TILE // cos_t.shape[0]
    cos = jnp.concatenate([cos_t] * reps, axis=0).T
    sin = jnp.concatenate([-sin_t, sin_t] * (reps // 2), axis=0).T
    cos = jnp.concatenate([cos] * (LANES // TILE), axis=1)
    sin = jnp.concatenate([sin] * (LANES // TILE), axis=1)
    first_half = (lax.broadcasted_iota(jnp.int32, cos.shape, 1) % HEAD_DIM) < (HEAD_DIM // 2)
    scale = HEAD_DIM ** -0.5 * LOG2E
    n_groups = len(DIL_GROUPS)
    for c in range(N_QKV_CHUNKS):
        acc = jnp.dot(a, w_ref[:, c * LANES:(c + 1) * LANES], preferred_element_type=jnp.float32)
        if DIL_Q0 <= c < DIL_V0:
            swapped = jnp.where(first_half,
                                pltpu.roll(acc, LANES - HEAD_DIM // 2, axis=1),
                                pltpu.roll(acc, HEAD_DIM // 2, axis=1))
            acc = acc * cos + swapped * sin
        if NA_Q0 <= c < NA_K0 or DIL_Q0 <= c < DIL_K0:
            acc = acc * scale
        group, kind = (c - DIL_Q0) % n_groups, (c - DIL_Q0) // n_groups
        if c < DIL_Q0:
            nat_ref[c, 0] = acc.astype(nat_ref.dtype)
        elif DIL_GROUPS[group][1] == 1:
            nat_ref[DIL_Q0 + kind, 0] = acc.astype(nat_ref.dtype)
        else:
            dil = DIL_GROUPS[group][1]
            o_ref = dil_refs[group - 1]
            for half in range(LANES // TILE):
                stage_ref[half] = acc[:, _tile(half)]
            for rho in range(dil):
                for half in range(LANES // TILE):
                    o_ref[kind, rho, :, _tile(half)] = (
                        stage_ref[half, pl.ds(rho, bm // dil, stride=dil), :].astype(o_ref.dtype))


def _qkv_project(x3, g, w_in, layer, positions, bm=1024):
    b, s, d = x3.shape
    assert DIL_GROUPS[0][1] == 1
    half = HEAD_DIM // 2
    tiles = s // bm
    inv_freq = ROPE_THETA ** (-jnp.arange(half, dtype=jnp.float32) / half)
    pos = positions.reshape(1, b * s).astype(jnp.float32)
    out_shape = [jax.ShapeDtypeStruct((b, DIL_Q0 + 3, 1, s, LANES), jnp.bfloat16)]
    out_specs = [pl.BlockSpec((None, DIL_Q0 + 3, 1, bm, LANES), lambda bi, t: (bi, 0, 0, t, 0))]
    for _, dil in DIL_GROUPS[1:]:
        out_shape.append(jax.ShapeDtypeStruct((b, 3, dil, s // dil, LANES), jnp.bfloat16))
        out_specs.append(pl.BlockSpec((None, 3, dil, bm // dil, LANES), lambda bi, t: (bi, 0, 0, t, 0)))
    return pl.pallas_call(
        functools.partial(_qkv_kernel, layer=layer),
        out_shape=out_shape,
        grid=(b, s // bm),
        in_specs=[
            pl.BlockSpec((None, bm, d), lambda bi, t: (bi, t, 0)),
            pl.BlockSpec((1, d), lambda bi, t: (0, 0)),
            pl.BlockSpec(memory_space=pl.ANY),
            pl.BlockSpec((1, bm), lambda bi, t: (0, bi * tiles + t)),
            pl.BlockSpec((half, 1), lambda bi, t: (0, 0)),
        ],
        out_specs=out_specs,
        scratch_shapes=[pltpu.VMEM((LANES // TILE, bm, TILE), jnp.float32), pltpu.VMEM((d, QKV_WIDTH), jnp.bfloat16)],
        compiler_params=pltpu.CompilerParams(
            dimension_semantics=("arbitrary",) * 2, vmem_limit_bytes=VMEM_LIMIT),
        name="qkv_project",
    )(x3, g, w_in, pos, inv_freq.reshape(half, 1))


def _na_window_start(row0, rows):
    return jnp.clip(row0 - NA_WIN_ROWS // 2, 0, rows - NA_K_ROWS)


def _na_bias_build(rpb_ref, out_ref, *, rows):
    n_off_c = 2 * NA_WIN_COLS - 1
    qc = lax.broadcasted_iota(jnp.int32, (GRID_W, TILE), 0)
    kc = lax.broadcasted_iota(jnp.int32, (GRID_W, TILE), 1)
    cs = jnp.clip(qc - NA_WIN_COLS // 2, 0, GRID_W - NA_WIN_COLS)
    col_valid = (kc >= cs) & (kc < cs + NA_WIN_COLS)
    neg = jnp.full((GRID_W, GRID_W), NEG_INF, jnp.float32)
    for h in range(HEADS_PER_CALL):
        toeplitz = []
        for ro in range(2 * NA_WIN_ROWS - 1):
            row = jnp.broadcast_to(rpb_ref[h, ro:ro + 1, :], (GRID_W, TILE))
            t = pltpu.roll(row, TILE - (n_off_c // 2), axis=1, stride=1, stride_axis=0)
            toeplitz.append(jnp.where(col_valid, t * LOG2E, NEG_INF)[:, :GRID_W])
        for variant, r0 in enumerate((0, NA_Q_ROWS, rows - NA_Q_ROWS)):
            w0 = int(np.clip(r0 - NA_WIN_ROWS // 2, 0, rows - NA_K_ROWS))
            for qr in range(NA_Q_ROWS):
                r = r0 + qr
                rs = int(np.clip(r - NA_WIN_ROWS // 2, 0, rows - NA_WIN_ROWS))
                blocks = [toeplitz[w0 + kr - r + NA_WIN_ROWS - 1] if rs <= w0 + kr < rs + NA_WIN_ROWS else neg
                          for kr in range(NA_K_ROWS)]
                row0 = (h * NA_Q_ROWS + qr) * GRID_W
                out_ref[variant, row0:row0 + GRID_W, :] = jnp.concatenate(blocks, axis=1)


def _attn_kernel(q_ref, k_ref, v_ref, bias_ref, o_ref, *rest, mq, kw, blocks_per_seq, window_start, row_chunk,
                 with_lse, interior_dead_tiles, split_heads, mxu_row_sums, bias_builder):
    f32 = jnp.float32
    if with_lse:
        lse_ref, *scratch = rest
    else:
        scratch = rest
    s_refs, p_refs, st_refs = scratch[0:2], scratch[2:4], scratch[4:6]
    if bias_builder is not None:
        table_ref = scratch[-1]
        pl.when(pl.program_id(1) == 0)(functools.partial(bias_builder, bias_ref, table_ref))
        bias_ref = table_ref
    ST_MAX, ST_SUM, ST_MAX_KEPT = 0, 1, 2
    n_blocks = q_ref.shape[0] * blocks_per_seq
    rows_s = HEADS_PER_CALL * mq
    rows_b = bias_ref.shape[1]
    log2_bps = blocks_per_seq.bit_length() - 1
    assert blocks_per_seq == 1 << log2_bps and n_blocks % 2 == 0 and n_blocks >= 2
    assert rows_s % rows_b == 0 and kw % TILE == 0

    def locate(j):
        j = jnp.int32(j)
        seq = lax.shift_right_logical(j, log2_bps)
        n = j & (blocks_per_seq - 1)
        q0 = pl.multiple_of(n * mq, mq)
        w0 = pl.multiple_of(window_start(n), 64)
        variant = jnp.where(n == 0, 0, jnp.where(n == blocks_per_seq - 1, 2, 1))
        return seq, q0, w0, variant

    def lanes_of(x, width):
        return jnp.concatenate([x] * (width // TILE), axis=1)

    def stage_a(j, par):
        seq, q0, w0, variant = locate(j)
        bias = jnp.concatenate([bias_ref[variant]] * (rows_s // rows_b), axis=0)
        nt = (((1,), (1,)), ((), ()))
        if split_heads:
            for t in range(LANES // TILE):
                qs = _stack_head_pair(q_ref[seq, pl.ds(q0, mq), _tile(t)])
                s = lax.dot_general(qs, k_ref[seq, pl.ds(w0, kw), _tile(t)], nt, preferred_element_type=f32)
                rows = slice(2 * t * mq, (2 * t + 2) * mq)
                s_refs[par][rows, :] = s + bias[rows]
        else:
            qs = _stack_heads(q_ref[seq, pl.ds(q0, mq), :])
            s = lax.dot_general(qs, k_ref[seq, pl.ds(w0, kw), :], nt, preferred_element_type=f32)
            s_refs[par][...] = s + bias
        for r0 in range(0, rows_s, row_chunk):
            mx = jnp.max(s_refs[par][r0:r0 + row_chunk, :], axis=-1, keepdims=True)
            st_refs[par][ST_MAX, r0:r0 + row_chunk, :] = jnp.broadcast_to(mx, (row_chunk, TILE))

    def stage_b(par, p_ref, dead_tiles):
        for r0 in range(0, rows_s, row_chunk):
            mx = st_refs[par][ST_MAX, r0:r0 + row_chunk, :]
            live = [t for t in range(kw // TILE) if t not in dead_tiles(r0)]
            s = jnp.concatenate([s_refs[par][r0:r0 + row_chunk, _tile(t)] for t in live], axis=1)
            p = jnp.exp2(s - lanes_of(mx, s.shape[1]))
            if not mxu_row_sums:
                l = jnp.sum(p, axis=-1, keepdims=True)
                st_refs[par][ST_SUM, r0:r0 + row_chunk, :] = jnp.broadcast_to(l, (row_chunk, TILE))
                if with_lse:
                    st_refs[par][ST_MAX_KEPT, r0:r0 + row_chunk, :] = mx
            p = p.astype(p_ref.dtype)
            for i, t in enumerate(live):
                p_ref[r0:r0 + row_chunk, _tile(t)] = p[:, _tile(i)]

    def stage_c(j, par, p_ref):
        seq, q0, w0, _ = locate(j)
        if mxu_row_sums:
            first = _first_head_of_tile(mq)
            ones = jnp.ones((kw, TILE), p_ref.dtype)
            tiles = []
            for t in range(LANES // TILE):
                rhs = jnp.concatenate([v_ref[seq, pl.ds(w0, kw), _tile(t)], ones], axis=1)
                pv = jnp.dot(p_ref[2 * t * mq:(2 * t + 2) * mq, :], rhs, preferred_element_type=f32)
                den = jnp.where(first, pv[:mq, TILE:], pv[mq:, TILE:])
                tiles.append(jnp.where(first, pv[:mq, :TILE], pv[mq:, :TILE]) * (1.0 / den))
            o_ref[seq, pl.ds(q0, mq), :] = jnp.concatenate(tiles, axis=1).astype(o_ref.dtype)
            return
        if split_heads:
            first = _first_head_of_tile(mq)
            tiles = []
            for t in range(LANES // TILE):
                vwin = v_ref[seq, pl.ds(w0, kw), _tile(t)]
                pv = jnp.dot(p_ref[2 * t * mq:(2 * t + 2) * mq, :], vwin, preferred_element_type=f32)
                tiles.append(jnp.where(first, pv[:mq], pv[mq:]))
            out = jnp.concatenate(tiles, axis=1)
        else:
            pv = jnp.dot(p_ref[...], v_ref[seq, pl.ds(w0, kw), :], preferred_element_type=f32)
            out = _unstack_heads(pv, mq)
        l = _unstack_heads(st_refs[par][ST_SUM], mq)
        o_ref[seq, pl.ds(q0, mq), :] = (out * (1.0 / l)).astype(o_ref.dtype)
        if with_lse:
            lse_ref[seq, pl.ds(q0, mq), :] = _unstack_heads(st_refs[par][ST_MAX_KEPT], mq) + jnp.log2(l)

    def no_dead_tiles(r0):
        return ()

    def steady_state(first_pair, dead_tiles):
        def body(i, carry):
            for par in (0, 1):
                j = 2 * i + par
                if mxu_row_sums:
                    stage_c(j - 2, par, p_refs[par])
                stage_a(j, par)
                stage_b(1 - par, p_refs[1 - par], dead_tiles)
                if not mxu_row_sums:
                    stage_c(j - 2, par, p_refs[par])
            return carry

        lax.fori_loop(first_pair, n_blocks // 2, body, 0)

    if interior_dead_tiles is None:
        stage_a(0, 0)
        stage_a(1, 1)
        stage_b(0, p_refs[0], no_dead_tiles)
        steady_state(1, no_dead_tiles)
        stage_b(1, p_refs[1], no_dead_tiles)
        stage_c(n_blocks - 2, 0, p_refs[0])
        stage_c(n_blocks - 1, 1, p_refs[1])
    else:
        p_edge = scratch[6]
        assert q_ref.shape[0] == 1 and n_blocks >= 6
        for p_ref in p_refs:
            for r0 in range(0, rows_s, row_chunk):
                for t in interior_dead_tiles(r0):
                    p_ref[r0:r0 + row_chunk, _tile(t)] = jnp.zeros((row_chunk, TILE), p_ref.dtype)
        stage_a(0, 0)
        stage_a(1, 1)
        stage_b(0, p_edge, no_dead_tiles)
        stage_a(2, 0)
        stage_b(1, p_refs[1], interior_dead_tiles)
        stage_c(0, 0, p_edge)
        stage_a(3, 1)
        stage_b(0, p_refs[0], interior_dead_tiles)
        stage_c(1, 1, p_refs[1])
        steady_state(2, interior_dead_tiles)
        stage_b(1, p_edge, no_dead_tiles)
        stage_c(n_blocks - 2, 0, p_refs[0])
        stage_c(n_blocks - 1, 1, p_edge)


def _windowed_attention(arrays, chunks, bias, bias_spec, grid, batch_of, chunk_of, out_chunks, *, name,
                        mq, kw, window_start, row_chunk, with_lse, split_heads, mxu_row_sums=False,
                        interior_dead_tiles=None, bias_builder=None, table_shape=None):
    b, _, r, sub_len, _ = arrays[0].shape
    blocks_per_seq = sub_len // mq

    def spec(chunk):
        return pl.BlockSpec((None, None, r, sub_len, LANES),
                            lambda *g: (batch_of(*g), chunk + chunk_of(*g), 0, 0, 0))

    out_shape = [jax.ShapeDtypeStruct((b, out_chunks, r, sub_len, LANES), jnp.bfloat16)]
    out_specs = [spec(0)]
    if with_lse:
        out_shape.append(jax.ShapeDtypeStruct((b, out_chunks, r, sub_len, LANES), jnp.float32))
        out_specs.append(spec(0))
    rows_s = HEADS_PER_CALL * mq
    scratch = ([pltpu.VMEM((rows_s, kw), jnp.float32)] * 2 + [pltpu.VMEM((rows_s, kw), jnp.bfloat16)] * 2
               + [pltpu.VMEM((3 if with_lse else 2, rows_s, TILE), jnp.float32)] * 2)
    if interior_dead_tiles is not None:
        scratch.append(pltpu.VMEM((rows_s, kw), jnp.bfloat16))
    if bias_builder is not None:
        scratch.append(pltpu.VMEM(table_shape, jnp.float32))
    return pl.pallas_call(
        functools.partial(_attn_kernel, mq=mq, kw=kw, blocks_per_seq=blocks_per_seq, window_start=window_start,
                          row_chunk=row_chunk, with_lse=with_lse, interior_dead_tiles=interior_dead_tiles,
                          split_heads=split_heads, mxu_row_sums=mxu_row_sums, bias_builder=bias_builder),
        out_shape=out_shape,
        grid=grid,
        in_specs=[spec(c) for c in chunks] + [bias_spec],
        out_specs=out_specs,
        scratch_shapes=scratch,
        compiler_params=pltpu.CompilerParams(
            dimension_semantics=("arbitrary",) * len(grid), vmem_limit_bytes=VMEM_LIMIT),
        name=name,
    )(*arrays, bias)


def _na_attention(nat, rpb):
    b, _, _, seq, _ = nat.shape
    rows = seq // GRID_W
    n_groups = NA_HEADS // HEADS_PER_CALL
    m_q = NA_Q_ROWS * GRID_W
    _, n_ro, n_co = rpb.shape
    rpb_pad = jnp.pad(rpb.astype(jnp.float32), ((0, 0), (0, 16 - n_ro), (0, TILE - n_co)))
    for blk in range(1, rows // NA_Q_ROWS - 1):
        r0 = blk * NA_Q_ROWS
        assert 0 <= r0 - NA_WIN_ROWS // 2 <= rows - NA_K_ROWS
        assert r0 + NA_Q_ROWS - 1 - NA_WIN_ROWS // 2 <= rows - NA_WIN_ROWS

    def interior_dead_tiles(row):
        qr = (row % m_q) // GRID_W
        per_tile = TILE // GRID_W
        return tuple(t for t in range(NA_K_ROWS // per_tile)
                     if not any(0 <= kr - qr < NA_WIN_ROWS for kr in range(t * per_tile, (t + 1) * per_tile)))

    (out,) = _windowed_attention(
        (nat, nat, nat), (NA_Q0, NA_K0, NA_V0), rpb_pad,
        pl.BlockSpec((HEADS_PER_CALL, 16, TILE), lambda g, bi: (g, 0, 0)),
        grid=(n_groups, b), batch_of=lambda g, bi: bi, chunk_of=lambda g, bi: g, out_chunks=n_groups,
        name="na_attention", mq=m_q, kw=NA_K_ROWS * GRID_W,
        window_start=lambda n: _na_window_start(n * NA_Q_ROWS, rows) * GRID_W, row_chunk=32, with_lse=False,
        split_heads=False, mxu_row_sums=True, interior_dead_tiles=interior_dead_tiles,
        bias_builder=functools.partial(_na_bias_build, rows=rows),
        table_shape=(3, HEADS_PER_CALL * m_q, NA_K_ROWS * GRID_W))
    return out.reshape(b, n_groups, seq, LANES)


def _dil_mask_tables():
    i = np.arange(DIL_Q)[:, None]
    j = np.arange(DIL_K)[None, :]
    tabs = []
    for off in (0, DIL_RADIUS, DIL_K - DIL_Q):
        tabs.append(np.where(np.abs(off + i - j) <= DIL_RADIUS, 0.0, NEG_INF))
    return jnp.asarray(np.stack(tabs), dtype=jnp.float32)


def _dil_attention(arr, mask, chunk0):
    b, _, dil, sub_len, _ = arr.shape
    o, lse = _windowed_attention(
        (arr, arr, arr), (chunk0, chunk0 + 1, chunk0 + 2), mask,
        pl.BlockSpec(mask.shape, lambda bi: (0, 0, 0)),
        grid=(b,), batch_of=lambda bi: bi, chunk_of=lambda bi: 0, out_chunks=1,
        name=f"dil_attention_d{dil}", mq=DIL_Q, kw=DIL_K,
        window_start=lambda n: jnp.clip(n * DIL_Q - DIL_RADIUS, 0, sub_len - DIL_K), row_chunk=DIL_Q, with_lse=True,
        split_heads=True)
    return o.reshape(b, dil, sub_len, LANES), lse.reshape(b, dil, sub_len, LANES)


def _post_kernel(x_ref, yna_ref, o1_ref, o2_ref, o3_ref, l1_ref, l2_ref, l3_ref, p_ref,
                 gmix_ref, gmlp_ref, gple_ref, gfin_ref,
                 win_hbm, wbna_hbm, wbdil_hbm, wout_hbm, wup_hbm, wdown_hbm, wpg_hbm, wpp_hbm,
                 out_ref, *scratch, layer, final_norm, ff_chunk, sub_rows):
    f32, bf16 = jnp.float32, jnp.bfloat16
    bm, d = x_ref.shape
    stage_refs = scratch[:-8]
    wgate_ref, wbna_ref, wbdil_ref, wout_ref, wup_ref, wdown_ref, wpg_ref, wpp_ref = scratch[-8:]
    d_ff = wup_ref.shape[1]

    @pl.when(_is_first_step())
    def _():
        _load_weights_bf16(layer, [(win_hbm, QKV_WIDTH, wgate_ref)])
        _load_weights_bf16(layer, [(wbna_hbm, 0, wbna_ref), (wbdil_hbm, 0, wbdil_ref), (wout_hbm, 0, wout_ref),
                                   (wdown_hbm, 0, wdown_ref), (wpg_hbm, 0, wpg_ref), (wpp_hbm, 0, wpp_ref)])
        _load_weights_bf16(layer, [(wup_hbm, 0, wup_ref)])

    stages = iter(stage_refs)
    token_order = []
    for ref in (o1_ref, o2_ref, o3_ref, l1_ref, l2_ref, l3_ref):
        dil = ref.shape[0]
        if dil == 1:
            token_order.append(lambda r0, ref=ref: ref[0, r0:r0 + sub_rows, :].astype(f32))
            continue
        stage_ref = next(stages)
        for rho in range(dil):
            blk = ref[rho].astype(f32)
            for half in range(LANES // TILE):
                stage_ref[half, pl.ds(rho, bm // dil, stride=dil), :] = blk[:, _tile(half)]
        token_order.append(lambda r0, stage_ref=stage_ref: jnp.concatenate(
            [stage_ref[half, r0:r0 + sub_rows, :] for half in range(LANES // TILE)], axis=1))

    def dot(lhs, w):
        return jnp.dot(lhs, w, preferred_element_type=f32)

    chains = [dict(r0=r0) for r0 in range(0, bm, sub_rows)]
    for c in chains:
        r0 = c["r0"]
        c["x"] = x_ref[r0:r0 + sub_rows, :]
        a = _rms(c["x"], gmix_ref[...]).astype(bf16)
        c["gate_na"] = _sigmoid(dot(a, wgate_ref[:, :d]))
        c["gate_dil"] = _sigmoid(dot(a, wgate_ref[:, d:]))
    for c in chains:
        r0 = c["r0"]
        o1, o2, o3, l1, l2, l3 = [get(r0) for get in token_order]
        mx = jnp.maximum(jnp.maximum(l1, l2), l3)
        e1, e2, e3 = jnp.exp2(l1 - mx), jnp.exp2(l2 - mx), jnp.exp2(l3 - mx)
        ydil = (e1 * o1 + e2 * o2 + e3 * o3) * (1.0 / (e1 + e2 + e3))
        yna = jnp.concatenate([yna_ref[g, r0:r0 + sub_rows, :] for g in range(yna_ref.shape[0])], axis=1)
        c["mixed"] = (c.pop("gate_na") * dot(yna, wbna_ref[...])
                      + c.pop("gate_dil") * dot(ydil.astype(bf16), wbdil_ref[...]))
    for c in chains:
        c["h"] = c.pop("x") + dot(c.pop("mixed").astype(bf16), wout_ref[...])
        c["c"] = _rms(c["h"], gmlp_ref[...]).astype(bf16)
        c["acc"] = jnp.zeros_like(c["h"])
    for f in range(d_ff // ff_chunk):
        for c in chains:
            u = dot(c["c"], wup_ref[:, f * ff_chunk:(f + 1) * ff_chunk])
            c["u"] = jnp.square(jnp.maximum(u, 0.0)).astype(bf16)
        for c in chains:
            c["acc"] = c["acc"] + dot(c.pop("u"), wdown_ref[f * ff_chunk:(f + 1) * ff_chunk, :])
    for c in chains:
        r0 = c["r0"]
        c["h"] = c["h"] + c.pop("acc")
        e = _rms(c["h"], gple_ref[...]).astype(bf16)
        c["pg"] = _sigmoid(dot(e, wpg_ref[...]))
        c["pp"] = dot(p_ref[r0:r0 + sub_rows, :].astype(bf16), wpp_ref[...])
    for c in chains:
        r0 = c["r0"]
        h = c["h"] + c["pg"] * c["pp"]
        if final_norm:
            h = _rms(h, gfin_ref[...])
        out_ref[r0:r0 + sub_rows, :] = h


def _post_block(x3, yna, outs, lses, p3, gains, weights, layer, final_norm, bm=512, sub_rows=256, ff_chunk=1024):
    b, s, d = x3.shape
    w_in = weights[0]
    resident = [(d, w_in.shape[2] - QKV_WIDTH)] + [w.shape[1:] for w in weights[1:]]

    def rows(arr):
        return pl.BlockSpec((None, bm, arr.shape[-1]), lambda bi, t: (bi, t, 0))

    def residues(arr):
        lead = arr.shape[1]
        rows_per = bm if arr is yna else bm // lead
        return pl.BlockSpec((None, lead, rows_per, arr.shape[-1]), lambda bi, t: (bi, 0, t, 0))

    def whole(arr):
        return pl.BlockSpec(arr.shape, lambda bi, t: (0, 0))

    n_stages = sum(a.shape[1] > 1 for a in (*outs, *lses))
    return pl.pallas_call(
        functools.partial(_post_kernel, layer=layer, final_norm=final_norm, ff_chunk=ff_chunk, sub_rows=sub_rows),
        out_shape=jax.ShapeDtypeStruct((b, s, d), jnp.float32),
        grid=(b, s // bm),
        in_specs=([rows(x3), residues(yna)] + [residues(a) for a in (*outs, *lses)] + [rows(p3)]
                  + [whole(g) for g in gains] + [pl.BlockSpec(memory_space=pl.ANY)] * len(weights)),
        out_specs=rows(x3),
        scratch_shapes=([pltpu.VMEM((LANES // TILE, bm, TILE), jnp.float32)] * n_stages
                        + [pltpu.VMEM(shape, jnp.bfloat16) for shape in resident]),
        compiler_params=pltpu.CompilerParams(
            dimension_semantics=("arbitrary",) * 2, vmem_limit_bytes=VMEM_LIMIT),
        name="post_block",
    )(x3, yna, *outs, *lses, p3, *gains, *weights)


def kernel(x, p, positions, g_mix, w_in, rpb, w_branch_na, w_branch_dil, w_out, g_mlp, w_up, w_down,
           g_ple, w_ple_gate, w_ple_proj, g_final):
    b, s, d = x.shape
    depth = w_in.shape[0]
    dil_mask = _dil_mask_tables()
    h = x
    for i in range(depth):
        nat, *dil_arrays = _qkv_project(h, g_mix[i].reshape(1, d), w_in, i, positions)
        yna = _na_attention(nat, rpb[i])
        outs, lses = [], []
        for g, (window, dil) in enumerate(DIL_GROUPS):
            assert window // (2 * dil) == DIL_RADIUS
            if dil == 1:
                o, lse = _dil_attention(nat, dil_mask, DIL_Q0)
            else:
                o, lse = _dil_attention(dil_arrays[g - 1], dil_mask, 0)
            outs.append(o)
            lses.append(lse)
        gains = [g_mix[i].reshape(1, d), g_mlp[i].reshape(1, d), g_ple[i].reshape(1, d), g_final.reshape(1, d)]
        weights = [w_in, w_branch_na, w_branch_dil, w_out, w_up, w_down, w_ple_gate, w_ple_proj]
        h = _post_block(h, yna, outs, lses, p[i], gains, weights, layer=i, final_norm=(i == depth - 1))
    return h
```

```python
import functools

import jax
import jax.numpy as jnp
import numpy as np
from jax import lax
from jax.experimental import pallas as pl
from jax.experimental.pallas import tpu as pltpu

HEAD_DIM = 64
GRID_W = 64
NA_HEADS = 8
NA_WIN_ROWS = 8
NA_WIN_COLS = 16
DIL_GROUPS = ((128, 1), (512, 4), (2048, 16))
DIL_HEADS_PER_GROUP = 4
ROPE_THETA = 10000.0
RMS_EPS = 1e-6
NEG_INF = -1e30
LOG2E = 1.4426950408889634

TILE = 128
LANES = 256
HEADS_PER_CALL = LANES // HEAD_DIM
NA_WIDTH = NA_HEADS * HEAD_DIM
DIL_WIDTH = DIL_HEADS_PER_GROUP * len(DIL_GROUPS) * HEAD_DIM
QKV_WIDTH = 3 * NA_WIDTH + 3 * DIL_WIDTH
N_QKV_CHUNKS = QKV_WIDTH // LANES
NA_Q0, NA_K0, NA_V0 = 0, 2, 4
DIL_Q0, DIL_K0, DIL_V0 = 6, 9, 12

NA_Q_ROWS = 4
NA_K_ROWS = 12
DIL_Q = 128
DIL_K = 256
DIL_RADIUS = 64

VMEM_LIMIT = 56 * 1024 * 1024


def _rms(x, g):
    ms = jnp.mean(x * x, axis=-1, keepdims=True)
    return x * lax.rsqrt(ms + RMS_EPS) * g


def _sigmoid(x):
    return 1.0 / (1.0 + jnp.exp(-x))


def _first_head_of_tile(rows):
    assert HEAD_DIM * 2 == TILE and LANES == 2 * TILE
    return lax.broadcasted_iota(jnp.int32, (rows, TILE), 1) < HEAD_DIM


def _tile(t):
    return slice(t * TILE, (t + 1) * TILE)


def _stack_head_pair(q_tile):
    first = _first_head_of_tile(q_tile.shape[0])
    zero = jnp.zeros_like(q_tile)
    return jnp.concatenate([jnp.where(first, q_tile, zero), jnp.where(first, zero, q_tile)], axis=0)


def _stack_heads(q):
    zero = jnp.zeros((2 * q.shape[0], TILE), q.dtype)
    pairs = [_stack_head_pair(q[:, _tile(t)]) for t in range(LANES // TILE)]
    return jnp.concatenate([jnp.concatenate([pairs[0], zero], axis=1),
                            jnp.concatenate([zero, pairs[1]], axis=1)], axis=0)


def _unstack_heads(x, m):
    first = _first_head_of_tile(m)
    tiles = []
    for t in range(LANES // TILE):
        lanes = _tile(t) if x.shape[1] == LANES else _tile(0)
        tiles.append(jnp.where(first, x[2 * t * m:(2 * t + 1) * m, lanes], x[(2 * t + 1) * m:(2 * t + 2) * m, lanes]))
    return jnp.concatenate(tiles, axis=1)


WEIGHT_STAGE_ELEMS = 1 << 18
WEIGHT_STAGE_SLOTS = 3


def _load_weights_bf16(layer, weights):
    n_cols = weights[0][2].shape[1]
    rc = 1 << ((WEIGHT_STAGE_ELEMS // n_cols).bit_length() - 1)
    rc = min([rc] + [w_vmem.shape[0] for _, _, w_vmem in weights])
    assert rc % 16 == 0
    chunks = []
    for w_hbm, col0, w_vmem in weights:
        assert w_vmem.shape[1] == n_cols and w_vmem.shape[0] % rc == 0
        chunks += [(w_hbm, col0, w_vmem, r0) for r0 in range(0, w_vmem.shape[0], rc)]
    slots = WEIGHT_STAGE_SLOTS

    def body(stage, sems):
        def copy(i):
            w_hbm, col0, _, r0 = chunks[i]
            return pltpu.make_async_copy(w_hbm.at[layer, pl.ds(r0, rc), pl.ds(col0, n_cols)],
                                         stage.at[i % slots], sems.at[i % slots])

        for i in range(min(slots - 1, len(chunks))):
            copy(i).start()
        for i, (_, _, w_vmem, r0) in enumerate(chunks):
            if i + slots - 1 < len(chunks):
                copy(i + slots - 1).start()
            copy(i).wait()
            w_vmem[r0:r0 + rc, :] = stage[i % slots].astype(w_vmem.dtype)

    pl.run_scoped(body, pltpu.VMEM((slots, rc, n_cols), jnp.float32), pltpu.SemaphoreType.DMA((slots,)))


def _is_first_step():
    return (pl.program_id(0) == 0) & (pl.program_id(1) == 0)


def _qkv_kernel(x_ref, g_ref, win_hbm, pos_ref, freq_ref, nat_ref, *rest, layer):
    dil_refs, stage_ref, w_ref = rest[:-2], rest[-2], rest[-1]
    bm = x_ref.shape[0]

    @pl.when(_is_first_step())
    def _():
        _load_weights_bf16(layer, [(win_hbm, 0, w_ref)])

    a = _rms(x_ref[...], g_ref[...]).astype(jnp.bfloat16)
    ang = pos_ref[...] * freq_ref[...]
    cos_t, sin_t = jnp.cos(ang), jnp.sin(ang)
    reps = TILE // cos_t.shape[0]
    cos = jnp.concatenate([cos_t] * reps, axis=0).T
    sin = jnp.concatenate([-sin_t, sin_t] * (reps // 2), axis=0).T
    cos = jnp.concatenate([cos] * (LANES // TILE), axis=1)
    sin = jnp.concatenate([sin] * (LANES // TILE), axis=1)
    first_half = (lax.broadcasted_iota(jnp.int32, cos.shape, 1) % HEAD_DIM) < (HEAD_DIM // 2)
    scale = HEAD_DIM ** -0.5 * LOG2E
    n_groups = len(DIL_GROUPS)
    for c in range(N_QKV_CHUNKS):
        acc = jnp.dot(a, w_ref[:, c * LANES:(c + 1) * LANES], preferred_element_type=jnp.float32)
        if DIL_Q0 <= c < DIL_V0:
            swapped = jnp.where(first_half,
                                pltpu.roll(acc, LANES - HEAD_DIM // 2, axis=1),
                                pltpu.roll(acc, HEAD_DIM // 2, axis=1))
            acc = acc * cos + swapped * sin
        if NA_Q0 <= c < NA_K0 or DIL_Q0 <= c < DIL_K0:
            acc = acc * scale
        group, kind = (c - DIL_Q0) % n_groups, (c - DIL_Q0) // n_groups
        if c < DIL_Q0:
            nat_ref[c, 0] = acc.astype(nat_ref.dtype)
        elif DIL_GROUPS[group][1] == 1:
            nat_ref[DIL_Q0 + kind, 0] = acc.astype(nat_ref.dtype)
        else:
            dil = DIL_GROUPS[group][1]
            o_ref = dil_refs[group - 1]
            for half in range(LANES // TILE):
                stage_ref[half] = acc[:, _tile(half)]
            for rho in range(dil):
                for half in range(LANES // TILE):
                    o_ref[kind, rho, :, _tile(half)] = (
                        stage_ref[half, pl.ds(rho, bm // dil, stride=dil), :].astype(o_ref.dtype))


def _qkv_project(x3, g, w_in, layer, positions, bm=1024):
    b, s, d = x3.shape
    assert DIL_GROUPS[0][1] == 1
    half = HEAD_DIM // 2
    tiles = s // bm
    inv_freq = ROPE_THETA ** (-jnp.arange(half, dtype=jnp.float32) / half)
    pos = positions.reshape(1, b * s).astype(jnp.float32)
    out_shape = [jax.ShapeDtypeStruct((b, DIL_Q0 + 3, 1, s, LANES), jnp.bfloat16)]
    out_specs = [pl.BlockSpec((None, DIL_Q0 + 3, 1, bm, LANES), lambda bi, t: (bi, 0, 0, t, 0))]
    for _, dil in DIL_GROUPS[1:]:
        out_shape.append(jax.ShapeDtypeStruct((b, 3, dil, s // dil, LANES), jnp.bfloat16))
        out_specs.append(pl.BlockSpec((None, 3, dil, bm // dil, LANES), lambda bi, t: (bi, 0, 0, t, 0)))
    return pl.pallas_call(
        functools.partial(_qkv_kernel, layer=layer),
        out_shape=out_shape,
        grid=(b, s // bm),
        in_specs=[
            pl.BlockSpec((None, bm, d), lambda bi, t: (bi, t, 0)),
            pl.BlockSpec((1, d), lambda bi, t: (0, 0)),
            pl.BlockSpec(memory_space=pl.ANY),
            pl.BlockSpec((1, bm), lambda bi, t: (0, bi * tiles + t)),
            pl.BlockSpec((half, 1), lambda bi, t: (0, 0)),
        ],
        out_specs=out_specs,
        scratch_shapes=[pltpu.VMEM((LANES // TILE, bm, TILE), jnp.float32), pltpu.VMEM((d, QKV_WIDTH), jnp.bfloat16)],
        compiler_params=pltpu.CompilerParams(
            dimension_semantics=("arbitrary",) * 2, vmem_limit_bytes=VMEM_LIMIT),
        name="qkv_project",
    )(x3, g, w_in, pos, inv_freq.reshape(half, 1))


def _na_window_start(row0, rows):
    return jnp.clip(row0 - NA_WIN_ROWS // 2, 0, rows - NA_K_ROWS)


def _na_bias_build(rpb_ref, out_ref, *, rows):
    n_off_c = 2 * NA_WIN_COLS - 1
    qc = lax.broadcasted_iota(jnp.int32, (GRID_W, TILE), 0)
    kc = lax.broadcasted_iota(jnp.int32, (GRID_W, TILE), 1)
    cs = jnp.clip(qc - NA_WIN_COLS // 2, 0, GRID_W - NA_WIN_COLS)
    col_valid = (kc >= cs) & (kc < cs + NA_WIN_COLS)
    neg = jnp.full((GRID_W, GRID_W), NEG_INF, jnp.float32)
    for h in range(HEADS_PER_CALL):
        toeplitz = []
        for ro in range(2 * NA_WIN_ROWS - 1):
            row = jnp.broadcast_to(rpb_ref[h, ro:ro + 1, :], (GRID_W, TILE))
            t = pltpu.roll(row, TILE - (n_off_c // 2), axis=1, stride=1, stride_axis=0)
            toeplitz.append(jnp.where(col_valid, t * LOG2E, NEG_INF)[:, :GRID_W])
        for variant, r0 in enumerate((0, NA_Q_ROWS, rows - NA_Q_ROWS)):
            w0 = int(np.clip(r0 - NA_WIN_ROWS // 2, 0, rows - NA_K_ROWS))
            for qr in range(NA_Q_ROWS):
                r = r0 + qr
                rs = int(np.clip(r - NA_WIN_ROWS // 2, 0, rows - NA_WIN_ROWS))
                blocks = [toeplitz[w0 + kr - r + NA_WIN_ROWS - 1] if rs <= w0 + kr < rs + NA_WIN_ROWS else neg
                          for kr in range(NA_K_ROWS)]
                row0 = (h * NA_Q_ROWS + qr) * GRID_W
                out_ref[variant, row0:row0 + GRID_W, :] = jnp.concatenate(blocks, axis=1)


class _Sequences:
    def __init__(self, ref):
        self.ref = ref
        self.per_batch = ref.shape[1]
        assert self.per_batch & (self.per_batch - 1) == 0
        self.shape = (ref.shape[0] * ref.shape[1],) + tuple(ref.shape[2:])
        self.dtype = ref.dtype

    def _index(self, idx):
        seq, rows, lanes = idx
        shift = self.per_batch.bit_length() - 1
        return lax.shift_right_logical(seq, shift), seq & (self.per_batch - 1), rows, lanes

    def __getitem__(self, idx):
        return self.ref[self._index(idx)]

    def __setitem__(self, idx, value):
        self.ref[self._index(idx)] = value


def _attn_kernel(q_ref, k_ref, v_ref, bias_ref, o_ref, *rest, mq, kw, blocks_per_seq, window_start, row_chunk,
                 with_lse, interior_dead_tiles, split_heads, mxu_row_sums, bias_builder):
    f32 = jnp.float32
    if with_lse:
        lse_ref, *scratch = rest
        lse_ref = _Sequences(lse_ref)
    else:
        scratch = rest
    q_ref, k_ref, v_ref, o_ref = (_Sequences(r) for r in (q_ref, k_ref, v_ref, o_ref))
    s_refs, p_refs, st_refs = scratch[0:2], scratch[2:4], scratch[4:6]
    if bias_builder is not None:
        table_ref = scratch[-1]
        pl.when(pl.program_id(1) == 0)(functools.partial(bias_builder, bias_ref, table_ref))
        bias_ref = table_ref
    ST_MAX, ST_SUM, ST_MAX_KEPT = 0, 1, 2
    n_blocks = q_ref.shape[0] * blocks_per_seq
    rows_s = HEADS_PER_CALL * mq
    rows_b = bias_ref.shape[1]
    log2_bps = blocks_per_seq.bit_length() - 1
    assert blocks_per_seq == 1 << log2_bps and n_blocks % 2 == 0 and n_blocks >= 2
    assert rows_s % rows_b == 0 and kw % TILE == 0

    def locate(j):
        j = jnp.int32(j)
        seq = lax.shift_right_logical(j, log2_bps)
        n = j & (blocks_per_seq - 1)
        q0 = pl.multiple_of(n * mq, mq)
        w0 = pl.multiple_of(window_start(n), 64)
        variant = jnp.where(n == 0, 0, jnp.where(n == blocks_per_seq - 1, 2, 1))
        return seq, q0, w0, variant

    def lanes_of(x, width):
        return jnp.concatenate([x] * (width // TILE), axis=1)

    def stage_a(j, par):
        seq, q0, w0, variant = locate(j)
        bias = jnp.concatenate([bias_ref[variant]] * (rows_s // rows_b), axis=0)
        nt = (((1,), (1,)), ((), ()))
        if split_heads:
            for t in range(LANES // TILE):
                qs = _stack_head_pair(q_ref[seq, pl.ds(q0, mq), _tile(t)])
                s = lax.dot_general(qs, k_ref[seq, pl.ds(w0, kw), _tile(t)], nt, preferred_element_type=f32)
                rows = slice(2 * t * mq, (2 * t + 2) * mq)
                s_refs[par][rows, :] = s + bias[rows]
        else:
            qs = _stack_heads(q_ref[seq, pl.ds(q0, mq), :])
            s = lax.dot_general(qs, k_ref[seq, pl.ds(w0, kw), :], nt, preferred_element_type=f32)
            s_refs[par][...] = s + bias
        for r0 in range(0, rows_s, row_chunk):
            mx = jnp.max(s_refs[par][r0:r0 + row_chunk, :], axis=-1, keepdims=True)
            st_refs[par][ST_MAX, r0:r0 + row_chunk, :] = jnp.broadcast_to(mx, (row_chunk, TILE))

    def stage_b(par, p_ref, dead_tiles):
        for r0 in range(0, rows_s, row_chunk):
            mx = st_refs[par][ST_MAX, r0:r0 + row_chunk, :]
            live = [t for t in range(kw // TILE) if t not in dead_tiles(r0)]
            s = jnp.concatenate([s_refs[par][r0:r0 + row_chunk, _tile(t)] for t in live], axis=1)
            p = jnp.exp2(s - lanes_of(mx, s.shape[1]))
            if not mxu_row_sums:
                l = jnp.sum(p, axis=-1, keepdims=True)
                st_refs[par][ST_SUM, r0:r0 + row_chunk, :] = jnp.broadcast_to(l, (row_chunk, TILE))
                if with_lse:
                    st_refs[par][ST_MAX_KEPT, r0:r0 + row_chunk, :] = mx
            p = p.astype(p_ref.dtype)
            for i, t in enumerate(live):
                p_ref[r0:r0 + row_chunk, _tile(t)] = p[:, _tile(i)]

    def stage_c(j, par, p_ref):
        seq, q0, w0, _ = locate(j)
        if mxu_row_sums:
            first = _first_head_of_tile(mq)
            ones = jnp.ones((kw, TILE), p_ref.dtype)
            tiles = []
            for t in range(LANES // TILE):
                rhs = jnp.concatenate([v_ref[seq, pl.ds(w0, kw), _tile(t)], ones], axis=1)
                pv = jnp.dot(p_ref[2 * t * mq:(2 * t + 2) * mq, :], rhs, preferred_element_type=f32)
                den = jnp.where(first, pv[:mq, TILE:], pv[mq:, TILE:])
                tiles.append(jnp.where(first, pv[:mq, :TILE], pv[mq:, :TILE]) * (1.0 / den))
            o_ref[seq, pl.ds(q0, mq), :] = jnp.concatenate(tiles, axis=1).astype(o_ref.dtype)
            return
        if split_heads:
            first = _first_head_of_tile(mq)
            tiles = []
            for t in range(LANES // TILE):
                vwin = v_ref[seq, pl.ds(w0, kw), _tile(t)]
                pv = jnp.dot(p_ref[2 * t * mq:(2 * t + 2) * mq, :], vwin, preferred_element_type=f32)
                tiles.append(jnp.where(first, pv[:mq], pv[mq:]))
            out = jnp.concatenate(tiles, axis=1)
        else:
            pv = jnp.dot(p_ref[...], v_ref[seq, pl.ds(w0, kw), :], preferred_element_type=f32)
            out = _unstack_heads(pv, mq)
        l = _unstack_heads(st_refs[par][ST_SUM], mq)
        o_ref[seq, pl.ds(q0, mq), :] = (out * (1.0 / l)).astype(o_ref.dtype)
        if with_lse:
            lse_ref[seq, pl.ds(q0, mq), :] = _unstack_heads(st_refs[par][ST_MAX_KEPT], mq) + jnp.log2(l)

    def no_dead_tiles(r0):
        return ()

    def steady_state(first_pair, dead_tiles):
        def body(i, carry):
            for par in (0, 1):
                j = 2 * i + par
                if mxu_row_sums:
                    stage_c(j - 2, par, p_refs[par])
                stage_a(j, par)
                stage_b(1 - par, p_refs[1 - par], dead_tiles)
                if not mxu_row_sums:
                    stage_c(j - 2, par, p_refs[par])
            return carry

        lax.fori_loop(first_pair, n_blocks // 2, body, 0)

    if interior_dead_tiles is None:
        stage_a(0, 0)
        stage_a(1, 1)
        stage_b(0, p_refs[0], no_dead_tiles)
        steady_state(1, no_dead_tiles)
        stage_b(1, p_refs[1], no_dead_tiles)
        stage_c(n_blocks - 2, 0, p_refs[0])
        stage_c(n_blocks - 1, 1, p_refs[1])
    else:
        p_edge = scratch[6]
        assert q_ref.shape[0] == 1 and n_blocks >= 6
        for p_ref in p_refs:
            for r0 in range(0, rows_s, row_chunk):
                for t in interior_dead_tiles(r0):
                    p_ref[r0:r0 + row_chunk, _tile(t)] = jnp.zeros((row_chunk, TILE), p_ref.dtype)
        stage_a(0, 0)
        stage_a(1, 1)
        stage_b(0, p_edge, no_dead_tiles)
        stage_a(2, 0)
        stage_b(1, p_refs[1], interior_dead_tiles)
        stage_c(0, 0, p_edge)
        stage_a(3, 1)
        stage_b(0, p_refs[0], interior_dead_tiles)
        stage_c(1, 1, p_refs[1])
        steady_state(2, interior_dead_tiles)
        stage_b(1, p_edge, no_dead_tiles)
        stage_c(n_blocks - 2, 0, p_refs[0])
        stage_c(n_blocks - 1, 1, p_edge)


def _windowed_attention(arrays, chunks, bias, bias_spec, grid, batch_of, chunk_of, out_chunks, *, name,
                        mq, kw, window_start, row_chunk, with_lse, split_heads, mxu_row_sums=False,
                        interior_dead_tiles=None, bias_builder=None, table_shape=None, batch_block=1):
    b, _, r, sub_len, _ = arrays[0].shape
    blocks_per_seq = sub_len // mq

    def spec(chunk):
        return pl.BlockSpec((batch_block, None, r, sub_len, LANES),
                            lambda *g: (batch_of(*g), chunk + chunk_of(*g), 0, 0, 0))

    out_shape = [jax.ShapeDtypeStruct((b, out_chunks, r, sub_len, LANES), jnp.bfloat16)]
    out_specs = [spec(0)]
    if with_lse:
        out_shape.append(jax.ShapeDtypeStruct((b, out_chunks, r, sub_len, LANES), jnp.float32))
        out_specs.append(spec(0))
    rows_s = HEADS_PER_CALL * mq
    scratch = ([pltpu.VMEM((rows_s, kw), jnp.float32)] * 2 + [pltpu.VMEM((rows_s, kw), jnp.bfloat16)] * 2
               + [pltpu.VMEM((3 if with_lse else 2, rows_s, TILE), jnp.float32)] * 2)
    if interior_dead_tiles is not None:
        scratch.append(pltpu.VMEM((rows_s, kw), jnp.bfloat16))
    if bias_builder is not None:
        scratch.append(pltpu.VMEM(table_shape, jnp.float32))
    return pl.pallas_call(
        functools.partial(_attn_kernel, mq=mq, kw=kw, blocks_per_seq=blocks_per_seq, window_start=window_start,
                          row_chunk=row_chunk, with_lse=with_lse, interior_dead_tiles=interior_dead_tiles,
                          split_heads=split_heads, mxu_row_sums=mxu_row_sums, bias_builder=bias_builder),
        out_shape=out_shape,
        grid=grid,
        in_specs=[spec(c) for c in chunks] + [bias_spec],
        out_specs=out_specs,
        scratch_shapes=scratch,
        compiler_params=pltpu.CompilerParams(
            dimension_semantics=("arbitrary",) * len(grid), vmem_limit_bytes=VMEM_LIMIT),
        name=name,
    )(*arrays, bias)


def _na_attention(nat, rpb):
    b, _, _, seq, _ = nat.shape
    rows = seq // GRID_W
    n_groups = NA_HEADS // HEADS_PER_CALL
    m_q = NA_Q_ROWS * GRID_W
    _, n_ro, n_co = rpb.shape
    rpb_pad = jnp.pad(rpb.astype(jnp.float32), ((0, 0), (0, 16 - n_ro), (0, TILE - n_co)))
    for blk in range(1, rows // NA_Q_ROWS - 1):
        r0 = blk * NA_Q_ROWS
        assert 0 <= r0 - NA_WIN_ROWS // 2 <= rows - NA_K_ROWS
        assert r0 + NA_Q_ROWS - 1 - NA_WIN_ROWS // 2 <= rows - NA_WIN_ROWS

    def interior_dead_tiles(row):
        qr = (row % m_q) // GRID_W
        per_tile = TILE // GRID_W
        return tuple(t for t in range(NA_K_ROWS // per_tile)
                     if not any(0 <= kr - qr < NA_WIN_ROWS for kr in range(t * per_tile, (t + 1) * per_tile)))

    (out,) = _windowed_attention(
        (nat, nat, nat), (NA_Q0, NA_K0, NA_V0), rpb_pad,
        pl.BlockSpec((HEADS_PER_CALL, 16, TILE), lambda g, bi: (g, 0, 0)),
        grid=(n_groups, b), batch_of=lambda g, bi: bi, chunk_of=lambda g, bi: g, out_chunks=n_groups,
        name="na_attention", mq=m_q, kw=NA_K_ROWS * GRID_W,
        window_start=lambda n: _na_window_start(n * NA_Q_ROWS, rows) * GRID_W, row_chunk=32, with_lse=False,
        split_heads=False, mxu_row_sums=True, interior_dead_tiles=interior_dead_tiles,
        bias_builder=functools.partial(_na_bias_build, rows=rows),
        table_shape=(3, HEADS_PER_CALL * m_q, NA_K_ROWS * GRID_W))
    return out.reshape(b, n_groups, seq, LANES)


def _dil_mask_tables():
    i = np.arange(DIL_Q)[:, None]
    j = np.arange(DIL_K)[None, :]
    tabs = []
    for off in (0, DIL_RADIUS, DIL_K - DIL_Q):
        tabs.append(np.where(np.abs(off + i - j) <= DIL_RADIUS, 0.0, NEG_INF))
    return jnp.asarray(np.stack(tabs), dtype=jnp.float32)


def _dil_attention(arr, mask, chunk0):
    b, _, dil, sub_len, _ = arr.shape
    per_step = 2 if b % 2 == 0 else 1
    o, lse = _windowed_attention(
        (arr, arr, arr), (chunk0, chunk0 + 1, chunk0 + 2), mask,
        pl.BlockSpec(mask.shape, lambda bi: (0, 0, 0)),
        grid=(b // per_step,), batch_of=lambda bi: bi, chunk_of=lambda bi: 0, out_chunks=1,
        name=f"dil_attention_d{dil}", mq=DIL_Q, kw=DIL_K,
        window_start=lambda n: jnp.clip(n * DIL_Q - DIL_RADIUS, 0, sub_len - DIL_K), row_chunk=DIL_Q, with_lse=True,
        split_heads=True, batch_block=per_step)
    return o.reshape(b, dil, sub_len, LANES), lse.reshape(b, dil, sub_len, LANES)


def _post_kernel(x_ref, yna_ref, o1_ref, o2_ref, o3_ref, l1_ref, l2_ref, l3_ref, p_ref,
                 gmix_ref, gmlp_ref, gple_ref, gfin_ref,
                 win_hbm, wbna_hbm, wbdil_hbm, wout_hbm, wup_hbm, wdown_hbm, wpg_hbm, wpp_hbm,
                 out_ref, *scratch, layer, final_norm, ff_chunk, sub_rows):
    f32, bf16 = jnp.float32, jnp.bfloat16
    bm, d = x_ref.shape
    stage_refs = scratch[:-8]
    wgate_ref, wbna_ref, wbdil_ref, wout_ref, wup_ref, wdown_ref, wpg_ref, wpp_ref = scratch[-8:]
    d_ff = wup_ref.shape[1]

    @pl.when(_is_first_step())
    def _():
        _load_weights_bf16(layer, [(win_hbm, QKV_WIDTH, wgate_ref)])
        _load_weights_bf16(layer, [(wbna_hbm, 0, wbna_ref), (wbdil_hbm, 0, wbdil_ref), (wout_hbm, 0, wout_ref),
                                   (wdown_hbm, 0, wdown_ref), (wpg_hbm, 0, wpg_ref), (wpp_hbm, 0, wpp_ref)])
        _load_weights_bf16(layer, [(wup_hbm, 0, wup_ref)])

    stages = iter(stage_refs)
    token_order = []
    for ref in (o1_ref, o2_ref, o3_ref, l1_ref, l2_ref, l3_ref):
        dil = ref.shape[0]
        if dil == 1:
            token_order.append(lambda r0, ref=ref: ref[0, r0:r0 + sub_rows, :].astype(f32))
            continue
        stage_ref = next(stages)
        for rho in range(dil):
            blk = ref[rho].astype(f32)
            for half in range(LANES // TILE):
                stage_ref[half, pl.ds(rho, bm // dil, stride=dil), :] = blk[:, _tile(half)]
        token_order.append(lambda r0, stage_ref=stage_ref: jnp.concatenate(
            [stage_ref[half, r0:r0 + sub_rows, :] for half in range(LANES // TILE)], axis=1))

    def dot(lhs, w):
        return jnp.dot(lhs, w, preferred_element_type=f32)

    chains = [dict(r0=r0) for r0 in range(0, bm, sub_rows)]
    for c in chains:
        r0 = c["r0"]
        c["x"] = x_ref[r0:r0 + sub_rows, :]
        a = _rms(c["x"], gmix_ref[...]).astype(bf16)
        c["gate_na"] = _sigmoid(dot(a, wgate_ref[:, :d]))
        c["gate_dil"] = _sigmoid(dot(a, wgate_ref[:, d:]))
    for c in chains:
        r0 = c["r0"]
        o1, o2, o3, l1, l2, l3 = [get(r0) for get in token_order]
        mx = jnp.maximum(jnp.maximum(l1, l2), l3)
        e1, e2, e3 = jnp.exp2(l1 - mx), jnp.exp2(l2 - mx), jnp.exp2(l3 - mx)
        ydil = (e1 * o1 + e2 * o2 + e3 * o3) * (1.0 / (e1 + e2 + e3))
        yna = jnp.concatenate([yna_ref[g, r0:r0 + sub_rows, :] for g in range(yna_ref.shape[0])], axis=1)
        c["mixed"] = (c.pop("gate_na") * dot(yna, wbna_ref[...])
                      + c.pop("gate_dil") * dot(ydil.astype(bf16), wbdil_ref[...]))
    for c in chains:
        c["h"] = c.pop("x") + dot(c.pop("mixed").astype(bf16), wout_ref[...])
        c["c"] = _rms(c["h"], gmlp_ref[...]).astype(bf16)
        c["acc"] = jnp.zeros_like(c["h"])
    for f in range(d_ff // ff_chunk):
        for c in chains:
            u = dot(c["c"], wup_ref[:, f * ff_chunk:(f + 1) * ff_chunk])
            c["u"] = jnp.square(jnp.maximum(u, 0.0)).astype(bf16)
        for c in chains:
            c["acc"] = c["acc"] + dot(c.pop("u"), wdown_ref[f * ff_chunk:(f + 1) * ff_chunk, :])
    for c in chains:
        r0 = c["r0"]
        c["h"] = c["h"] + c.pop("acc")
        e = _rms(c["h"], gple_ref[...]).astype(bf16)
        c["pg"] = _sigmoid(dot(e, wpg_ref[...]))
        c["pp"] = dot(p_ref[r0:r0 + sub_rows, :].astype(bf16), wpp_ref[...])
    for c in chains:
        r0 = c["r0"]
        h = c["h"] + c["pg"] * c["pp"]
        if final_norm:
            h = _rms(h, gfin_ref[...])
        out_ref[r0:r0 + sub_rows, :] = h


def _post_block(x3, yna, outs, lses, p3, gains, weights, layer, final_norm, bm=512, sub_rows=256, ff_chunk=1024):
    b, s, d = x3.shape
    w_in = weights[0]
    resident = [(d, w_in.shape[2] - QKV_WIDTH)] + [w.shape[1:] for w in weights[1:]]

    def rows(arr):
        return pl.BlockSpec((None, bm, arr.shape[-1]), lambda bi, t: (bi, t, 0))

    def residues(arr):
        lead = arr.shape[1]
        rows_per = bm if arr is yna else bm // lead
        return pl.BlockSpec((None, lead, rows_per, arr.shape[-1]), lambda bi, t: (bi, 0, t, 0))

    def whole(arr):
        return pl.BlockSpec(arr.shape, lambda bi, t: (0, 0))

    n_stages = sum(a.shape[1] > 1 for a in (*outs, *lses))
    return pl.pallas_call(
        functools.partial(_post_kernel, layer=layer, final_norm=final_norm, ff_chunk=ff_chunk, sub_rows=sub_rows),
        out_shape=jax.ShapeDtypeStruct((b, s, d), jnp.float32),
        grid=(b, s // bm),
        in_specs=([rows(x3), residues(yna)] + [residues(a) for a in (*outs, *lses)] + [rows(p3)]
                  + [whole(g) for g in gains] + [pl.BlockSpec(memory_space=pl.ANY)] * len(weights)),
        out_specs=rows(x3),
        scratch_shapes=([pltpu.VMEM((LANES // TILE, bm, TILE), jnp.float32)] * n_stages
                        + [pltpu.VMEM(shape, jnp.bfloat16) for shape in resident]),
        compiler_params=pltpu.CompilerParams(
            dimension_semantics=("arbitrary",) * 2, vmem_limit_bytes=VMEM_LIMIT),
        name="post_block",
    )(x3, yna, *outs, *lses, p3, *gains, *weights)


def kernel(x, p, positions, g_mix, w_in, rpb, w_branch_na, w_branch_dil, w_out, g_mlp, w_up, w_down,
           g_ple, w_ple_gate, w_ple_proj, g_final):
    b, s, d = x.shape
    depth = w_in.shape[0]
    dil_mask = _dil_mask_tables()
    h = x
    for i in range(depth):
        nat, *dil_arrays = _qkv_project(h, g_mix[i].reshape(1, d), w_in, i, positions)
        yna = _na_attention(nat, rpb[i])
        outs, lses = [], []
        for g, (window, dil) in enumerate(DIL_GROUPS):
            assert window // (2 * dil) == DIL_RADIUS
            if dil == 1:
                o, lse = _dil_attention(nat, dil_mask, DIL_Q0)
            else:
                o, lse = _dil_attention(dil_arrays[g - 1], dil_mask, 0)
            outs.append(o)
            lses.append(lse)
        gains = [g_mix[i].reshape(1, d), g_mlp[i].reshape(1, d), g_ple[i].reshape(1, d), g_final.reshape(1, d)]
        weights = [w_in, w_branch_na, w_branch_dil, w_out, w_up, w_down, w_ple_gate, w_ple_proj]
        h = _post_block(h, yna, outs, lses, p[i], gains, weights, layer=i, final_norm=(i == depth - 1))
    return h
```

```python
import functools

import jax
import jax.numpy as jnp
import numpy as np
from jax import lax
from jax.experimental import pallas as pl
from jax.experimental.pallas import tpu as pltpu

HEAD_DIM = 64
GRID_W = 64
NA_HEADS = 8
NA_WIN_ROWS = 8
NA_WIN_COLS = 16
DIL_GROUPS = ((128, 1), (512, 4), (2048, 16))
DIL_HEADS_PER_GROUP = 4
ROPE_THETA = 10000.0
RMS_EPS = 1e-6
NEG_INF = -1e30
LOG2E = 1.4426950408889634

TILE = 128
LANES = 256
HEADS_PER_CALL = LANES // HEAD_DIM
NA_WIDTH = NA_HEADS * HEAD_DIM
DIL_WIDTH = DIL_HEADS_PER_GROUP * len(DIL_GROUPS) * HEAD_DIM
QKV_WIDTH = 3 * NA_WIDTH + 3 * DIL_WIDTH
N_QKV_CHUNKS = QKV_WIDTH // LANES
NA_Q0, NA_K0, NA_V0 = 0, 2, 4
DIL_Q0, DIL_K0, DIL_V0 = 6, 9, 12

NA_Q_ROWS = 4
NA_K_ROWS = 12
DIL_Q = 128
DIL_K = 256
DIL_RADIUS = 64

VMEM_LIMIT = 56 * 1024 * 1024


def _rms(x, g):
    ms = jnp.mean(x * x, axis=-1, keepdims=True)
    return x * lax.rsqrt(ms + RMS_EPS) * g


def _sigmoid(x):
    return 1.0 / (1.0 + jnp.exp(-x))


def _first_head_of_tile(rows):
    assert HEAD_DIM * 2 == TILE and LANES == 2 * TILE
    return lax.broadcasted_iota(jnp.int32, (rows, TILE), 1) < HEAD_DIM


def _tile(t):
    return slice(t * TILE, (t + 1) * TILE)


def _stack_head_pair(q_tile):
    first = _first_head_of_tile(q_tile.shape[0])
    zero = jnp.zeros_like(q_tile)
    return jnp.concatenate([jnp.where(first, q_tile, zero), jnp.where(first, zero, q_tile)], axis=0)


def _stack_heads(q):
    zero = jnp.zeros((2 * q.shape[0], TILE), q.dtype)
    pairs = [_stack_head_pair(q[:, _tile(t)]) for t in range(LANES // TILE)]
    return jnp.concatenate([jnp.concatenate([pairs[0], zero], axis=1),
                            jnp.concatenate([zero, pairs[1]], axis=1)], axis=0)


def _quarter_mask(rows):
    return (lax.broadcasted_iota(jnp.int32, (rows, TILE), 1) % HEAD_DIM) < HEAD_DIM // 2


def _compact_heads(x):
    return jnp.where(_quarter_mask(x.shape[0]), x[:, _tile(0)], x[:, _tile(1)])


def _expand_heads(c):
    first = _quarter_mask(c.shape[0])
    return jnp.concatenate([jnp.where(first, c, pltpu.roll(c, HEAD_DIM // 2, axis=1)),
                            jnp.where(first, pltpu.roll(c, TILE - HEAD_DIM // 2, axis=1), c)], axis=1)


def _unstack_heads(x, m):
    first = _first_head_of_tile(m)
    tiles = []
    for t in range(LANES // TILE):
        lanes = _tile(t) if x.shape[1] == LANES else _tile(0)
        tiles.append(jnp.where(first, x[2 * t * m:(2 * t + 1) * m, lanes], x[(2 * t + 1) * m:(2 * t + 2) * m, lanes]))
    return jnp.concatenate(tiles, axis=1)


WEIGHT_STAGE_ELEMS = 1 << 18
WEIGHT_STAGE_SLOTS = 3


def _load_weights_bf16(layer, weights):
    n_cols = weights[0][2].shape[1]
    rc = 1 << ((WEIGHT_STAGE_ELEMS // n_cols).bit_length() - 1)
    rc = min([rc] + [w_vmem.shape[0] for _, _, w_vmem in weights])
    assert rc % 16 == 0
    chunks = []
    for w_hbm, col0, w_vmem in weights:
        assert w_vmem.shape[1] == n_cols and w_vmem.shape[0] % rc == 0
        chunks += [(w_hbm, col0, w_vmem, r0) for r0 in range(0, w_vmem.shape[0], rc)]
    slots = WEIGHT_STAGE_SLOTS

    def body(stage, sems):
        def copy(i):
            w_hbm, col0, _, r0 = chunks[i]
            return pltpu.make_async_copy(w_hbm.at[layer, pl.ds(r0, rc), pl.ds(col0, n_cols)],
                                         stage.at[i % slots], sems.at[i % slots])

        for i in range(min(slots - 1, len(chunks))):
            copy(i).start()
        for i, (_, _, w_vmem, r0) in enumerate(chunks):
            if i + slots - 1 < len(chunks):
                copy(i + slots - 1).start()
            copy(i).wait()
            w_vmem[r0:r0 + rc, :] = stage[i % slots].astype(w_vmem.dtype)

    pl.run_scoped(body, pltpu.VMEM((slots, rc, n_cols), jnp.float32), pltpu.SemaphoreType.DMA((slots,)))


def _is_first_step():
    return (pl.program_id(0) == 0) & (pl.program_id(1) == 0)


def _qkv_kernel(x_ref, g_ref, win_hbm, pos_ref, freq_ref, nat_ref, *rest, layer):
    dil_refs, stage_ref, w_ref = rest[:-2], rest[-2], rest[-1]
    bm = x_ref.shape[0]

    @pl.when(_is_first_step())
    def _():
        _load_weights_bf16(layer, [(win_hbm, 0, w_ref)])

    a = _rms(x_ref[...], g_ref[...]).astype(jnp.bfloat16)
    ang = pos_ref[...] * freq_ref[...]
    cos_t, sin_t = jnp.cos(ang), jnp.sin(ang)
    reps = TILE // cos_t.shape[0]
    cos = jnp.concatenate([cos_t] * reps, axis=0).T
    sin = jnp.concatenate([-sin_t, sin_t] * (reps // 2), axis=0).T
    cos = jnp.concatenate([cos] * (LANES // TILE), axis=1)
    sin = jnp.concatenate([sin] * (LANES // TILE), axis=1)
    first_half = (lax.broadcasted_iota(jnp.int32, cos.shape, 1) % HEAD_DIM) < (HEAD_DIM // 2)
    scale = HEAD_DIM ** -0.5 * LOG2E
    n_groups = len(DIL_GROUPS)
    for c in range(N_QKV_CHUNKS):
        acc = jnp.dot(a, w_ref[:, c * LANES:(c + 1) * LANES], preferred_element_type=jnp.float32)
        if DIL_Q0 <= c < DIL_V0:
            swapped = jnp.where(first_half,
                                pltpu.roll(acc, LANES - HEAD_DIM // 2, axis=1),
                                pltpu.roll(acc, HEAD_DIM // 2, axis=1))
            acc = acc * cos + swapped * sin
        if NA_Q0 <= c < NA_K0 or DIL_Q0 <= c < DIL_K0:
            acc = acc * scale
        group, kind = (c - DIL_Q0) % n_groups, (c - DIL_Q0) // n_groups
        if c < DIL_Q0:
            nat_ref[c, 0] = acc.astype(nat_ref.dtype)
        elif DIL_GROUPS[group][1] == 1:
            nat_ref[DIL_Q0 + kind, 0] = acc.astype(nat_ref.dtype)
        else:
            dil = DIL_GROUPS[group][1]
            o_ref = dil_refs[group - 1]
            for half in range(LANES // TILE):
                stage_ref[half] = acc[:, _tile(half)]
            for rho in range(dil):
                for half in range(LANES // TILE):
                    o_ref[kind, rho, :, _tile(half)] = (
                        stage_ref[half, pl.ds(rho, bm // dil, stride=dil), :].astype(o_ref.dtype))


def _qkv_project(x3, g, w_in, layer, positions, bm=1024):
    b, s, d = x3.shape
    assert DIL_GROUPS[0][1] == 1
    half = HEAD_DIM // 2
    tiles = s // bm
    inv_freq = ROPE_THETA ** (-jnp.arange(half, dtype=jnp.float32) / half)
    pos = positions.reshape(1, b * s).astype(jnp.float32)
    out_shape = [jax.ShapeDtypeStruct((b, DIL_Q0 + 3, 1, s, LANES), jnp.bfloat16)]
    out_specs = [pl.BlockSpec((None, DIL_Q0 + 3, 1, bm, LANES), lambda bi, t: (bi, 0, 0, t, 0))]
    for _, dil in DIL_GROUPS[1:]:
        out_shape.append(jax.ShapeDtypeStruct((b, 3, dil, s // dil, LANES), jnp.bfloat16))
        out_specs.append(pl.BlockSpec((None, 3, dil, bm // dil, LANES), lambda bi, t: (bi, 0, 0, t, 0)))
    return pl.pallas_call(
        functools.partial(_qkv_kernel, layer=layer),
        out_shape=out_shape,
        grid=(b, s // bm),
        in_specs=[
            pl.BlockSpec((None, bm, d), lambda bi, t: (bi, t, 0)),
            pl.BlockSpec((1, d), lambda bi, t: (0, 0)),
            pl.BlockSpec(memory_space=pl.ANY),
            pl.BlockSpec((1, bm), lambda bi, t: (0, bi * tiles + t)),
            pl.BlockSpec((half, 1), lambda bi, t: (0, 0)),
        ],
        out_specs=out_specs,
        scratch_shapes=[pltpu.VMEM((LANES // TILE, bm, TILE), jnp.float32), pltpu.VMEM((d, QKV_WIDTH), jnp.bfloat16)],
        compiler_params=pltpu.CompilerParams(
            dimension_semantics=("arbitrary",) * 2, vmem_limit_bytes=VMEM_LIMIT),
        name="qkv_project",
    )(x3, g, w_in, pos, inv_freq.reshape(half, 1))


def _na_window_start(row0, rows):
    return jnp.clip(row0 - NA_WIN_ROWS // 2, 0, rows - NA_K_ROWS)


def _na_bias_build(rpb_ref, out_ref, *, rows):
    n_off_c = 2 * NA_WIN_COLS - 1
    qc = lax.broadcasted_iota(jnp.int32, (GRID_W, TILE), 0)
    kc = lax.broadcasted_iota(jnp.int32, (GRID_W, TILE), 1)
    cs = jnp.clip(qc - NA_WIN_COLS // 2, 0, GRID_W - NA_WIN_COLS)
    col_valid = (kc >= cs) & (kc < cs + NA_WIN_COLS)
    neg = jnp.full((GRID_W, GRID_W), NEG_INF, jnp.float32)
    for h in range(HEADS_PER_CALL):
        toeplitz = []
        for ro in range(2 * NA_WIN_ROWS - 1):
            row = jnp.broadcast_to(rpb_ref[h, ro:ro + 1, :], (GRID_W, TILE))
            t = pltpu.roll(row, TILE - (n_off_c // 2), axis=1, stride=1, stride_axis=0)
            toeplitz.append(jnp.where(col_valid, t * LOG2E, NEG_INF)[:, :GRID_W])
        for variant, r0 in enumerate((0, NA_Q_ROWS, rows - NA_Q_ROWS)):
            w0 = int(np.clip(r0 - NA_WIN_ROWS // 2, 0, rows - NA_K_ROWS))
            for qr in range(NA_Q_ROWS):
                r = r0 + qr
                rs = int(np.clip(r - NA_WIN_ROWS // 2, 0, rows - NA_WIN_ROWS))
                blocks = [toeplitz[w0 + kr - r + NA_WIN_ROWS - 1] if rs <= w0 + kr < rs + NA_WIN_ROWS else neg
                          for kr in range(NA_K_ROWS)]
                row0 = (h * NA_Q_ROWS + qr) * GRID_W
                out_ref[variant, row0:row0 + GRID_W, :] = jnp.concatenate(blocks, axis=1)


def _attn_kernel(q_ref, k_ref, v_ref, bias_ref, o_ref, *rest, mq, kw, blocks_per_seq, window_start, row_chunk,
                 with_lse, interior_dead_tiles, split_heads, mxu_row_sums, bias_builder):
    f32 = jnp.float32
    if with_lse:
        lse_ref, *scratch = rest
    else:
        scratch = rest
    s_refs, p_refs, st_refs = scratch[0:2], scratch[2:4], scratch[4:6]
    if bias_builder is not None:
        table_ref = scratch[-1]
        pl.when(pl.program_id(1) == 0)(functools.partial(bias_builder, bias_ref, table_ref))
        bias_ref = table_ref
    ST_MAX, ST_SUM, ST_MAX_KEPT = 0, 1, 2
    n_blocks = q_ref.shape[0] * blocks_per_seq
    rows_s = HEADS_PER_CALL * mq
    rows_b = bias_ref.shape[1]
    log2_bps = blocks_per_seq.bit_length() - 1
    assert blocks_per_seq == 1 << log2_bps and n_blocks % 2 == 0 and n_blocks >= 2
    assert rows_s % rows_b == 0 and kw % TILE == 0

    def locate(j):
        j = jnp.int32(j)
        seq = lax.shift_right_logical(j, log2_bps)
        n = j & (blocks_per_seq - 1)
        q0 = pl.multiple_of(n * mq, mq)
        w0 = pl.multiple_of(window_start(n), 64)
        variant = jnp.where(n == 0, 0, jnp.where(n == blocks_per_seq - 1, 2, 1))
        return seq, q0, w0, variant

    def lanes_of(x, width):
        return jnp.concatenate([x] * (width // TILE), axis=1)

    def stage_a(j, par):
        seq, q0, w0, variant = locate(j)
        bias = jnp.concatenate([bias_ref[variant]] * (rows_s // rows_b), axis=0)
        nt = (((1,), (1,)), ((), ()))
        if split_heads:
            for t in range(LANES // TILE):
                qs = _stack_head_pair(q_ref[seq, pl.ds(q0, mq), _tile(t)])
                s = lax.dot_general(qs, k_ref[seq, pl.ds(w0, kw), _tile(t)], nt, preferred_element_type=f32)
                rows = slice(2 * t * mq, (2 * t + 2) * mq)
                s_refs[par][rows, :] = s + bias[rows]
        else:
            qs = _stack_heads(q_ref[seq, pl.ds(q0, mq), :])
            s = lax.dot_general(qs, k_ref[seq, pl.ds(w0, kw), :], nt, preferred_element_type=f32)
            s_refs[par][...] = s + bias
        for r0 in range(0, rows_s, row_chunk):
            mx = jnp.max(s_refs[par][r0:r0 + row_chunk, :], axis=-1, keepdims=True)
            st_refs[par][ST_MAX, r0:r0 + row_chunk, :] = jnp.broadcast_to(mx, (row_chunk, TILE))

    def stage_b(par, p_ref, dead_tiles):
        for r0 in range(0, rows_s, row_chunk):
            mx = st_refs[par][ST_MAX, r0:r0 + row_chunk, :]
            live = [t for t in range(kw // TILE) if t not in dead_tiles(r0)]
            s = jnp.concatenate([s_refs[par][r0:r0 + row_chunk, _tile(t)] for t in live], axis=1)
            p = jnp.exp2(s - lanes_of(mx, s.shape[1]))
            if not mxu_row_sums:
                l = jnp.sum(p, axis=-1, keepdims=True)
                st_refs[par][ST_SUM, r0:r0 + row_chunk, :] = jnp.broadcast_to(l, (row_chunk, TILE))
                if with_lse:
                    st_refs[par][ST_MAX_KEPT, r0:r0 + row_chunk, :] = mx
            p = p.astype(p_ref.dtype)
            for i, t in enumerate(live):
                p_ref[r0:r0 + row_chunk, _tile(t)] = p[:, _tile(i)]

    def stage_c(j, par, p_ref):
        seq, q0, w0, _ = locate(j)
        if mxu_row_sums:
            first = _first_head_of_tile(mq)
            ones = jnp.ones((kw, TILE), p_ref.dtype)
            tiles = []
            for t in range(LANES // TILE):
                rhs = jnp.concatenate([v_ref[seq, pl.ds(w0, kw), _tile(t)], ones], axis=1)
                pv = jnp.dot(p_ref[2 * t * mq:(2 * t + 2) * mq, :], rhs, preferred_element_type=f32)
                den = jnp.where(first, pv[:mq, TILE:], pv[mq:, TILE:])
                tiles.append(jnp.where(first, pv[:mq, :TILE], pv[mq:, :TILE]) * (1.0 / den))
            o_ref[seq, pl.ds(q0, mq), :] = jnp.concatenate(tiles, axis=1).astype(o_ref.dtype)
            return
        if split_heads:
            first = _first_head_of_tile(mq)
            tiles = []
            for t in range(LANES // TILE):
                vwin = v_ref[seq, pl.ds(w0, kw), _tile(t)]
                pv = jnp.dot(p_ref[2 * t * mq:(2 * t + 2) * mq, :], vwin, preferred_element_type=f32)
                tiles.append(jnp.where(first, pv[:mq], pv[mq:]))
            out = jnp.concatenate(tiles, axis=1)
        else:
            pv = jnp.dot(p_ref[...], v_ref[seq, pl.ds(w0, kw), :], preferred_element_type=f32)
            out = _unstack_heads(pv, mq)
        l = _unstack_heads(st_refs[par][ST_SUM], mq)
        o_ref[seq, pl.ds(q0, mq), :] = (out * (1.0 / l)).astype(o_ref.dtype)
        if with_lse:
            lse = _unstack_heads(st_refs[par][ST_MAX_KEPT], mq) + jnp.log2(l)
            lse_ref[seq, pl.ds(q0, mq), :] = _compact_heads(lse)

    def no_dead_tiles(r0):
        return ()

    def steady_state(first_pair, dead_tiles):
        def body(i, carry):
            for par in (0, 1):
                j = 2 * i + par
                if mxu_row_sums:
                    stage_c(j - 2, par, p_refs[par])
                stage_a(j, par)
                stage_b(1 - par, p_refs[1 - par], dead_tiles)
                if not mxu_row_sums:
                    stage_c(j - 2, par, p_refs[par])
            return carry

        lax.fori_loop(first_pair, n_blocks // 2, body, 0)

    if interior_dead_tiles is None:
        stage_a(0, 0)
        stage_a(1, 1)
        stage_b(0, p_refs[0], no_dead_tiles)
        steady_state(1, no_dead_tiles)
        stage_b(1, p_refs[1], no_dead_tiles)
        stage_c(n_blocks - 2, 0, p_refs[0])
        stage_c(n_blocks - 1, 1, p_refs[1])
    else:
        p_edge = scratch[6]
        assert q_ref.shape[0] == 1 and n_blocks >= 6
        for p_ref in p_refs:
            for r0 in range(0, rows_s, row_chunk):
                for t in interior_dead_tiles(r0):
                    p_ref[r0:r0 + row_chunk, _tile(t)] = jnp.zeros((row_chunk, TILE), p_ref.dtype)
        stage_a(0, 0)
        stage_a(1, 1)
        stage_b(0, p_edge, no_dead_tiles)
        stage_a(2, 0)
        stage_b(1, p_refs[1], interior_dead_tiles)
        stage_c(0, 0, p_edge)
        stage_a(3, 1)
        stage_b(0, p_refs[0], interior_dead_tiles)
        stage_c(1, 1, p_refs[1])
        steady_state(2, interior_dead_tiles)
        stage_b(1, p_edge, no_dead_tiles)
        stage_c(n_blocks - 2, 0, p_refs[0])
        stage_c(n_blocks - 1, 1, p_edge)


def _windowed_attention(arrays, chunks, bias, bias_spec, grid, batch_of, chunk_of, out_chunks, *, name,
                        mq, kw, window_start, row_chunk, with_lse, split_heads, mxu_row_sums=False,
                        interior_dead_tiles=None, bias_builder=None, table_shape=None):
    b, _, r, sub_len, _ = arrays[0].shape
    blocks_per_seq = sub_len // mq

    def spec(chunk):
        return pl.BlockSpec((None, None, r, sub_len, LANES),
                            lambda *g: (batch_of(*g), chunk + chunk_of(*g), 0, 0, 0))

    out_shape = [jax.ShapeDtypeStruct((b, out_chunks, r, sub_len, LANES), jnp.bfloat16)]
    out_specs = [spec(0)]
    if with_lse:
        out_shape.append(jax.ShapeDtypeStruct((b, out_chunks, r, sub_len, TILE), jnp.float32))
        out_specs.append(pl.BlockSpec((None, None, r, sub_len, TILE),
                                      lambda *g: (batch_of(*g), chunk_of(*g), 0, 0, 0)))
    rows_s = HEADS_PER_CALL * mq
    scratch = ([pltpu.VMEM((rows_s, kw), jnp.float32)] * 2 + [pltpu.VMEM((rows_s, kw), jnp.bfloat16)] * 2
               + [pltpu.VMEM((3 if with_lse else 2, rows_s, TILE), jnp.float32)] * 2)
    if interior_dead_tiles is not None:
        scratch.append(pltpu.VMEM((rows_s, kw), jnp.bfloat16))
    if bias_builder is not None:
        scratch.append(pltpu.VMEM(table_shape, jnp.float32))
    return pl.pallas_call(
        functools.partial(_attn_kernel, mq=mq, kw=kw, blocks_per_seq=blocks_per_seq, window_start=window_start,
                          row_chunk=row_chunk, with_lse=with_lse, interior_dead_tiles=interior_dead_tiles,
                          split_heads=split_heads, mxu_row_sums=mxu_row_sums, bias_builder=bias_builder),
        out_shape=out_shape,
        grid=grid,
        in_specs=[spec(c) for c in chunks] + [bias_spec],
        out_specs=out_specs,
        scratch_shapes=scratch,
        compiler_params=pltpu.CompilerParams(
            dimension_semantics=("arbitrary",) * len(grid), vmem_limit_bytes=VMEM_LIMIT),
        name=name,
    )(*arrays, bias)


def _na_attention(nat, rpb):
    b, _, _, seq, _ = nat.shape
    rows = seq // GRID_W
    n_groups = NA_HEADS // HEADS_PER_CALL
    m_q = NA_Q_ROWS * GRID_W
    _, n_ro, n_co = rpb.shape
    rpb_pad = jnp.pad(rpb.astype(jnp.float32), ((0, 0), (0, 16 - n_ro), (0, TILE - n_co)))
    for blk in range(1, rows // NA_Q_ROWS - 1):
        r0 = blk * NA_Q_ROWS
        assert 0 <= r0 - NA_WIN_ROWS // 2 <= rows - NA_K_ROWS
        assert r0 + NA_Q_ROWS - 1 - NA_WIN_ROWS // 2 <= rows - NA_WIN_ROWS

    def interior_dead_tiles(row):
        qr = (row % m_q) // GRID_W
        per_tile = TILE // GRID_W
        return tuple(t for t in range(NA_K_ROWS // per_tile)
                     if not any(0 <= kr - qr < NA_WIN_ROWS for kr in range(t * per_tile, (t + 1) * per_tile)))

    (out,) = _windowed_attention(
        (nat, nat, nat), (NA_Q0, NA_K0, NA_V0), rpb_pad,
        pl.BlockSpec((HEADS_PER_CALL, 16, TILE), lambda g, bi: (g, 0, 0)),
        grid=(n_groups, b), batch_of=lambda g, bi: bi, chunk_of=lambda g, bi: g, out_chunks=n_groups,
        name="na_attention", mq=m_q, kw=NA_K_ROWS * GRID_W,
        window_start=lambda n: _na_window_start(n * NA_Q_ROWS, rows) * GRID_W, row_chunk=32, with_lse=False,
        split_heads=False, mxu_row_sums=True, interior_dead_tiles=interior_dead_tiles,
        bias_builder=functools.partial(_na_bias_build, rows=rows),
        table_shape=(3, HEADS_PER_CALL * m_q, NA_K_ROWS * GRID_W))
    return out.reshape(b, n_groups, seq, LANES)


def _dil_mask_tables():
    i = np.arange(DIL_Q)[:, None]
    j = np.arange(DIL_K)[None, :]
    tabs = []
    for off in (0, DIL_RADIUS, DIL_K - DIL_Q):
        tabs.append(np.where(np.abs(off + i - j) <= DIL_RADIUS, 0.0, NEG_INF))
    return jnp.asarray(np.stack(tabs), dtype=jnp.float32)


def _dil_attention(arr, mask, chunk0):
    b, _, dil, sub_len, _ = arr.shape
    o, lse = _windowed_attention(
        (arr, arr, arr), (chunk0, chunk0 + 1, chunk0 + 2), mask,
        pl.BlockSpec(mask.shape, lambda bi: (0, 0, 0)),
        grid=(b,), batch_of=lambda bi: bi, chunk_of=lambda bi: 0, out_chunks=1,
        name=f"dil_attention_d{dil}", mq=DIL_Q, kw=DIL_K,
        window_start=lambda n: jnp.clip(n * DIL_Q - DIL_RADIUS, 0, sub_len - DIL_K), row_chunk=DIL_Q, with_lse=True,
        split_heads=True)
    return o.reshape(b, dil, sub_len, LANES), lse.reshape(b, dil, sub_len, TILE)


def _post_kernel(x_ref, yna_ref, o1_ref, o2_ref, o3_ref, l1_ref, l2_ref, l3_ref, p_ref,
                 gmix_ref, gmlp_ref, gple_ref, gfin_ref,
                 win_hbm, wbna_hbm, wbdil_hbm, wout_hbm, wup_hbm, wdown_hbm, wpg_hbm, wpp_hbm,
                 out_ref, *scratch, layer, final_norm, ff_chunk, sub_rows):
    f32, bf16 = jnp.float32, jnp.bfloat16
    bm, d = x_ref.shape
    stage_refs = scratch[:-8]
    wgate_ref, wbna_ref, wbdil_ref, wout_ref, wup_ref, wdown_ref, wpg_ref, wpp_ref = scratch[-8:]
    d_ff = wup_ref.shape[1]

    @pl.when(_is_first_step())
    def _():
        _load_weights_bf16(layer, [(win_hbm, QKV_WIDTH, wgate_ref)])
        _load_weights_bf16(layer, [(wbna_hbm, 0, wbna_ref), (wbdil_hbm, 0, wbdil_ref), (wout_hbm, 0, wout_ref),
                                   (wdown_hbm, 0, wdown_ref), (wpg_hbm, 0, wpg_ref), (wpp_hbm, 0, wpp_ref)])
        _load_weights_bf16(layer, [(wup_hbm, 0, wup_ref)])

    stages = iter(stage_refs)
    token_order = []
    for ref in (o1_ref, o2_ref, o3_ref, l1_ref, l2_ref, l3_ref):
        dil = ref.shape[0]
        if dil == 1:
            token_order.append(lambda r0, ref=ref: ref[0, r0:r0 + sub_rows, :].astype(f32))
            continue
        stage_ref = next(stages)
        halves = ref.shape[-1] // TILE
        for rho in range(dil):
            blk = ref[rho].astype(f32)
            for half in range(halves):
                stage_ref[half, pl.ds(rho, bm // dil, stride=dil), :] = blk[:, _tile(half)]
        token_order.append(lambda r0, stage_ref=stage_ref, halves=halves: jnp.concatenate(
            [stage_ref[half, r0:r0 + sub_rows, :] for half in range(halves)], axis=1))

    def dot(lhs, w):
        return jnp.dot(lhs, w, preferred_element_type=f32)

    chains = [dict(r0=r0) for r0 in range(0, bm, sub_rows)]
    for c in chains:
        r0 = c["r0"]
        c["x"] = x_ref[r0:r0 + sub_rows, :]
        a = _rms(c["x"], gmix_ref[...]).astype(bf16)
        c["gate_na"] = _sigmoid(dot(a, wgate_ref[:, :d]))
        c["gate_dil"] = _sigmoid(dot(a, wgate_ref[:, d:]))
    for c in chains:
        r0 = c["r0"]
        o1, o2, o3, l1, l2, l3 = [get(r0) for get in token_order]
        mx = jnp.maximum(jnp.maximum(l1, l2), l3)
        e1, e2, e3 = jnp.exp2(l1 - mx), jnp.exp2(l2 - mx), jnp.exp2(l3 - mx)
        inv = 1.0 / (e1 + e2 + e3)
        ydil = (_expand_heads(e1 * inv) * o1 + _expand_heads(e2 * inv) * o2 + _expand_heads(e3 * inv) * o3)
        yna = jnp.concatenate([yna_ref[g, r0:r0 + sub_rows, :] for g in range(yna_ref.shape[0])], axis=1)
        c["mixed"] = (c.pop("gate_na") * dot(yna, wbna_ref[...])
                      + c.pop("gate_dil") * dot(ydil.astype(bf16), wbdil_ref[...]))
    for c in chains:
        c["h"] = c.pop("x") + dot(c.pop("mixed").astype(bf16), wout_ref[...])
        c["c"] = _rms(c["h"], gmlp_ref[...]).astype(bf16)
        c["acc"] = jnp.zeros_like(c["h"])
    for f in range(d_ff // ff_chunk):
        for c in chains:
            u = dot(c["c"], wup_ref[:, f * ff_chunk:(f + 1) * ff_chunk])
            c["u"] = jnp.square(jnp.maximum(u, 0.0)).astype(bf16)
        for c in chains:
            c["acc"] = c["acc"] + dot(c.pop("u"), wdown_ref[f * ff_chunk:(f + 1) * ff_chunk, :])
    for c in chains:
        r0 = c["r0"]
        c["h"] = c["h"] + c.pop("acc")
        e = _rms(c["h"], gple_ref[...]).astype(bf16)
        c["pg"] = _sigmoid(dot(e, wpg_ref[...]))
        c["pp"] = dot(p_ref[r0:r0 + sub_rows, :].astype(bf16), wpp_ref[...])
    for c in chains:
        r0 = c["r0"]
        h = c["h"] + c["pg"] * c["pp"]
        if final_norm:
            h = _rms(h, gfin_ref[...])
        out_ref[r0:r0 + sub_rows, :] = h


def _post_block(x3, yna, outs, lses, p3, gains, weights, layer, final_norm, bm=512, sub_rows=256, ff_chunk=1024):
    b, s, d = x3.shape
    w_in = weights[0]
    resident = [(d, w_in.shape[2] - QKV_WIDTH)] + [w.shape[1:] for w in weights[1:]]

    def rows(arr):
        return pl.BlockSpec((None, bm, arr.shape[-1]), lambda bi, t: (bi, t, 0))

    def residues(arr):
        lead = arr.shape[1]
        rows_per = bm if arr is yna else bm // lead
        return pl.BlockSpec((None, lead, rows_per, arr.shape[-1]), lambda bi, t: (bi, 0, t, 0))

    def whole(arr):
        return pl.BlockSpec(arr.shape, lambda bi, t: (0, 0))

    n_stages = sum(a.shape[1] > 1 for a in (*outs, *lses))
    return pl.pallas_call(
        functools.partial(_post_kernel, layer=layer, final_norm=final_norm, ff_chunk=ff_chunk, sub_rows=sub_rows),
        out_shape=jax.ShapeDtypeStruct((b, s, d), jnp.float32),
        grid=(b, s // bm),
        in_specs=([rows(x3), residues(yna)] + [residues(a) for a in (*outs, *lses)] + [rows(p3)]
                  + [whole(g) for g in gains] + [pl.BlockSpec(memory_space=pl.ANY)] * len(weights)),
        out_specs=rows(x3),
        scratch_shapes=([pltpu.VMEM((LANES // TILE, bm, TILE), jnp.float32)] * n_stages
                        + [pltpu.VMEM(shape, jnp.bfloat16) for shape in resident]),
        compiler_params=pltpu.CompilerParams(
            dimension_semantics=("arbitrary",) * 2, vmem_limit_bytes=VMEM_LIMIT),
        name="post_block",
    )(x3, yna, *outs, *lses, p3, *gains, *weights)


def kernel(x, p, positions, g_mix, w_in, rpb, w_branch_na, w_branch_dil, w_out, g_mlp, w_up, w_down,
           g_ple, w_ple_gate, w_ple_proj, g_final):
    b, s, d = x.shape
    depth = w_in.shape[0]
    dil_mask = _dil_mask_tables()
    h = x
    for i in range(depth):
        nat, *dil_arrays = _qkv_project(h, g_mix[i].reshape(1, d), w_in, i, positions)
        yna = _na_attention(nat, rpb[i])
        outs, lses = [], []
        for g, (window, dil) in enumerate(DIL_GROUPS):
            assert window // (2 * dil) == DIL_RADIUS
            if dil == 1:
                o, lse = _dil_attention(nat, dil_mask, DIL_Q0)
            else:
                o, lse = _dil_attention(dil_arrays[g - 1], dil_mask, 0)
            outs.append(o)
            lses.append(lse)
        gains = [g_mix[i].reshape(1, d), g_mlp[i].reshape(1, d), g_ple[i].reshape(1, d), g_final.reshape(1, d)]
        weights = [w_in, w_branch_na, w_branch_dil, w_out, w_up, w_down, w_ple_gate, w_ple_proj]
        h = _post_block(h, yna, outs, lses, p[i], gains, weights, layer=i, final_norm=(i == depth - 1))
    return h
```
